```python
import math
import jax
import jax.numpy as jnp
from jax import lax
import numpy as np

D_MODEL = 2048
BATCH = 1
SEQ = 8192
DEPTH = 2
DEC_BATCH = 32
DEC_SEQ = 4
PAST_LEN = 8192
PAGE_SIZE = 128

N_EVEN = (DEPTH + 1) // 2
N_ODD = DEPTH // 2
MIX_W = D_MODEL
HALF_W = MIX_W // 2
D_FF = 4 * D_MODEL
A_HEADS = 8
A_DK = HALF_W // A_HEADS
A_DV = HALF_W // A_HEADS
A_CONV = 4
A_CHUNK = 64
A_CONV_CH = A_HEADS * (2 * A_DK + A_DV)
B_HEADS = 8
B_HD = HALF_W // B_HEADS
B_KV_HEADS = 2
IDX_HEADS = 8
IDX_HD = 64
TOPK_MAX = 256
C_HEADS = 16
C_HD = MIX_W // (2 * C_HEADS)
C_DV = 2 * C_HD
ROPE_THETA = 10000.0
EPS = 1e-6
Q_BLOCK = 128
F32 = jnp.float32
AB_SIZES = (A_HEADS * A_DK, A_HEADS * A_DK, A_HEADS * A_DV, A_HEADS * A_DV, A_HEADS, A_HEADS,
            B_HEADS * B_HD, B_KV_HEADS * B_HD, B_KV_HEADS * B_HD, IDX_HEADS * IDX_HD, IDX_HD, IDX_HEADS)
AB_IN = sum(AB_SIZES)
AB_OFFSETS = tuple(sum(AB_SIZES[:n]) for n in range(1, len(AB_SIZES)))
AB_OUT_IN = A_HEADS * A_DV + B_HEADS * B_HD
C_QW = C_HEADS * 2 * C_HD
C_IN = 2 * C_QW + C_HEADS * C_DV

kernel_name = 'hybrid_gdn_dsa_diffattn_decoder_step'


def rmsnorm(x, g):
    xf = x.astype(F32)
    y = xf * lax.rsqrt(jnp.mean(xf * xf, axis=-1, keepdims=True) + EPS)
    return (y * g.astype(F32)).astype(x.dtype)


def l2norm(x):
    return x * lax.rsqrt(jnp.sum(x * x, axis=-1, keepdims=True) + 1e-6)


def rope(x, pos):
    half = x.shape[-1] // 2
    inv = ROPE_THETA ** (-jnp.arange(half, dtype=F32) / half)
    ang = pos.astype(F32)[:, None] * inv[None, :]
    shp = (pos.shape[0],) + (1,) * (x.ndim - 3) + (half,)
    cos = jnp.cos(ang).reshape(shp)
    sin = jnp.sin(ang).reshape(shp)
    xf = x.astype(F32)
    x1, x2 = xf[..., :half], xf[..., half:]
    return jnp.concatenate([x1 * cos - x2 * sin, x2 * cos + x1 * sin], axis=-1).astype(x.dtype)


def to_blocks(a):
    b, t = a.shape[:2]
    return jnp.swapaxes(a.reshape((b, t // Q_BLOCK, Q_BLOCK) + a.shape[2:]), 0, 1)


def from_blocks(a):
    nb, b, qb = a.shape[:3]
    return jnp.swapaxes(a, 0, 1).reshape((b, nb * qb) + a.shape[3:])


def causal_conv(x, buf, w):
    t = x.shape[1]
    xp = jnp.concatenate([buf.astype(x.dtype), x], axis=1)
    y = sum(xp[:, j:j + t] * w[j] for j in range(A_CONV))
    return jax.nn.silu(y), xp[:, t:]


def gated_delta_chunked(q, k, v, g, beta, s0):
    bsz, t, nh, dk = q.shape
    c = min(A_CHUNK, t)
    pad = (-t) % c
    if pad:
        padt = lambda a: jnp.pad(a, [(0, 0), (0, pad)] + [(0, 0)] * (a.ndim - 2))
        q, k, v, g, beta = (padt(a) for a in (q, k, v, g, beta))
    n = (t + pad) // c

    def chunks(a):
        return jnp.moveaxis(a.reshape((bsz, n, c) + a.shape[2:]), (1, 3), (0, 2))

    q, k, v, g, beta = (chunks(a) for a in (q, k, v, g, beta))
    q = q * dk ** -0.5
    kb = k * beta[..., None]
    vb = v * beta[..., None]
    gc = jnp.cumsum(g, axis=-1)
    incl = jnp.tril(jnp.ones((c, c), bool))
    strict = jnp.tril(jnp.ones((c, c), bool), -1)
    decay = jnp.exp(jnp.where(incl, gc[..., :, None] - gc[..., None, :], -jnp.inf))
    a_mat = jnp.where(strict, jnp.einsum('nbhid,nbhjd->nbhij', kb, k) * decay, 0.0)
    eye = jnp.eye(c, dtype=F32)
    t_inv = lax.linalg.triangular_solve(eye + a_mat, jnp.broadcast_to(eye, a_mat.shape),
                                        left_side=True, lower=True, unit_diagonal=True)
    u = t_inv @ vb
    w = t_inv @ (kb * jnp.exp(gc)[..., None])
    qk = jnp.einsum('nbhid,nbhjd->nbhij', q, k) * decay
    q_dec = q * jnp.exp(gc)[..., None]
    k_dec = k * jnp.exp(gc[..., -1:] - gc)[..., None]
    g_last = jnp.exp(gc[..., -1])

    def step(s, xs):
        q_i, qk_i, u_i, w_i, k_i, gl = xs
        v_new = u_i - jnp.einsum('bhck,bhkv->bhcv', w_i, s)
        o = jnp.einsum('bhck,bhkv->bhcv', q_i, s) + jnp.einsum('bhij,bhjv->bhiv', qk_i, v_new)
        s = s * gl[..., None, None] + jnp.einsum('bhck,bhcv->bhkv', k_i, v_new)
        return s, o

    s, o = lax.scan(step, s0, (q_dec, qk, u, w, k_dec, g_last))
    o = jnp.moveaxis(o, (0, 2), (1, 3)).reshape(bsz, n * c, nh, -1)[:, :t]
    return o, s


def indexer_scores(qi, wi, ki, q_pos):
    rel = jax.nn.relu(jnp.einsum('bthd,bsd->bths', qi.astype(F32), ki.astype(F32)) * IDX_HD ** -0.5)
    s = jnp.einsum('bths,bth->bts', rel, wi.astype(F32))
    mask = jnp.arange(ki.shape[1])[None, :] <= q_pos[:, None]
    return jnp.where(mask, s, -jnp.inf)


def sparse_attend(q, ks, vs, valid):
    bsz, t, nh, d = q.shape
    qg = q.reshape(bsz, t, B_KV_HEADS, nh // B_KV_HEADS, d)
    s = jnp.einsum('bthgd,btjhd->bthgj', qg, ks).astype(F32) * d ** -0.5
    p = jax.nn.softmax(jnp.where(valid[:, :, None, None, :], s, -jnp.inf), axis=-1).astype(vs.dtype)
    return jnp.einsum('bthgj,btjhd->bthgd', p, vs).reshape(bsz, t, nh, d)


def dsa_prompt(q, k, v, qi, ki, wi, pos):
    t = q.shape[1]
    n_sel = min(TOPK_MAX, t // 4)
    take = jax.vmap(lambda a, i: a[i])

    def one(args):
        qb, qib, wib, pb = args
        _, idx = lax.top_k(indexer_scores(qib, wib, ki, pb), n_sel)
        valid = idx <= pb[None, :, None]
        return sparse_attend(qb, take(k, idx), take(v, idx), valid)

    o = lax.map(one, (to_blocks(q), to_blocks(qi), to_blocks(wi), pos.reshape(-1, Q_BLOCK)))
    return from_blocks(o)


def dsa_sample(q, k, v, qi, ki, wi, pos, pool_k, pool_v, pool_ki, page_table, j):
    bsz, t = q.shape[:2]
    n_sel = min(TOPK_MAX, (PAST_LEN + t) // 4)
    ki_past = pool_ki[j, page_table].reshape(bsz, PAST_LEN, IDX_HD)
    ki_all = jnp.concatenate([ki_past.astype(ki.dtype), ki], axis=1)
    _, idx = lax.top_k(indexer_scores(qi, wi, ki_all, pos), n_sel)
    valid = idx <= pos[None, :, None]
    is_new = idx >= PAST_LEN
    idx_past = jnp.minimum(idx, PAST_LEN - 1)
    phys = jax.vmap(lambda pt, i: pt[i])(page_table, idx_past // PAGE_SIZE)
    off = idx_past % PAGE_SIZE
    idx_new = jnp.clip(idx - PAST_LEN, 0, t - 1)
    take = jax.vmap(lambda a, i: a[i])

    def select(pool, new):
        return jnp.where(is_new[..., None, None], take(new, idx_new), pool[j, phys, off].astype(new.dtype))

    return sparse_attend(q, select(pool_k, k), select(pool_v, v), valid)


def diff_core(q, k, v, mask, lam):
    d = q.shape[-1]
    s = jnp.einsum('bthpd,bshpd->bhpts', q, k).astype(F32) * d ** -0.5
    p = jax.nn.softmax(jnp.where(mask, s, -jnp.inf), axis=-1)
    a = p[:, :, 0] - lam * p[:, :, 1]
    return jnp.einsum('bhts,bshe->bthe', a.astype(v.dtype), v)


def diff_prompt(q, k, v, pos, lam):
    kpos = jnp.arange(k.shape[1])

    def one(args):
        qb, pb = args
        return diff_core(qb, k, v, kpos[None, :] <= pb[:, None], lam)

    return from_blocks(lax.map(one, (to_blocks(q), pos.reshape(-1, Q_BLOCK))))


def diff_sample(q, k, v, pos, lam, pool_k, pool_v, page_table, j):
    t = q.shape[1]
    mask = jnp.arange(PAST_LEN + t)[None, :] <= pos[:, None]

    def one(args):
        qb, kb, vb, pt = args
        k_all = jnp.concatenate([pool_k[j, pt].reshape((PAST_LEN,) + kb.shape[1:]).astype(kb.dtype), kb], axis=0)
        v_all = jnp.concatenate([pool_v[j, pt].reshape((PAST_LEN,) + vb.shape[1:]).astype(vb.dtype), vb], axis=0)
        return diff_core(qb[None], k_all[None], v_all[None], mask, lam)[0]

    return lax.map(one, (q, k, v, page_table))


def ab_mixer(h, pos, w_in, w_out, conv_w, a_log, dt_bias, a_norm_g, conv_buf, s0, past):
    bsz, t, _ = h.shape
    qa, ka, va, za, ba, aa, qb, kb, vb, qi, ki, wi = jnp.split(h @ w_in, AB_OFFSETS, axis=-1)
    qkv, new_buf = causal_conv(jnp.concatenate([qa, ka, va], axis=-1), conv_buf, conv_w)
    qa, ka, va = jnp.split(qkv.astype(F32), (A_HEADS * A_DK, 2 * A_HEADS * A_DK), axis=-1)
    qa = l2norm(qa.reshape(bsz, t, A_HEADS, A_DK))
    ka = l2norm(ka.reshape(bsz, t, A_HEADS, A_DK))
    va = va.reshape(bsz, t, A_HEADS, A_DV)
    beta = jax.nn.sigmoid(ba.astype(F32))
    g = -jnp.exp(a_log.astype(F32)) * jax.nn.softplus(aa.astype(F32) + dt_bias.astype(F32))
    oa, s_new = gated_delta_chunked(qa, ka, va, g, beta, s0.astype(F32))
    oa = rmsnorm(oa, a_norm_g).astype(h.dtype) * jax.nn.silu(za.reshape(bsz, t, A_HEADS, A_DV))
    qb = rope(qb.reshape(bsz, t, B_HEADS, B_HD), pos)
    kb = rope(kb.reshape(bsz, t, B_KV_HEADS, B_HD), pos)
    vb = vb.reshape(bsz, t, B_KV_HEADS, B_HD)
    qi = rope(qi.reshape(bsz, t, IDX_HEADS, IDX_HD), pos)
    ki = rope(ki[:, :, None, :], pos)[:, :, 0]
    wi = wi * IDX_HEADS ** -0.5
    if past is None:
        ob = dsa_prompt(qb, kb, vb, qi, ki, wi, pos)
    else:
        ob = dsa_sample(qb, kb, vb, qi, ki, wi, pos, *past)
    out = jnp.concatenate([oa.reshape(bsz, t, -1), ob.reshape(bsz, t, -1)], axis=-1) @ w_out
    return out, (kb, vb, ki, s_new.astype(h.dtype), new_buf)


def c_mixer(h, pos, w_in, w_out, lam_p, c_norm_g, lam_init, past):
    bsz, t, _ = h.shape
    q, k, v = jnp.split(h @ w_in, (C_QW, 2 * C_QW), axis=-1)
    q = rope(q.reshape(bsz, t, C_HEADS, 2, C_HD), pos)
    k = rope(k.reshape(bsz, t, C_HEADS, 2, C_HD), pos)
    v = v.reshape(bsz, t, C_HEADS, C_DV)
    lp = lam_p.astype(F32)
    lam = jnp.exp(jnp.sum(lp[0] * lp[1])) - jnp.exp(jnp.sum(lp[2] * lp[3])) + lam_init
    if past is None:
        o = diff_prompt(q, k, v, pos, lam)
    else:
        o = diff_sample(q, k, v, pos, lam, *past)
    o = rmsnorm(o, c_norm_g) * (1.0 - lam_init)
    return o.reshape(bsz, t, C_HEADS * C_DV) @ w_out, (k, v)


def trunk(x, c, pos, cache, P):
    bsz = x.shape[0]
    names = ('ab_k', 'ab_v', 'ab_kidx', 'ab_delta', 'ab_conv', 'c_k', 'c_v')
    new = {n: [] for n in names}
    for i in range(DEPTH):
        j = i // 2
        mod = (jax.nn.silu(c) @ P['ada_w'][i] + P['ada_b'][i])[:, None, :]
        sh1, sc1, g1, sh2, sc2, g2 = jnp.split(mod, 6, axis=-1)
        h = rmsnorm(x, P['norm_g'][i, 0]) * (1.0 + sc1) + sh1
        if i % 2 == 0:
            if cache is None:
                conv_buf = jnp.zeros((bsz, A_CONV - 1, A_CONV_CH), x.dtype)
                s0 = jnp.zeros((bsz, A_HEADS, A_DK, A_DV), F32)
                past = None
            else:
                conv_buf = cache['ab_conv'][j]
                s0 = cache['ab_delta'][j]
                past = (cache['ab_k'], cache['ab_v'], cache['ab_kidx'], cache['page_table'], j)
            m, outs = ab_mixer(h, pos, P['ab_w_in'][j], P['ab_w_out'][j], P['ab_conv_w'][j], P['ab_A_log'][j],
                               P['ab_dt_bias'][j], P['ab_norm_g'][j], conv_buf, s0, past)
            for n, a in zip(names[:5], outs):
                new[n].append(a)
        else:
            past = None if cache is None else (cache['c_k'], cache['c_v'], cache['page_table'], j)
            lam_init = 0.8 - 0.6 * math.exp(-0.3 * i)
            m, (kc, vc) = c_mixer(h, pos, P['c_w_in'][j], P['c_w_out'][j], P['c_lambda'][j], P['c_norm_g'][j],
                                  lam_init, past)
            new['c_k'].append(kc)
            new['c_v'].append(vc)
        x = x + g1 * rmsnorm(m, P['norm_g'][i, 1])
        h = rmsnorm(x, P['norm_g'][i, 2]) * (1.0 + sc2) + sh2
        f = jnp.square(jax.nn.relu(h @ P['mlp_w1'][i])) @ P['mlp_w2'][i]
        x = x + g2 * rmsnorm(f, P['norm_g'][i, 3])
    return x, {n: jnp.stack(a) for n, a in new.items()}


def setup_inputs(seed: int = 0) -> dict:
    key = jax.random.key(seed)
    ks = jax.random.split(key, 32)
    n_pages = PAST_LEN // PAGE_SIZE
    n_used = DEC_BATCH * n_pages
    n_pool = n_used + max(1, n_used // 4)

    def nrm(k, shape, scale=1.0):
        return scale * jax.random.normal(k, shape, F32)

    page_table = jax.random.permutation(ks[0], n_pool)[:n_used].reshape(DEC_BATCH, n_pages).astype(jnp.int32)
    dt = jnp.exp(jax.random.uniform(ks[1], (N_EVEN, A_HEADS), F32, math.log(1e-3), math.log(1e-1)))
    return {
        'x_prompt': nrm(ks[2], (BATCH, SEQ, D_MODEL)),
        'x_sample': nrm(ks[3], (DEC_BATCH, DEC_SEQ, D_MODEL)),
        'cache_ab_k': nrm(ks[4], (N_EVEN, n_pool, PAGE_SIZE, B_KV_HEADS, B_HD)),
        'cache_ab_v': nrm(ks[5], (N_EVEN, n_pool, PAGE_SIZE, B_KV_HEADS, B_HD)),
        'cache_ab_kidx': nrm(ks[6], (N_EVEN, n_pool, PAGE_SIZE, IDX_HD)),
        'state_ab_delta': nrm(ks[7], (N_EVEN, DEC_BATCH, A_HEADS, A_DK, A_DV), 0.1),
        'state_ab_conv': nrm(ks[8], (N_EVEN, DEC_BATCH, A_CONV - 1, A_CONV_CH)),
        'cache_c_k': nrm(ks[9], (N_ODD, n_pool, PAGE_SIZE, C_HEADS, 2, C_HD)),
        'cache_c_v': nrm(ks[10], (N_ODD, n_pool, PAGE_SIZE, C_HEADS, C_DV)),
        'page_table': page_table,
        'c_prompt': nrm(ks[11], (BATCH, D_MODEL)),
        'c_sample': nrm(ks[12], (DEC_BATCH, D_MODEL)),
        'ada_w': nrm(ks[13], (DEPTH, D_MODEL, 6 * D_MODEL), 0.5 * D_MODEL ** -0.5),
        'ada_b': nrm(ks[14], (DEPTH, 6 * D_MODEL), 0.02),
        'norm_g': 1.0 + nrm(ks[15], (DEPTH, 4, D_MODEL), 0.02),
        'mlp_w1': nrm(ks[16], (DEPTH, D_MODEL, D_FF), D_MODEL ** -0.5),
        'mlp_w2': nrm(ks[17], (DEPTH, D_FF, D_MODEL), D_FF ** -0.5),
        'ab_w_in': nrm(ks[18], (N_EVEN, D_MODEL, AB_IN), D_MODEL ** -0.5),
        'ab_w_out': nrm(ks[19], (N_EVEN, AB_OUT_IN, D_MODEL), AB_OUT_IN ** -0.5),
        'ab_conv_w': nrm(ks[20], (N_EVEN, A_CONV, A_CONV_CH), A_CONV ** -0.5),
        'ab_A_log': jnp.log(jax.random.uniform(ks[21], (N_EVEN, A_HEADS), F32, 1.0, 16.0)),
        'ab_dt_bias': dt + jnp.log(-jnp.expm1(-dt)),
        'ab_norm_g': 1.0 + nrm(ks[22], (N_EVEN, A_DV), 0.02),
        'c_w_in': nrm(ks[23], (N_ODD, D_MODEL, C_IN), D_MODEL ** -0.5),
        'c_w_out': nrm(ks[24], (N_ODD, C_HEADS * C_DV, D_MODEL), (C_HEADS * C_DV) ** -0.5),
        'c_lambda': nrm(ks[25], (N_ODD, 4, C_HD), 0.1),
        'c_norm_g': 1.0 + nrm(ks[26], (N_ODD, C_DV), 0.02),
    }


def reference(x_prompt, x_sample, cache_ab_k, cache_ab_v, cache_ab_kidx, state_ab_delta, state_ab_conv,
              cache_c_k, cache_c_v, page_table, c_prompt, c_sample, ada_w, ada_b, norm_g, mlp_w1, mlp_w2,
              ab_w_in, ab_w_out, ab_conv_w, ab_A_log, ab_dt_bias, ab_norm_g, c_w_in, c_w_out, c_lambda, c_norm_g):
    P = {'ada_w': ada_w, 'ada_b': ada_b, 'norm_g': norm_g, 'mlp_w1': mlp_w1, 'mlp_w2': mlp_w2,
         'ab_w_in': ab_w_in, 'ab_w_out': ab_w_out, 'ab_conv_w': ab_conv_w, 'ab_A_log': ab_A_log,
         'ab_dt_bias': ab_dt_bias, 'ab_norm_g': ab_norm_g, 'c_w_in': c_w_in, 'c_w_out': c_w_out,
         'c_lambda': c_lambda, 'c_norm_g': c_norm_g}
    pos_p = jnp.arange(x_prompt.shape[1], dtype=jnp.int32)
    pos_s = PAST_LEN + jnp.arange(x_sample.shape[1], dtype=jnp.int32)
    y_prompt, np_ = trunk(x_prompt, c_prompt, pos_p, None, P)
    cache = {'ab_k': cache_ab_k, 'ab_v': cache_ab_v, 'ab_kidx': cache_ab_kidx, 'ab_delta': state_ab_delta,
             'ab_conv': state_ab_conv, 'c_k': cache_c_k, 'c_v': cache_c_v, 'page_table': page_table}
    y_sample, ns = trunk(x_sample, c_sample, pos_s, cache, P)
    return (y_prompt, y_sample,
            np_['ab_k'], np_['ab_v'], np_['ab_kidx'], np_['ab_delta'], np_['ab_conv'], np_['c_k'], np_['c_v'],
            ns['ab_k'], ns['ab_v'], ns['ab_kidx'], ns['ab_delta'], ns['ab_conv'], ns['c_k'], ns['c_v'])
```

```python
import functools
import math

import jax
import jax.numpy as jnp
from jax import lax
from jax.experimental import pallas as pl
from jax.experimental.pallas import tpu as pltpu

F32 = jnp.float32
BF16 = jnp.bfloat16
I32 = jnp.int32

EPS = 1e-6
NEG_BIG = -1e30
ROPE_THETA = 10000.0
PAGE = 128
A_HEADS = 8
A_DK = 128
A_CONV = 4
B_HEADS = 8
B_KV = 2
B_HD = 128
IDX_HEADS = 8
IDX_HD = 64
TOPK = 256
C_HEADS = 16
C_HD = 64
LANES = 128

NN = (((1,), (0,)), ((), ()))
NT = (((1,), (1,)), ((), ()))
TN = (((0,), (0,)), ((), ()))
INT_MIN = -2147483648
KEY_NEG_INF = -2139095041


def _cp(dims, vmem_mb=None):
    kw = dict(dimension_semantics=dims)
    if vmem_mb is not None:
        kw["vmem_limit_bytes"] = vmem_mb << 20
    return pltpu.CompilerParams(**kw)


def _dot(a, b, dims=NN):
    return lax.dot_general(a, b, dims, preferred_element_type=F32)


def _split2(a):
    hi = a.astype(BF16)
    return hi, (a - hi.astype(F32)).astype(BF16)


def _split3(a):
    hi = a.astype(BF16)
    r = a - hi.astype(F32)
    mid = r.astype(BF16)
    return hi, mid, (r - mid.astype(F32)).astype(BF16)


def _dot1(a, b, dims=NN):
    return _dot(a.astype(BF16), b.astype(BF16), dims)


def _dot3(a, b, dims=NN):
    ah, al = _split2(a)
    bh, bl = _split2(b)
    return _dot(ah, bh, dims) + (_dot(ah, bl, dims) + _dot(al, bh, dims))


def _dot_exact_left(ones_bf16, b, dims=NN):
    b0, b1, b2 = _split3(b)
    return _dot(ones_bf16, b0, dims) + (_dot(ones_bf16, b1, dims) + _dot(ones_bf16, b2, dims))


def _sigmoid(x):
    return 1.0 / (1.0 + jnp.exp(-x))


def _silu(x):
    return x * _sigmoid(x)


def _ada_kernel(c_ref, w_ref, b_ref, o_ref):
    o_ref[0] = _dot3(_silu(c_ref[...]), w_ref[0]) + b_ref[0]


def _ada(c_all, ada_w, ada_b, tn=512):
    nl, d, n = ada_w.shape
    mc = c_all.shape[0]
    return pl.pallas_call(
        _ada_kernel,
        grid=(nl, n // tn),
        in_specs=[pl.BlockSpec((mc, d), lambda l, j: (0, 0)),
                  pl.BlockSpec((1, d, tn), lambda l, j: (l, 0, j)),
                  pl.BlockSpec((1, 1, tn), lambda l, j: (l, 0, j))],
        out_specs=pl.BlockSpec((1, mc, tn), lambda l, j: (l, 0, j)),
        out_shape=jax.ShapeDtypeStruct((nl, mc, n), F32),
        compiler_params=_cp(("arbitrary", "arbitrary"), 40),
        name="ada_mod",
    )(c_all, ada_w, ada_b.reshape(nl, 1, n))


def _norm_mm_kernel(x_ref, g_ref, sc_ref, sh_ref, w_ref, o_ref, *scratch, hi, relu2):
    hh_ref = scratch[0]

    @pl.when(pl.program_id(1) == 0)
    def _():
        x = x_ref[...]
        y = x * lax.rsqrt(jnp.mean(x * x, axis=-1, keepdims=True) + EPS)
        h = (y * g_ref[...]) * (1.0 + sc_ref[...]) + sh_ref[...]
        hh = h.astype(BF16)
        hh_ref[...] = hh
        if hi:
            scratch[1][...] = (h - hh.astype(F32)).astype(BF16)

    w = w_ref[...]
    wh = w.astype(BF16)
    acc = _dot(hh_ref[...], wh)
    if hi:
        wl = (w - wh.astype(F32)).astype(BF16)
        acc = acc + (_dot(hh_ref[...], wl) + _dot(scratch[1][...], wh))
    if relu2:
        acc = jnp.square(jnp.maximum(acc, 0.0))
    o_ref[...] = acc.astype(o_ref.dtype)


def _norm_matmul(x, g, sc, sh, w, *, tm, tn, hi=False, relu2=False, out_dtype=F32, vmem_mb=48):
    m, d = x.shape
    n = w.shape[1]
    per_row = sc.shape[0] != 1
    mod_spec = pl.BlockSpec((tm, d), lambda i, j: (i, 0)) if per_row else pl.BlockSpec((1, d), lambda i, j: (0, 0))
    scratch = [pltpu.VMEM((tm, d), BF16)] + ([pltpu.VMEM((tm, d), BF16)] if hi else [])
    return pl.pallas_call(
        functools.partial(_norm_mm_kernel, hi=hi, relu2=relu2),
        grid=(m // tm, n // tn),
        in_specs=[pl.BlockSpec((tm, d), lambda i, j: (i, 0)),
                  pl.BlockSpec((1, d), lambda i, j: (0, 0)),
                  mod_spec, mod_spec,
                  pl.BlockSpec((d, tn), lambda i, j: (0, j))],
        out_specs=pl.BlockSpec((tm, tn), lambda i, j: (i, j)),
        out_shape=jax.ShapeDtypeStruct((m, n), out_dtype),
        scratch_shapes=scratch,
        compiler_params=_cp(("arbitrary", "arbitrary"), vmem_mb),
        name="norm_matmul",
    )(x, g, sc, sh, w)


def _mm_norm_res_kernel(a_ref, w_ref, ng_ref, gate_ref, res_ref, o_ref, *, hi):
    k = pl.program_id(1)
    if hi:
        part = _dot3(a_ref[...].astype(F32), w_ref[...])
    else:
        part = _dot1(a_ref[...], w_ref[...])

    @pl.when(k == 0)
    def _():
        o_ref[...] = part

    @pl.when(k > 0)
    def _():
        o_ref[...] += part

    @pl.when(k == pl.num_programs(1) - 1)
    def _():
        m = o_ref[...]
        y = m * lax.rsqrt(jnp.mean(m * m, axis=-1, keepdims=True) + EPS)
        o_ref[...] = res_ref[...] + gate_ref[...] * (y * ng_ref[...])


def _matmul_norm_res(a, w, ng, gate, res, *, tm, tk, hi=False, vmem_mb=48):
    m, kdim = a.shape
    n = w.shape[1]
    per_row = gate.shape[0] != 1
    gate_spec = pl.BlockSpec((tm, n), lambda i, k: (i, 0)) if per_row else pl.BlockSpec((1, n), lambda i, k: (0, 0))
    return pl.pallas_call(
        functools.partial(_mm_norm_res_kernel, hi=hi),
        grid=(m // tm, kdim // tk),
        in_specs=[pl.BlockSpec((tm, tk), lambda i, k: (i, k)),
                  pl.BlockSpec((tk, n), lambda i, k: (k, 0)),
                  pl.BlockSpec((1, n), lambda i, k: (0, 0)),
                  gate_spec,
                  pl.BlockSpec((tm, n), lambda i, k: (i, 0))],
        out_specs=pl.BlockSpec((tm, n), lambda i, k: (i, 0)),
        out_shape=jax.ShapeDtypeStruct((m, n), F32),
        compiler_params=_cp(("arbitrary", "arbitrary"), vmem_mb),
        name="matmul_norm_res",
    )(a, w, ng, gate, res)


def _rope_kernel(x_ref, tab_ref, o_ref, *, r1, r2):
    x = x_ref[...]
    out = x * tab_ref[:, 0:LANES] + pltpu.roll(x, r1, 1) * tab_ref[:, LANES:2 * LANES]
    if r2 is not None:
        out = out + pltpu.roll(x, r2, 1) * tab_ref[:, 2 * LANES:3 * LANES]
    o_ref[...] = out


def _rope(x, tab, *, col0, ngroups, n_kind0, r1, r2, tm):
    t = x.shape[0]
    return pl.pallas_call(
        functools.partial(_rope_kernel, r1=r1, r2=r2),
        grid=(t // tm, ngroups),
        in_specs=[pl.BlockSpec((tm, LANES), lambda i, j: (i, col0 + j)),
                  pl.BlockSpec((tm, 3 * LANES), lambda i, j: (i, jnp.where(j >= n_kind0, 1, 0)))],
        out_specs=pl.BlockSpec((tm, LANES), lambda i, j: (i, j)),
        out_shape=jax.ShapeDtypeStruct((t, ngroups * LANES), F32),
        compiler_params=_cp(("arbitrary", "arbitrary")),
        name="rope",
    )(x, tab)


def _rope_tables(pos):
    p = pos.astype(F32)[:, None]

    def cs(half):
        inv = ROPE_THETA ** (-jnp.arange(half, dtype=F32) / half)
        ang = p * inv[None, :]
        return jnp.cos(ang), jnp.sin(ang)

    c, s = cs(64)
    tab128 = jnp.concatenate([c, c, -s, s, jnp.zeros_like(c), jnp.zeros_like(c)], axis=1)
    c, s = cs(32)
    z = jnp.zeros_like(c)
    one64 = jnp.ones((pos.shape[0], 64), F32)
    z64 = jnp.zeros((pos.shape[0], 64), F32)
    kind0 = jnp.concatenate([c, c, c, c, z, s, z, s, -s, z, -s, z], axis=1)
    kind1 = jnp.concatenate([c, c, one64, z, s, z64, -s, z, z64], axis=1)
    return tab128, jnp.concatenate([kind0, kind1], axis=1)


def _gdn_kernel(qkv_ref, z_ref, gate_ref, buf_ref, cw_ref, alog_ref, dtb_ref, ng_ref, s0_ref,
                o_ref, sout_ref, xbuf_ref, s_ref, *, c, t_valid, gate_off):
    ci = pl.program_id(1)
    nh, dk = A_HEADS, A_DK
    hw = nh * dk

    @pl.when(ci == 0)
    def _():
        s_ref[...] = s0_ref[0]
        xbuf_ref[0:8, :] = jnp.zeros((8, 3 * hw), F32)
        xbuf_ref[8 - (A_CONV - 1):8, :] = buf_ref[0]

    xbuf_ref[8:8 + c, :] = qkv_ref[0]
    y = xbuf_ref[5:5 + c, :] * cw_ref[0:1, :]
    for j in range(1, A_CONV):
        y = y + xbuf_ref[5 + j:5 + j + c, :] * cw_ref[j:j + 1, :]
    tail = xbuf_ref[8 + c - 3:8 + c, :]
    xbuf_ref[5:8, :] = tail
    y = _silu(y)

    gt = gate_ref[0]
    ba = gt[:, gate_off:gate_off + nh]
    aa = gt[:, gate_off + nh:gate_off + 2 * nh]
    beta = _sigmoid(ba)
    xs = aa + dtb_ref[...]
    softplus = jnp.maximum(xs, 0.0) + jnp.log1p(jnp.exp(-jnp.abs(xs)))
    g = -jnp.exp(alog_ref[...]) * softplus
    row = lax.broadcasted_iota(I32, (c, c), 0)
    col = lax.broadcasted_iota(I32, (c, c), 1)
    if t_valid < c:
        valid = lax.broadcasted_iota(I32, (c, nh), 0) < t_valid
        beta = jnp.where(valid, beta, 0.0)
        g = jnp.where(valid, g, 0.0)
    incl = row >= col
    strict = row > col
    tri = jnp.where(incl, 1.0, 0.0).astype(BF16)
    gc = _dot_exact_left(tri, g)
    eye_h = jnp.where(lax.broadcasted_iota(I32, (nh, nh), 0) == lax.broadcasted_iota(I32, (nh, nh), 1),
                      1.0, 0.0).astype(BF16)
    gct = _dot_exact_left(eye_h, gc, NT)
    eye_c = jnp.where(row == col, 1.0, 0.0)
    n_dbl = int(round(math.log2(c))) - 1

    for h in range(nh):
        qh = y[:, h * dk:(h + 1) * dk]
        kh = y[:, hw + h * dk:hw + (h + 1) * dk]
        vh = y[:, 2 * hw + h * dk:2 * hw + (h + 1) * dk]
        qh = qh * lax.rsqrt(jnp.sum(qh * qh, axis=-1, keepdims=True) + 1e-6)
        kh = kh * lax.rsqrt(jnp.sum(kh * kh, axis=-1, keepdims=True) + 1e-6)
        bh = beta[:, h:h + 1]
        gch = gc[:, h:h + 1]
        decay = jnp.where(incl, jnp.exp(jnp.minimum(gch - gct[h:h + 1, :], 0.0)), 0.0)
        kb = kh * bh
        vb = vh * bh
        a_mat = jnp.where(strict, _dot3(kb, kh, NT) * decay, 0.0)
        pw = -a_mat
        t_inv = eye_c + pw
        for _ in range(n_dbl):
            pw = _dot3(pw, pw)
            t_inv = t_inv + _dot3(t_inv, pw)
        egc = jnp.exp(gch)
        u = _dot3(t_inv, vb)
        w = _dot3(t_inv, kb * egc)
        qs = qh * dk ** -0.5
        qk = _dot3(qs, kh, NT) * decay
        g_last = gch[c - 1:c, :]
        k_dec = kh * jnp.exp(jnp.minimum(g_last - gch, 0.0))
        s_old = s_ref[h]
        v_new = u - _dot3(w, s_old)
        o = _dot3(qs * egc, s_old) + _dot3(qk, v_new)
        s_ref[h] = s_old * jnp.exp(g_last) + _dot3(k_dec, v_new, TN)
        on = o * lax.rsqrt(jnp.mean(o * o, axis=-1, keepdims=True) + EPS) * ng_ref[...]
        o_ref[0, :, h * dk:(h + 1) * dk] = on * _silu(z_ref[0, :, h * dk:(h + 1) * dk])

    @pl.when(ci == pl.num_programs(1) - 1)
    def _():
        sout_ref[0] = s_ref[...]


def _gdn(proj, small, conv_buf, conv_w, a_log, dt_bias, norm_g, s0, *, c, t_valid, gate_group, gate_off):
    b, t = proj.shape[:2]
    nh, dk = A_HEADS, A_DK
    hw = nh * dk
    return pl.pallas_call(
        functools.partial(_gdn_kernel, c=c, t_valid=t_valid, gate_off=gate_off),
        grid=(b, t // c),
        in_specs=[pl.BlockSpec((1, c, 3 * hw), lambda bi, ci: (bi, ci, 0)),
                  pl.BlockSpec((1, c, hw), lambda bi, ci: (bi, ci, 3)),
                  pl.BlockSpec((1, c, LANES), lambda bi, ci: (bi, ci, gate_group)),
                  pl.BlockSpec((1, A_CONV - 1, 3 * hw), lambda bi, ci: (bi, 0, 0)),
                  pl.BlockSpec((A_CONV, 3 * hw), lambda bi, ci: (0, 0)),
                  pl.BlockSpec((1, nh), lambda bi, ci: (0, 0)),
                  pl.BlockSpec((1, nh), lambda bi, ci: (0, 0)),
                  pl.BlockSpec((1, dk), lambda bi, ci: (0, 0)),
                  pl.BlockSpec((1, nh, dk, dk), lambda bi, ci: (bi, 0, 0, 0))],
        out_specs=[pl.BlockSpec((1, c, hw), lambda bi, ci: (bi, ci, 0)),
                   pl.BlockSpec((1, nh, dk, dk), lambda bi, ci: (bi, 0, 0, 0))],
        out_shape=[jax.ShapeDtypeStruct((b, t, hw), F32), jax.ShapeDtypeStruct((b, nh, dk, dk), F32)],
        scratch_shapes=[pltpu.VMEM((8 + c, 3 * hw), F32), pltpu.VMEM((nh, dk, dk), F32)],
        compiler_params=_cp(("arbitrary", "arbitrary"), 40),
        name="gdn",
    )(proj, proj, small, conv_buf, conv_w, a_log.reshape(1, nh), dt_bias.reshape(1, nh), norm_g.reshape(1, dk), s0)


def _idx_prep_kernel(x_ref, q_ref, k_ref):
    tm = x_ref.shape[0]
    lo_half = lax.broadcasted_iota(I32, (tm, LANES), 1) < 64

    def split(xg):
        hi = xg.astype(BF16).astype(F32)
        return hi, xg - hi

    for gq in range(IDX_HEADS // 2):
        hi, lo = split(x_ref[:, gq * LANES:(gq + 1) * LANES])
        hi_r = pltpu.roll(hi, 64, 1)
        lo_r = pltpu.roll(lo, 64, 1)
        base = 2 * gq * 256
        q_ref[:, base:base + 128] = jnp.where(lo_half, hi, hi_r).astype(BF16)
        q_ref[:, base + 128:base + 256] = jnp.where(lo_half, lo, lo_r).astype(BF16)
        q_ref[:, base + 256:base + 384] = jnp.where(lo_half, hi_r, hi).astype(BF16)
        q_ref[:, base + 384:base + 512] = jnp.where(lo_half, lo_r, lo).astype(BF16)
    hi, lo = split(x_ref[:, 4 * LANES:5 * LANES])
    kk = jnp.where(lo_half, hi, pltpu.roll(lo, 64, 1)).astype(BF16)
    k_ref[:, 0:128] = kk
    k_ref[:, 128:256] = kk


def _idx_prep(small_r, tm=512):
    t = small_r.shape[0]
    return pl.pallas_call(
        _idx_prep_kernel,
        grid=(t // tm,),
        in_specs=[pl.BlockSpec((tm, 5 * LANES), lambda i: (i, 0))],
        out_specs=[pl.BlockSpec((tm, IDX_HEADS * 256), lambda i: (i, 0)),
                   pl.BlockSpec((tm, 256), lambda i: (i, 0))],
        out_shape=[jax.ShapeDtypeStruct((t, IDX_HEADS * 256), BF16), jax.ShapeDtypeStruct((t, 256), BF16)],
        compiler_params=_cp(("arbitrary",)),
        name="idx_prep",
    )(small_r)


IDX_SCALE = IDX_HEADS ** -0.5 * IDX_HD ** -0.5
W_OFF = 64


def _idx_scores_kernel(q_ref, k_ref, w_ref, o_ref, *, tq, ts):
    i = pl.program_id(0)
    j = pl.program_id(1)
    live = j * ts <= i * tq + tq - 1

    @pl.when(live)
    def _():
        k = k_ref[...]
        wg = w_ref[...] * IDX_SCALE
        acc = jnp.zeros((tq, ts), F32)
        for h in range(IDX_HEADS):
            s = _dot(q_ref[:, h * 256:(h + 1) * 256], k, NT)
            acc = acc + wg[:, W_OFF + h:W_OFF + h + 1] * jnp.maximum(s, 0.0)
        row = i * tq + lax.broadcasted_iota(I32, (tq, ts), 0)
        col = j * ts + lax.broadcasted_iota(I32, (tq, ts), 1)
        o_ref[...] = jnp.where(col <= row, acc, -jnp.inf)

    @pl.when(jnp.logical_not(live))
    def _():
        o_ref[...] = jnp.full((tq, ts), -jnp.inf, F32)


def _idx_scores(qcat, kcat, small_r, *, tq=256, ts=512):
    t = qcat.shape[0]
    return pl.pallas_call(
        functools.partial(_idx_scores_kernel, tq=tq, ts=ts),
        grid=(t // tq, t // ts),
        in_specs=[pl.BlockSpec((tq, IDX_HEADS * 256), lambda i, j: (i, 0)),
                  pl.BlockSpec((ts, 256), lambda i, j: (j, 0)),
                  pl.BlockSpec((tq, LANES), lambda i, j: (i, 4))],
        out_specs=pl.BlockSpec((tq, ts), lambda i, j: (i, j)),
        out_shape=jax.ShapeDtypeStruct((t, t), F32),
        compiler_params=_cp(("arbitrary", "arbitrary")),
        name="idx_scores",
    )(qcat, kcat, small_r)


def _thr_kernel(s_ref, thr_ref, need_ref, tie_ref, key_ref, *, k):
    x = s_ref[...] + 0.0
    bits = pltpu.bitcast(x, I32)
    key_ref[...] = jnp.where(bits < 0, bits ^ 0x7FFFFFFF, bits)
    kf = float(k)

    def count_ge(cand):
        return jnp.sum(jnp.where(key_ref[...] >= cand, 1.0, 0.0), axis=-1, keepdims=True)

    p0 = jnp.where(count_ge(0) >= kf, 0, INT_MIN).astype(I32)

    def body(b, p):
        cand = p | jnp.left_shift(jnp.int32(1), 30 - b)
        return jnp.where(count_ge(cand) >= kf, cand, p)

    p = lax.fori_loop(0, 31, body, p0)
    cnt_ge = count_ge(p)
    cnt_gt = jnp.sum(jnp.where(key_ref[...] > p, 1.0, 0.0), axis=-1, keepdims=True)
    thr_ref[...] = pltpu.bitcast(jnp.where(p < 0, p ^ 0x7FFFFFFF, p), F32)
    need_ref[...] = kf - cnt_gt
    tie_ref[...] = jnp.where(jnp.logical_and(cnt_ge > kf, p != KEY_NEG_INF), 1, 0).astype(I32)


def _topk_thr(scores, *, k, tr=64):
    r, s = scores.shape
    tr = min(tr, r)
    return pl.pallas_call(
        functools.partial(_thr_kernel, k=k),
        grid=(r // tr,),
        in_specs=[pl.BlockSpec((tr, s), lambda i: (i, 0))],
        out_specs=[pl.BlockSpec((tr, 1), lambda i: (i, 0))] * 3,
        out_shape=[jax.ShapeDtypeStruct((r, 1), F32), jax.ShapeDtypeStruct((r, 1), F32),
                   jax.ShapeDtypeStruct((r, 1), I32)],
        scratch_shapes=[pltpu.VMEM((tr, s), I32)],
        compiler_params=_cp(("arbitrary",), 32),
        name="topk_thr",
    )(scores)


def _select_mask(sc, thr, need, causal, eq_before, tie):
    if not tie:
        return jnp.logical_and(sc >= thr, causal), None
    eq = sc == thr
    n = sc.shape[1]
    upper = jnp.where(lax.broadcasted_iota(I32, (n, n), 0) < lax.broadcasted_iota(I32, (n, n), 1), 1.0, 0.0)
    eqf = jnp.where(eq, 1.0, 0.0)
    rank = eq_before + _dot(eqf.astype(BF16), upper.astype(BF16))
    sel = jnp.logical_or(sc > thr, jnp.logical_and(eq, rank < need))
    return jnp.logical_and(sel, causal), eq_before + jnp.sum(eqf, axis=-1, keepdims=True)


def _dsa_attn_kernel(tie_ref, q_ref, k_ref, v_ref, sc_ref, thr_ref, need_ref, o_ref,
                     kb_ref, vb_ref, m_ref, l_ref, acc_ref, eq_ref, *, tq, ts):
    i = pl.program_id(1)
    hg = B_HEADS // B_KV

    @pl.when(i == 0)
    def _():
        kb_ref[...] = k_ref[...].astype(BF16)
        vb_ref[...] = v_ref[...].astype(BF16)

    m_ref[...] = jnp.full(m_ref.shape, NEG_BIG, F32)
    l_ref[...] = jnp.zeros(l_ref.shape, F32)
    acc_ref[...] = jnp.zeros(acc_ref.shape, F32)
    eq_ref[...] = jnp.zeros(eq_ref.shape, F32)
    thr = thr_ref[...]
    need = need_ref[...]
    nblk = (i * tq + tq + ts - 1) // ts
    scale = B_HD ** -0.5

    def run(tie):
        def body(jb, carry):
            off = pl.multiple_of(jb * ts, ts)
            sc = sc_ref[:, pl.ds(off, ts)]
            row = i * tq + lax.broadcasted_iota(I32, (tq, ts), 0)
            col = off + lax.broadcasted_iota(I32, (tq, ts), 1)
            mask, eq_new = _select_mask(sc, thr, need, col <= row, eq_ref[...], tie)
            if tie:
                eq_ref[...] = eq_new
            kblk = kb_ref[pl.ds(off, ts), :]
            vblk = vb_ref[pl.ds(off, ts), :]
            for h in range(hg):
                qh = (q_ref[:, h * B_HD:(h + 1) * B_HD] * scale).astype(BF16)
                s = jnp.where(mask, _dot(qh, kblk, NT), NEG_BIG)
                m_old = m_ref[h]
                m_new = jnp.maximum(m_old, jnp.max(s, axis=-1, keepdims=True))
                p = jnp.where(mask, jnp.exp(s - m_new), 0.0)
                alpha = jnp.exp(m_old - m_new)
                l_ref[h] = alpha * l_ref[h] + jnp.sum(p, axis=-1, keepdims=True)
                acc_ref[h] = alpha * acc_ref[h] + _dot(p.astype(BF16), vblk)
                m_ref[h] = m_new
            return carry

        lax.fori_loop(0, nblk, body, 0)

    has_tie = tie_ref[i] > 0

    @pl.when(has_tie)
    def _():
        run(True)

    @pl.when(jnp.logical_not(has_tie))
    def _():
        run(False)

    for h in range(hg):
        o_ref[:, h * B_HD:(h + 1) * B_HD] = acc_ref[h] / l_ref[h]


def _dsa_attn(tie_blk, q_r, k_r, v_src, v_col0, scores, thr, need, *, tq=128, ts=512):
    t = q_r.shape[0]
    hg = B_HEADS // B_KV
    grid_spec = pltpu.PrefetchScalarGridSpec(
        num_scalar_prefetch=1,
        grid=(B_KV, t // tq),
        in_specs=[pl.BlockSpec((tq, hg * B_HD), lambda g, i, tie: (i, g)),
                  pl.BlockSpec((t, B_HD), lambda g, i, tie: (0, g)),
                  pl.BlockSpec((t, B_HD), lambda g, i, tie: (0, v_col0 + g)),
                  pl.BlockSpec((tq, t), lambda g, i, tie: (i, 0)),
                  pl.BlockSpec((tq, 1), lambda g, i, tie: (i, 0)),
                  pl.BlockSpec((tq, 1), lambda g, i, tie: (i, 0))],
        out_specs=pl.BlockSpec((tq, hg * B_HD), lambda g, i, tie: (i, g)),
        scratch_shapes=[pltpu.VMEM((t, B_HD), BF16), pltpu.VMEM((t, B_HD), BF16),
                        pltpu.VMEM((hg, tq, 1), F32), pltpu.VMEM((hg, tq, 1), F32),
                        pltpu.VMEM((hg, tq, B_HD), F32), pltpu.VMEM((tq, 1), F32)])
    return pl.pallas_call(
        functools.partial(_dsa_attn_kernel, tq=tq, ts=ts),
        grid_spec=grid_spec,
        out_shape=jax.ShapeDtypeStruct((t, B_HEADS * B_HD), F32),
        compiler_params=_cp(("arbitrary", "arbitrary"), 48),
        name="dsa_attn",
    )(tie_blk, q_r, k_r, v_src, scores, thr, need)


def _lambda_value(lam_ref, lam_init):
    lp = lam_ref[...]
    a = jnp.sum(lp[0:1, :] * lp[1:2, :], axis=-1, keepdims=True)
    b = jnp.sum(lp[2:3, :] * lp[3:4, :], axis=-1, keepdims=True)
    return jnp.exp(a) - jnp.exp(b) + lam_init


def _diff_attn_kernel(q_ref, k_ref, v_ref, lam_ref, ng_ref, o_ref, kb_ref, vb_ref, qs_ref, m_ref, l_ref, acc_ref,
                      *, tq, lam_init):
    i = pl.program_id(1)
    ts = tq

    @pl.when(i == 0)
    def _():
        kb_ref[...] = k_ref[...].astype(BF16)
        vb_ref[...] = v_ref[...].astype(BF16)

    q = q_ref[...] * C_HD ** -0.5
    first = lax.broadcasted_iota(I32, (tq, 2 * C_HD), 1) < C_HD
    qs_ref[0:tq, :] = jnp.where(first, q, 0.0).astype(BF16)
    qs_ref[tq:2 * tq, :] = jnp.where(first, 0.0, q).astype(BF16)
    m_ref[...] = jnp.full(m_ref.shape, NEG_BIG, F32)
    l_ref[...] = jnp.zeros(l_ref.shape, F32)
    acc_ref[...] = jnp.zeros(acc_ref.shape, F32)

    def step(jb, masked):
        off = pl.multiple_of(jb * ts, ts)
        s = _dot(qs_ref[...], kb_ref[pl.ds(off, ts), :], NT)
        if masked:
            r = lax.broadcasted_iota(I32, (2 * tq, ts), 0)
            r = jnp.where(r >= tq, r - tq, r)
            s = jnp.where(lax.broadcasted_iota(I32, (2 * tq, ts), 1) <= r, s, NEG_BIG)
        m_old = m_ref[...]
        m_new = jnp.maximum(m_old, jnp.max(s, axis=-1, keepdims=True))
        p = jnp.exp(s - m_new)
        alpha = jnp.exp(m_old - m_new)
        l_ref[...] = alpha * l_ref[...] + jnp.sum(p, axis=-1, keepdims=True)
        acc_ref[...] = alpha * acc_ref[...] + _dot(p.astype(BF16), vb_ref[pl.ds(off, ts), :])
        m_ref[...] = m_new

    def body(jb, carry):
        step(jb, False)
        return carry

    lax.fori_loop(0, i, body, 0)
    step(i, True)

    lam = _lambda_value(lam_ref, lam_init)
    o = acc_ref[0:tq, :] / l_ref[0:tq, :] - lam * (acc_ref[tq:2 * tq, :] / l_ref[tq:2 * tq, :])
    on = o * lax.rsqrt(jnp.mean(o * o, axis=-1, keepdims=True) + EPS) * ng_ref[...]
    o_ref[...] = on * (1.0 - lam_init)


def _diff_attn(qk_r, v_src, v_col0, c_lambda, c_norm_g, *, lam_init, tq=512):
    t = qk_r.shape[0]
    d2 = 2 * C_HD
    return pl.pallas_call(
        functools.partial(_diff_attn_kernel, tq=tq, lam_init=lam_init),
        grid=(C_HEADS, t // tq),
        in_specs=[pl.BlockSpec((tq, d2), lambda h, i: (i, h)),
                  pl.BlockSpec((t, d2), lambda h, i: (0, C_HEADS + h)),
                  pl.BlockSpec((t, d2), lambda h, i: (0, v_col0 + h)),
                  pl.BlockSpec((4, C_HD), lambda h, i: (0, 0)),
                  pl.BlockSpec((1, d2), lambda h, i: (0, 0))],
        out_specs=pl.BlockSpec((tq, d2), lambda h, i: (i, h)),
        out_shape=jax.ShapeDtypeStruct((t, C_HEADS * d2), F32),
        scratch_shapes=[pltpu.VMEM((t, d2), BF16), pltpu.VMEM((t, d2), BF16), pltpu.VMEM((2 * tq, d2), BF16),
                        pltpu.VMEM((2 * tq, 1), F32), pltpu.VMEM((2 * tq, 1), F32), pltpu.VMEM((2 * tq, d2), F32)],
        compiler_params=_cp(("arbitrary", "arbitrary"), 48),
        name="diff_attn",
    )(qk_r, qk_r, v_src, c_lambda, c_norm_g.reshape(1, d2))


TS_PAD = 8


def _idx_scores_s_kernel(pt_ref, q_ref, w_ref, pool_ref, new_ref, o_ref, *, n_pages, t_new):
    p = pl.program_id(1)
    is_new = p >= n_pages
    keys = jnp.where(is_new, new_ref[0], pool_ref[0, 0])
    s = _dot3(q_ref[0], keys, NT)
    w = w_ref[0] * IDX_SCALE
    acc = jnp.zeros((TS_PAD, PAGE), F32)
    for h in range(IDX_HEADS):
        acc = acc + w[h * TS_PAD:(h + 1) * TS_PAD, :] * jnp.maximum(s[h * TS_PAD:(h + 1) * TS_PAD, :], 0.0)
    tq = lax.broadcasted_iota(I32, (TS_PAD, PAGE), 0)
    tk = lax.broadcasted_iota(I32, (TS_PAD, PAGE), 1)
    ok = jnp.logical_or(jnp.logical_not(is_new), jnp.logical_and(tk <= tq, tk < t_new))
    o_ref[0] = jnp.where(ok, acc, -jnp.inf)


def _idx_scores_s(page_table, q_s, w_s, pool_ki, ki_new, *, layer, t_new):
    b, n_pages = page_table.shape
    grid_spec = pltpu.PrefetchScalarGridSpec(
        num_scalar_prefetch=1,
        grid=(b, n_pages + 1),
        in_specs=[pl.BlockSpec((1, IDX_HEADS * TS_PAD, IDX_HD), lambda bi, p, pt: (bi, 0, 0)),
                  pl.BlockSpec((1, IDX_HEADS * TS_PAD, 1), lambda bi, p, pt: (bi, 0, 0)),
                  pl.BlockSpec((1, 1, PAGE, IDX_HD),
                               lambda bi, p, pt: (layer, pt[bi, jnp.minimum(p, n_pages - 1)], 0, 0)),
                  pl.BlockSpec((1, PAGE, IDX_HD), lambda bi, p, pt: (bi, 0, 0))],
        out_specs=pl.BlockSpec((1, TS_PAD, PAGE), lambda bi, p, pt: (bi, 0, p)))
    return pl.pallas_call(
        functools.partial(_idx_scores_s_kernel, n_pages=n_pages, t_new=t_new),
        grid_spec=grid_spec,
        out_shape=jax.ShapeDtypeStruct((b, TS_PAD, (n_pages + 1) * PAGE), F32),
        compiler_params=_cp(("arbitrary", "arbitrary")),
        name="idx_scores_sample",
    )(page_table, q_s, w_s, pool_ki, ki_new)


def _dsa_attn_s_kernel(pt_ref, tie_ref, q_ref, pk_ref, pv_ref, nk_ref, nv_ref, sc_ref, thr_ref, need_ref, o_ref,
                       m_ref, l_ref, acc_ref, eq_ref, *, n_pages):
    bi = pl.program_id(0)
    p = pl.program_id(1)
    hg = B_HEADS // B_KV
    rows = hg * TS_PAD

    @pl.when(p == 0)
    def _():
        m_ref[...] = jnp.full(m_ref.shape, NEG_BIG, F32)
        l_ref[...] = jnp.zeros(l_ref.shape, F32)
        acc_ref[...] = jnp.zeros(acc_ref.shape, F32)
        eq_ref[...] = jnp.zeros(eq_ref.shape, F32)

    is_new = p >= n_pages
    keys = jnp.where(is_new, nk_ref[0], pk_ref[0, 0])
    vals = jnp.where(is_new, nv_ref[0], pv_ref[0, 0])
    sc = sc_ref[0]
    thr = thr_ref[0]
    need = need_ref[0]
    valid = sc > -jnp.inf
    scale = B_HD ** -0.5

    def run(tie):
        mask8, eq_new = _select_mask(sc, thr, need, valid, eq_ref[...], tie)
        if tie:
            eq_ref[...] = eq_new
        mask = jnp.concatenate([mask8] * hg, axis=0)
        for g in range(B_KV):
            s = _dot3(q_ref[0, g] * scale, keys[:, g * B_HD:(g + 1) * B_HD], NT)
            s = jnp.where(mask, s, NEG_BIG)
            m_old = m_ref[g]
            m_new = jnp.maximum(m_old, jnp.max(s, axis=-1, keepdims=True))
            pr = jnp.where(mask, jnp.exp(s - m_new), 0.0)
            alpha = jnp.exp(m_old - m_new)
            l_ref[g] = alpha * l_ref[g] + jnp.sum(pr, axis=-1, keepdims=True)
            acc_ref[g] = alpha * acc_ref[g] + _dot3(pr, vals[:, g * B_HD:(g + 1) * B_HD])
            m_ref[g] = m_new

    has_tie = tie_ref[bi] > 0

    @pl.when(has_tie)
    def _():
        run(True)

    @pl.when(jnp.logical_not(has_tie))
    def _():
        run(False)

    @pl.when(p == pl.num_programs(1) - 1)
    def _():
        for g in range(B_KV):
            o_ref[0, g] = acc_ref[g] / l_ref[g]


def _dsa_attn_s(page_table, tie_b, q_s, pool_k, pool_v, k_new, v_new, scores, thr, need, *, layer):
    b, n_pages = page_table.shape
    hg = B_HEADS // B_KV
    rows = hg * TS_PAD
    kvw = B_KV * B_HD

    def page_map(bi, p, pt, tie):
        return (layer, pt[bi, jnp.minimum(p, n_pages - 1)], 0, 0)

    grid_spec = pltpu.PrefetchScalarGridSpec(
        num_scalar_prefetch=2,
        grid=(b, n_pages + 1),
        in_specs=[pl.BlockSpec((1, B_KV, rows, B_HD), lambda bi, p, pt, tie: (bi, 0, 0, 0)),
                  pl.BlockSpec((1, 1, PAGE, kvw), page_map),
                  pl.BlockSpec((1, 1, PAGE, kvw), page_map),
                  pl.BlockSpec((1, PAGE, kvw), lambda bi, p, pt, tie: (bi, 0, 0)),
                  pl.BlockSpec((1, PAGE, kvw), lambda bi, p, pt, tie: (bi, 0, 0)),
                  pl.BlockSpec((1, TS_PAD, PAGE), lambda bi, p, pt, tie: (bi, 0, p)),
                  pl.BlockSpec((1, TS_PAD, 1), lambda bi, p, pt, tie: (bi, 0, 0)),
                  pl.BlockSpec((1, TS_PAD, 1), lambda bi, p, pt, tie: (bi, 0, 0))],
        out_specs=pl.BlockSpec((1, B_KV, rows, B_HD), lambda bi, p, pt, tie: (bi, 0, 0, 0)),
        scratch_shapes=[pltpu.VMEM((B_KV, rows, 1), F32), pltpu.VMEM((B_KV, rows, 1), F32),
                        pltpu.VMEM((B_KV, rows, B_HD), F32), pltpu.VMEM((TS_PAD, 1), F32)])
    return pl.pallas_call(
        functools.partial(_dsa_attn_s_kernel, n_pages=n_pages),
        grid_spec=grid_spec,
        out_shape=jax.ShapeDtypeStruct((b, B_KV, rows, B_HD), F32),
        compiler_params=_cp(("arbitrary", "arbitrary")),
        name="dsa_attn_sample",
    )(page_table, tie_b, q_s, pool_k, pool_v, k_new, v_new, scores, thr, need)


def _diff_attn_s_kernel(pt_ref, wt_ref, pk_ref, pv_ref, nk_ref, nv_ref, lam_ref, ng_ref, o_ref,
                        m_ref, l_ref, acc_ref, *, t_new, lam_init):
    p = pl.program_id(1)
    ncol = wt_ref.shape[1]

    @pl.when(p == 0)
    def _():
        m_ref[...] = jnp.full(m_ref.shape, NEG_BIG, F32)
        l_ref[...] = jnp.zeros(l_ref.shape, F32)
        acc_ref[...] = jnp.zeros(acc_ref.shape, F32)

    wt = wt_ref[0]

    def attend(keys, vals, mask):
        s = _dot(wt, keys.astype(BF16), NT)
        if mask is not None:
            s = jnp.where(mask, s, NEG_BIG)
        m_old = m_ref[...]
        m_new = jnp.maximum(m_old, jnp.max(s, axis=-1, keepdims=True))
        pr = jnp.exp(s - m_new)
        alpha = jnp.exp(m_old - m_new)
        l_ref[...] = alpha * l_ref[...] + jnp.sum(pr, axis=-1, keepdims=True)
        acc_ref[...] = alpha * acc_ref[...] + _dot(pr.astype(BF16), vals.astype(BF16))
        m_ref[...] = m_new

    attend(pk_ref[0, 0], pv_ref[0, 0], None)

    @pl.when(p == pl.num_programs(1) - 1)
    def _():
        tq = lax.broadcasted_iota(I32, (ncol, TS_PAD), 0) & (t_new - 1)
        tk = lax.broadcasted_iota(I32, (ncol, TS_PAD), 1)
        attend(nk_ref[0], nv_ref[0], tk <= tq)
        lam = _lambda_value(lam_ref, lam_init)
        half = ncol // 2
        d = acc_ref[0:half, :] / l_ref[0:half, :] - lam * (acc_ref[half:ncol, :] / l_ref[half:ncol, :])
        dv = 2 * C_HD
        for h in range(C_HEADS):
            o = d[h * t_new:(h + 1) * t_new, h * dv:(h + 1) * dv]
            on = o * lax.rsqrt(jnp.mean(o * o, axis=-1, keepdims=True) + EPS) * ng_ref[...]
            o_ref[0, :, h * dv:(h + 1) * dv] = on * (1.0 - lam_init)


def _diff_attn_s(page_table, wt, pool_k, pool_v, k_new, v_new, c_lambda, c_norm_g, *, layer, t_new, lam_init):
    b, n_pages = page_table.shape
    ncol, width = wt.shape[1:]

    def page_map(bi, p, pt):
        return (layer, pt[bi, p], 0, 0)

    grid_spec = pltpu.PrefetchScalarGridSpec(
        num_scalar_prefetch=1,
        grid=(b, n_pages),
        in_specs=[pl.BlockSpec((1, ncol, width), lambda bi, p, pt: (bi, 0, 0)),
                  pl.BlockSpec((1, 1, PAGE, width), page_map),
                  pl.BlockSpec((1, 1, PAGE, width), page_map),
                  pl.BlockSpec((1, TS_PAD, width), lambda bi, p, pt: (bi, 0, 0)),
                  pl.BlockSpec((1, TS_PAD, width), lambda bi, p, pt: (bi, 0, 0)),
                  pl.BlockSpec((4, C_HD), lambda bi, p, pt: (0, 0)),
                  pl.BlockSpec((1, 2 * C_HD), lambda bi, p, pt: (0, 0))],
        out_specs=pl.BlockSpec((1, t_new, width), lambda bi, p, pt: (bi, 0, 0)),
        scratch_shapes=[pltpu.VMEM((ncol, 1), F32), pltpu.VMEM((ncol, 1), F32), pltpu.VMEM((ncol, width), F32)])
    return pl.pallas_call(
        functools.partial(_diff_attn_s_kernel, t_new=t_new, lam_init=lam_init),
        grid_spec=grid_spec,
        out_shape=jax.ShapeDtypeStruct((b, t_new, width), F32),
        compiler_params=_cp(("arbitrary", "arbitrary"), 32),
        name="diff_attn_sample",
    )(page_table, wt, pool_k, pool_v, k_new, v_new, c_lambda, c_norm_g.reshape(1, 2 * C_HD))


def _split_ab_weights(w_in):
    n_a = 4 * A_HEADS * A_DK
    n_gate = 2 * A_HEADS
    n_b = (B_HEADS + 2 * B_KV) * B_HD
    n_idx = IDX_HEADS * IDX_HD + IDX_HD + IDX_HEADS
    big = jnp.concatenate([w_in[:, :n_a], w_in[:, n_a + n_gate:n_a + n_gate + n_b]], axis=1)
    pad = 5 * LANES - n_idx - n_gate
    small = jnp.concatenate([w_in[:, n_a + n_gate + n_b:], w_in[:, n_a:n_a + n_gate],
                             jnp.zeros((w_in.shape[0], pad), w_in.dtype)], axis=1)
    return big, small


GATE_OFF = IDX_HD + IDX_HEADS
QB_COL0 = 4 * A_HEADS * A_DK // LANES
KB_COL0 = QB_COL0 + B_HEADS
VB_COL0 = KB_COL0 + B_KV


def _mlp(x, mods, norm_g, w1, w2, *, tm, hi):
    sh2, sc2, g2 = mods
    hid = _norm_matmul(x, norm_g[2:3], sc2, sh2, w1, tm=tm, tn=512, hi=hi, relu2=True,
                       out_dtype=F32 if hi else BF16)
    return _matmul_norm_res(hid, w2, norm_g[3:4], g2, x, tm=tm, tk=512, hi=hi)


def _prompt_trunk(x, mods, P):
    t = x.shape[0]
    pos = jnp.arange(t, dtype=I32)
    tab128, tab64 = _rope_tables(pos)
    sh1, sc1, g1, sh2, sc2, g2 = mods[0]
    ng = P["norm_g"][0]
    big = _norm_matmul(x, ng[0:1], sc1, sh1, P["w_big"], tm=512, tn=512)
    small = _norm_matmul(x, ng[0:1], sc1, sh1, P["w_small"], tm=512, tn=5 * LANES, hi=True)
    small_r = _rope(small, tab64, col0=0, ngroups=5, n_kind0=4, r1=32, r2=96, tm=1024)
    qb_r = _rope(big, tab128, col0=QB_COL0, ngroups=B_HEADS, n_kind0=B_HEADS, r1=64, r2=None, tm=1024)
    kb_r = _rope(big, tab128, col0=KB_COL0, ngroups=B_KV, n_kind0=B_KV, r1=64, r2=None, tm=1024)
    qcat, kcat = _idx_prep(small_r)
    scores = _idx_scores(qcat, kcat, small_r)
    thr, need, tie = _topk_thr(scores, k=min(TOPK, t // 4))
    tq = 128
    tie_blk = jnp.max(tie.reshape(t // tq, tq), axis=1)
    ob = _dsa_attn(tie_blk, qb_r, kb_r, big, VB_COL0, scores, thr, need, tq=tq)
    oa, s_new = _gdn(big[None], small_r[None], jnp.zeros((1, A_CONV - 1, 3 * A_HEADS * A_DK), F32), P["ab_conv_w"],
                     P["ab_A_log"], P["ab_dt_bias"], P["ab_norm_g"], jnp.zeros((1, A_HEADS, A_DK, A_DK), F32),
                     c=64, t_valid=64, gate_group=4, gate_off=GATE_OFF)
    mix = jnp.concatenate([oa[0], ob], axis=1)
    x = _matmul_norm_res(mix, P["ab_w_out"], ng[1:2], g1, x, tm=512, tk=512)
    x = _mlp(x, (sh2, sc2, g2), ng, P["mlp_w1"][0], P["mlp_w2"][0], tm=512, hi=False)
    outs0 = (kb_r, big[:, VB_COL0 * LANES:], small_r[:, 4 * LANES:4 * LANES + IDX_HD], s_new,
             big[t - (A_CONV - 1):, :3 * A_HEADS * A_DK])
    sh1, sc1, g1, sh2, sc2, g2 = mods[1]
    ng = P["norm_g"][1]
    lam_init = 0.8 - 0.6 * math.exp(-0.3 * 1)
    proj = _norm_matmul(x, ng[0:1], sc1, sh1, P["c_w_in"], tm=512, tn=512)
    qk_r = _rope(proj, tab64, col0=0, ngroups=2 * C_HEADS, n_kind0=2 * C_HEADS, r1=32, r2=96, tm=1024)
    o = _diff_attn(qk_r, proj, 2 * C_HEADS, P["c_lambda"], P["c_norm_g"], lam_init=lam_init)
    x = _matmul_norm_res(o, P["c_w_out"], ng[1:2], g1, x, tm=512, tk=512)
    x = _mlp(x, (sh2, sc2, g2), ng, P["mlp_w1"][1], P["mlp_w2"][1], tm=512, hi=False)
    outs1 = (qk_r[:, 2 * C_HEADS * C_HD:], proj[:, 4 * C_HEADS * C_HD:])
    return x, outs0, outs1


def _sample_trunk(x, mods, P, cache, past_len):
    b, ts_, d = x.shape
    m = b * ts_
    x = x.reshape(m, d)
    pos = past_len + jnp.arange(ts_, dtype=I32)
    tab128, tab64 = _rope_tables(pos)
    tab128 = jnp.tile(tab128, (b, 1))
    tab64 = jnp.tile(tab64, (b, 1))
    page_table = cache["page_table"]
    n_pages = page_table.shape[1]
    hw = A_HEADS * A_DK

    def rows(a):
        return jnp.repeat(a, ts_, axis=0)

    def pad_t(a, n):
        return jnp.pad(a, ((0, 0), (0, n - ts_), (0, 0)))

    sh1, sc1, g1, sh2, sc2, g2 = (rows(a) for a in mods[0])
    ng = P["norm_g"][0]
    big = _norm_matmul(x, ng[0:1], sc1, sh1, P["w_big"], tm=m, tn=512, hi=True)
    small = _norm_matmul(x, ng[0:1], sc1, sh1, P["w_small"], tm=m, tn=5 * LANES, hi=True)
    small_r = _rope(small, tab64, col0=0, ngroups=5, n_kind0=4, r1=32, r2=96, tm=m)
    qb_r = _rope(big, tab128, col0=QB_COL0, ngroups=B_HEADS, n_kind0=B_HEADS, r1=64, r2=None, tm=m)
    kb_r = _rope(big, tab128, col0=KB_COL0, ngroups=B_KV, n_kind0=B_KV, r1=64, r2=None, tm=m)
    vb = big[:, VB_COL0 * LANES:]
    sm3 = small_r.reshape(b, ts_, 5 * LANES)
    qi = sm3[:, :, :IDX_HEADS * IDX_HD].reshape(b, ts_, IDX_HEADS, IDX_HD)
    qi = jnp.pad(jnp.swapaxes(qi, 1, 2), ((0, 0), (0, 0), (0, TS_PAD - ts_), (0, 0)))
    qi = qi.reshape(b, IDX_HEADS * TS_PAD, IDX_HD)
    wi = sm3[:, :, 4 * LANES + W_OFF:4 * LANES + W_OFF + IDX_HEADS]
    wi = jnp.pad(jnp.swapaxes(wi, 1, 2), ((0, 0), (0, 0), (0, TS_PAD - ts_))).reshape(b, IDX_HEADS * TS_PAD, 1)
    ki_r = sm3[:, :, 4 * LANES:4 * LANES + IDX_HD]
    scores = _idx_scores_s(page_table, qi, wi, cache["ab_kidx"], pad_t(ki_r, PAGE), layer=0, t_new=ts_)
    ncols = scores.shape[2]
    thr, need, tie = _topk_thr(scores.reshape(b * TS_PAD, ncols), k=min(TOPK, (past_len + ts_) // 4))
    tie_b = jnp.max(tie.reshape(b, TS_PAD)[:, :ts_], axis=1)
    hg = B_HEADS // B_KV
    q4 = qb_r.reshape(b, ts_, B_KV, hg, B_HD)
    q4 = jnp.pad(jnp.transpose(q4, (0, 2, 3, 1, 4)), ((0, 0), (0, 0), (0, 0), (0, TS_PAD - ts_), (0, 0)))
    q4 = q4.reshape(b, B_KV, hg * TS_PAD, B_HD)
    kvw = B_KV * B_HD
    pool_k = cache["ab_k"].reshape(cache["ab_k"].shape[:3] + (kvw,))
    pool_v = cache["ab_v"].reshape(cache["ab_v"].shape[:3] + (kvw,))
    ob = _dsa_attn_s(page_table, tie_b, q4, pool_k, pool_v, pad_t(kb_r.reshape(b, ts_, kvw), PAGE),
                     pad_t(vb.reshape(b, ts_, kvw), PAGE), scores, thr.reshape(b, TS_PAD, 1),
                     need.reshape(b, TS_PAD, 1), layer=0)
    ob = ob.reshape(b, B_KV, hg, TS_PAD, B_HD)[:, :, :, :ts_]
    ob = jnp.transpose(ob, (0, 3, 1, 2, 4)).reshape(m, B_HEADS * B_HD)
    oa, s_new = _gdn(pad_t(big.reshape(b, ts_, -1), TS_PAD), pad_t(sm3, TS_PAD), cache["ab_conv"][0], P["ab_conv_w"],
                     P["ab_A_log"], P["ab_dt_bias"], P["ab_norm_g"], cache["ab_delta"][0],
                     c=TS_PAD, t_valid=ts_, gate_group=4, gate_off=GATE_OFF)
    mix = jnp.concatenate([oa[:, :ts_].reshape(m, hw), ob], axis=1)
    x = _matmul_norm_res(mix, P["ab_w_out"], ng[1:2], g1, x, tm=m, tk=512, hi=True)
    x = _mlp(x, (sh2, sc2, g2), ng, P["mlp_w1"][0], P["mlp_w2"][0], tm=m, hi=True)
    conv_in = jnp.concatenate([cache["ab_conv"][0], big.reshape(b, ts_, -1)[:, :, :3 * hw]], axis=1)
    outs0 = (kb_r, vb, ki_r, s_new, conv_in[:, ts_:])
    sh1, sc1, g1, sh2, sc2, g2 = (rows(a) for a in mods[1])
    ng = P["norm_g"][1]
    lam_init = 0.8 - 0.6 * math.exp(-0.3 * 1)
    proj = _norm_matmul(x, ng[0:1], sc1, sh1, P["c_w_in"], tm=m, tn=512, hi=True)
    qk_r = _rope(proj, tab64, col0=0, ngroups=2 * C_HEADS, n_kind0=2 * C_HEADS, r1=32, r2=96, tm=m)
    cw = 2 * C_HEADS * C_HD
    q3 = qk_r[:, :cw].reshape(b, ts_, cw) * C_HD ** -0.5
    k3 = qk_r[:, cw:].reshape(b, ts_, cw)
    v3 = proj[:, 2 * cw:].reshape(b, ts_, cw)
    lane_hp = jnp.arange(cw, dtype=I32) // C_HD
    want = (2 * jnp.arange(C_HEADS, dtype=I32)[None, :] + jnp.arange(2, dtype=I32)[:, None])
    sel = (lane_hp[None, None, :] == want[:, :, None]).astype(F32)
    wt = (q3[:, None, None, :, :] * sel[None, :, :, None, :]).reshape(b, 2 * C_HEADS * ts_, cw).astype(BF16)
    pool_ck = cache["c_k"].reshape(cache["c_k"].shape[:3] + (cw,))
    pool_cv = cache["c_v"].reshape(cache["c_v"].shape[:3] + (cw,))
    o = _diff_attn_s(page_table, wt, pool_ck, pool_cv, pad_t(k3, TS_PAD), pad_t(v3, TS_PAD), P["c_lambda"],
                     P["c_norm_g"], layer=0, t_new=ts_, lam_init=lam_init)
    x = _matmul_norm_res(o.reshape(m, cw), P["c_w_out"], ng[1:2], g1, x, tm=m, tk=512, hi=True)
    x = _mlp(x, (sh2, sc2, g2), ng, P["mlp_w1"][1], P["mlp_w2"][1], tm=m, hi=True)
    outs1 = (qk_r[:, cw:], proj[:, 2 * cw:])
    return x.reshape(b, ts_, d), outs0, outs1


def kernel(x_prompt, x_sample, cache_ab_k, cache_ab_v, cache_ab_kidx, state_ab_delta, state_ab_conv, cache_c_k,
           cache_c_v, page_table, c_prompt, c_sample, ada_w, ada_b, norm_g, mlp_w1, mlp_w2, ab_w_in, ab_w_out,
           ab_conv_w, ab_A_log, ab_dt_bias, ab_norm_g, c_w_in, c_w_out, c_lambda, c_norm_g):
    bp, t, d = x_prompt.shape
    bs, ts_, _ = x_sample.shape
    assert bp == 1
    past_len = page_table.shape[1] * PAGE
    w_big, w_small = _split_ab_weights(ab_w_in[0])
    P = {"norm_g": norm_g, "mlp_w1": mlp_w1, "mlp_w2": mlp_w2, "w_big": w_big, "w_small": w_small,
         "ab_w_out": ab_w_out[0], "ab_conv_w": ab_conv_w[0], "ab_A_log": ab_A_log[0], "ab_dt_bias": ab_dt_bias[0],
         "ab_norm_g": ab_norm_g[0], "c_w_in": c_w_in[0], "c_w_out": c_w_out[0], "c_lambda": c_lambda[0],
         "c_norm_g": c_norm_g[0]}
    n_seq = bp + bs
    mc = -(-n_seq // 8) * 8
    c_all = jnp.pad(jnp.concatenate([c_prompt, c_sample], axis=0), ((0, mc - n_seq), (0, 0)))
    mod = _ada(c_all, ada_w, ada_b)
    mods_p = [tuple(mod[i, 0:bp, n * d:(n + 1) * d] for n in range(6)) for i in range(2)]
    mods_s = [tuple(mod[i, bp:n_seq, n * d:(n + 1) * d] for n in range(6)) for i in range(2)]

    y_p, p0, p1 = _prompt_trunk(x_prompt[0], mods_p, P)
    cache = {"ab_k": cache_ab_k, "ab_v": cache_ab_v, "ab_kidx": cache_ab_kidx, "ab_delta": state_ab_delta,
             "ab_conv": state_ab_conv, "c_k": cache_c_k, "c_v": cache_c_v, "page_table": page_table}
    y_s, s0, s1 = _sample_trunk(x_sample, mods_s, P, cache, past_len)

    return (y_p[None], y_s,
            p0[0].reshape(1, 1, t, B_KV, B_HD), p0[1].reshape(1, 1, t, B_KV, B_HD), p0[2].reshape(1, 1, t, IDX_HD),
            p0[3][None], p0[4].reshape(1, 1, A_CONV - 1, -1),
            p1[0].reshape(1, 1, t, C_HEADS, 2, C_HD), p1[1].reshape(1, 1, t, C_HEADS, 2 * C_HD),
            s0[0].reshape(1, bs, ts_, B_KV, B_HD), s0[1].reshape(1, bs, ts_, B_KV, B_HD),
            s0[2].reshape(1, bs, ts_, IDX_HD), s0[3][None], s0[4][None],
            s1[0].reshape(1, bs, ts_, C_HEADS, 2, C_HD), s1[1].reshape(1, bs, ts_, C_HEADS, 2 * C_HD))
```

```python
import functools
import math

import jax
import jax.numpy as jnp
from jax import lax
from jax.experimental import pallas as pl
from jax.experimental.pallas import tpu as pltpu

F32 = jnp.float32
BF16 = jnp.bfloat16
I32 = jnp.int32

EPS = 1e-6
NEG_BIG = -1e30
ROPE_THETA = 10000.0
PAGE = 128
A_HEADS = 8
A_DK = 128
A_CONV = 4
B_HEADS = 8
B_KV = 2
B_HD = 128
IDX_HEADS = 8
IDX_HD = 64
TOPK = 256
C_HEADS = 16
C_HD = 64
LANES = 128

NN = (((1,), (0,)), ((), ()))
NT = (((1,), (1,)), ((), ()))
TN = (((0,), (0,)), ((), ()))
INT_MIN = -2147483648
KEY_NEG_INF = -2139095041


def _cp(dims, vmem_mb=None):
    kw = dict(dimension_semantics=dims)
    if vmem_mb is not None:
        kw["vmem_limit_bytes"] = vmem_mb << 20
    return pltpu.CompilerParams(**kw)


def _dot(a, b, dims=NN):
    return lax.dot_general(a, b, dims, preferred_element_type=F32)


def _split2(a):
    hi = a.astype(BF16)
    return hi, (a - hi.astype(F32)).astype(BF16)


def _split3(a):
    hi = a.astype(BF16)
    r = a - hi.astype(F32)
    mid = r.astype(BF16)
    return hi, mid, (r - mid.astype(F32)).astype(BF16)


def _dot1(a, b, dims=NN):
    return _dot(a.astype(BF16), b.astype(BF16), dims)


def _dot3(a, b, dims=NN):
    ah, al = _split2(a)
    bh, bl = _split2(b)
    return _dot(ah, bh, dims) + (_dot(ah, bl, dims) + _dot(al, bh, dims))


def _dot_exact_left(ones_bf16, b, dims=NN):
    b0, b1, b2 = _split3(b)
    return _dot(ones_bf16, b0, dims) + (_dot(ones_bf16, b1, dims) + _dot(ones_bf16, b2, dims))


def _sigmoid(x):
    return 1.0 / (1.0 + jnp.exp(-x))


def _silu(x):
    return x * _sigmoid(x)


def _ada_kernel(c_ref, w_ref, b_ref, o_ref):
    o_ref[0] = _dot3(_silu(c_ref[...]), w_ref[0]) + b_ref[0]


def _ada(c_all, ada_w, ada_b, tn=512):
    nl, d, n = ada_w.shape
    mc = c_all.shape[0]
    return pl.pallas_call(
        _ada_kernel,
        grid=(nl, n // tn),
        in_specs=[pl.BlockSpec((mc, d), lambda l, j: (0, 0)),
                  pl.BlockSpec((1, d, tn), lambda l, j: (l, 0, j)),
                  pl.BlockSpec((1, 1, tn), lambda l, j: (l, 0, j))],
        out_specs=pl.BlockSpec((1, mc, tn), lambda l, j: (l, 0, j)),
        out_shape=jax.ShapeDtypeStruct((nl, mc, n), F32),
        compiler_params=_cp(("arbitrary", "arbitrary"), 40),
        name="ada_mod",
    )(c_all, ada_w, ada_b.reshape(nl, 1, n))


def _norm_mm_kernel(x_ref, g_ref, sc_ref, sh_ref, w_ref, o_ref, *scratch, hi, relu2):
    hh_ref = scratch[0]

    @pl.when(pl.program_id(1) == 0)
    def _():
        x = x_ref[...]
        y = x * lax.rsqrt(jnp.mean(x * x, axis=-1, keepdims=True) + EPS)
        h = (y * g_ref[...]) * (1.0 + sc_ref[...]) + sh_ref[...]
        hh = h.astype(BF16)
        hh_ref[...] = hh
        if hi:
            scratch[1][...] = (h - hh.astype(F32)).astype(BF16)

    w = w_ref[...]
    wh = w.astype(BF16)
    acc = _dot(hh_ref[...], wh)
    if hi:
        wl = (w - wh.astype(F32)).astype(BF16)
        acc = acc + (_dot(hh_ref[...], wl) + _dot(scratch[1][...], wh))
    if relu2:
        acc = jnp.square(jnp.maximum(acc, 0.0))
    o_ref[...] = acc.astype(o_ref.dtype)


def _norm_matmul(x, g, sc, sh, w, *, tm, tn, hi=False, relu2=False, out_dtype=F32, vmem_mb=48):
    m, d = x.shape
    n = w.shape[1]
    per_row = sc.shape[0] != 1
    mod_spec = pl.BlockSpec((tm, d), lambda i, j: (i, 0)) if per_row else pl.BlockSpec((1, d), lambda i, j: (0, 0))
    scratch = [pltpu.VMEM((tm, d), BF16)] + ([pltpu.VMEM((tm, d), BF16)] if hi else [])
    return pl.pallas_call(
        functools.partial(_norm_mm_kernel, hi=hi, relu2=relu2),
        grid=(m // tm, n // tn),
        in_specs=[pl.BlockSpec((tm, d), lambda i, j: (i, 0), pipeline_mode=pl.Buffered(1)),
                  pl.BlockSpec((1, d), lambda i, j: (0, 0)),
                  mod_spec, mod_spec,
                  pl.BlockSpec((d, tn), lambda i, j: (0, j))],
        out_specs=pl.BlockSpec((tm, tn), lambda i, j: (i, j)),
        out_shape=jax.ShapeDtypeStruct((m, n), out_dtype),
        scratch_shapes=scratch,
        compiler_params=_cp(("arbitrary", "arbitrary"), vmem_mb),
        name="norm_matmul",
    )(x, g, sc, sh, w)


def _mm_norm_res_kernel(a_ref, w_ref, ng_ref, gate_ref, res_ref, o_ref, *, hi):
    k = pl.program_id(1)

    @pl.when(k == 0)
    def _():
        o_ref[...] = jnp.zeros(o_ref.shape, F32)

    if hi:
        o_ref[...] += _dot3(a_ref[...].astype(F32), w_ref[...])
    else:
        o_ref[...] += _dot1(a_ref[...], w_ref[...])

    @pl.when(k == pl.num_programs(1) - 1)
    def _():
        m = o_ref[...]
        y = m * lax.rsqrt(jnp.mean(m * m, axis=-1, keepdims=True) + EPS)
        o_ref[...] = res_ref[...] + gate_ref[...] * (y * ng_ref[...])


def _matmul_norm_res(a, w, ng, gate, res, *, tm, tk, hi=False, vmem_mb=48):
    m, kdim = a.shape
    n = w.shape[1]
    per_row = gate.shape[0] != 1
    gate_spec = pl.BlockSpec((tm, n), lambda i, k: (i, 0)) if per_row else pl.BlockSpec((1, n), lambda i, k: (0, 0))
    return pl.pallas_call(
        functools.partial(_mm_norm_res_kernel, hi=hi),
        grid=(m // tm, kdim // tk),
        in_specs=[pl.BlockSpec((tm, tk), lambda i, k: (i, k)),
                  pl.BlockSpec((tk, n), lambda i, k: (k, 0)),
                  pl.BlockSpec((1, n), lambda i, k: (0, 0)),
                  gate_spec,
                  pl.BlockSpec((tm, n), lambda i, k: (i, 0), pipeline_mode=pl.Buffered(1))],
        out_specs=pl.BlockSpec((tm, n), lambda i, k: (i, 0)),
        out_shape=jax.ShapeDtypeStruct((m, n), F32),
        compiler_params=_cp(("arbitrary", "arbitrary"), vmem_mb),
        name="matmul_norm_res",
    )(a, w, ng, gate, res)


def _rope_kernel(x_ref, tab_ref, o_ref, *, r1, r2):
    x = x_ref[...]
    out = x * tab_ref[:, 0:LANES] + pltpu.roll(x, r1, 1) * tab_ref[:, LANES:2 * LANES]
    if r2 is not None:
        out = out + pltpu.roll(x, r2, 1) * tab_ref[:, 2 * LANES:3 * LANES]
    o_ref[...] = out


def _rope(x, tab, *, col0, ngroups, n_kind0, r1, r2, tm):
    t = x.shape[0]
    return pl.pallas_call(
        functools.partial(_rope_kernel, r1=r1, r2=r2),
        grid=(t // tm, ngroups),
        in_specs=[pl.BlockSpec((tm, LANES), lambda i, j: (i, col0 + j)),
                  pl.BlockSpec((tm, 3 * LANES), lambda i, j: (i, jnp.where(j >= n_kind0, 1, 0)))],
        out_specs=pl.BlockSpec((tm, LANES), lambda i, j: (i, j)),
        out_shape=jax.ShapeDtypeStruct((t, ngroups * LANES), F32),
        compiler_params=_cp(("arbitrary", "arbitrary")),
        name="rope",
    )(x, tab)


def _rope_tables(pos):
    p = pos.astype(F32)[:, None]

    def cs(half):
        inv = ROPE_THETA ** (-jnp.arange(half, dtype=F32) / half)
        ang = p * inv[None, :]
        return jnp.cos(ang), jnp.sin(ang)

    c, s = cs(64)
    tab128 = jnp.concatenate([c, c, -s, s, jnp.zeros_like(c), jnp.zeros_like(c)], axis=1)
    c, s = cs(32)
    z = jnp.zeros_like(c)
    one64 = jnp.ones((pos.shape[0], 64), F32)
    z64 = jnp.zeros((pos.shape[0], 64), F32)
    kind0 = jnp.concatenate([c, c, c, c, z, s, z, s, -s, z, -s, z], axis=1)
    kind1 = jnp.concatenate([c, c, one64, z, s, z64, -s, z, z64], axis=1)
    return tab128, jnp.concatenate([kind0, kind1], axis=1)


def _gdn_kernel(qkv_ref, z_ref, gate_ref, buf_ref, cw_ref, alog_ref, dtb_ref, ng_ref, s0_ref,
                o_ref, sout_ref, xbuf_ref, s_ref, *, c, t_valid, gate_off):
    ci = pl.program_id(1)
    nh, dk = A_HEADS, A_DK
    hw = nh * dk

    @pl.when(ci == 0)
    def _():
        s_ref[...] = s0_ref[0]
        xbuf_ref[0:8, :] = jnp.zeros((8, 3 * hw), F32)
        xbuf_ref[8 - (A_CONV - 1):8, :] = buf_ref[0]

    xbuf_ref[8:8 + c, :] = qkv_ref[0]
    y = xbuf_ref[5:5 + c, :] * cw_ref[0:1, :]
    for j in range(1, A_CONV):
        y = y + xbuf_ref[5 + j:5 + j + c, :] * cw_ref[j:j + 1, :]
    tail = xbuf_ref[8 + c - 3:8 + c, :]
    xbuf_ref[5:8, :] = tail
    y = _silu(y)

    gt = gate_ref[0]
    ba = gt[:, gate_off:gate_off + nh]
    aa = gt[:, gate_off + nh:gate_off + 2 * nh]
    beta = _sigmoid(ba)
    xs = aa + dtb_ref[...]
    softplus = jnp.maximum(xs, 0.0) + jnp.log1p(jnp.exp(-jnp.abs(xs)))
    g = -jnp.exp(alog_ref[...]) * softplus
    row = lax.broadcasted_iota(I32, (c, c), 0)
    col = lax.broadcasted_iota(I32, (c, c), 1)
    if t_valid < c:
        valid = lax.broadcasted_iota(I32, (c, nh), 0) < t_valid
        beta = jnp.where(valid, beta, 0.0)
        g = jnp.where(valid, g, 0.0)
    incl = row >= col
    strict = row > col
    tri = jnp.where(incl, 1.0, 0.0).astype(BF16)
    gc = _dot_exact_left(tri, g)
    eye_h = jnp.where(lax.broadcasted_iota(I32, (nh, nh), 0) == lax.broadcasted_iota(I32, (nh, nh), 1),
                      1.0, 0.0).astype(BF16)
    gct = _dot_exact_left(eye_h, gc, NT)
    eye_c = jnp.where(row == col, 1.0, 0.0)
    n_dbl = int(round(math.log2(c))) - 1

    for h in range(nh):
        qh = y[:, h * dk:(h + 1) * dk]
        kh = y[:, hw + h * dk:hw + (h + 1) * dk]
        vh = y[:, 2 * hw + h * dk:2 * hw + (h + 1) * dk]
        qh = qh * lax.rsqrt(jnp.sum(qh * qh, axis=-1, keepdims=True) + 1e-6)
        kh = kh * lax.rsqrt(jnp.sum(kh * kh, axis=-1, keepdims=True) + 1e-6)
        bh = beta[:, h:h + 1]
        gch = gc[:, h:h + 1]
        decay = jnp.where(incl, jnp.exp(jnp.minimum(gch - gct[h:h + 1, :], 0.0)), 0.0)
        kb = kh * bh
        vb = vh * bh
        a_mat = jnp.where(strict, _dot3(kb, kh, NT) * decay, 0.0)
        pw = -a_mat
        t_inv = eye_c + pw
        for _ in range(n_dbl):
            pw = _dot3(pw, pw)
            t_inv = t_inv + _dot3(t_inv, pw)
        egc = jnp.exp(gch)
        u = _dot3(t_inv, vb)
        w = _dot3(t_inv, kb * egc)
        qs = qh * dk ** -0.5
        qk = _dot3(qs, kh, NT) * decay
        g_last = gch[c - 1:c, :]
        k_dec = kh * jnp.exp(jnp.minimum(g_last - gch, 0.0))
        s_old = s_ref[h]
        v_new = u - _dot3(w, s_old)
        o = _dot3(qs * egc, s_old) + _dot3(qk, v_new)
        s_ref[h] = s_old * jnp.exp(g_last) + _dot3(k_dec, v_new, TN)
        on = o * lax.rsqrt(jnp.mean(o * o, axis=-1, keepdims=True) + EPS) * ng_ref[...]
        o_ref[0, :, h * dk:(h + 1) * dk] = on * _silu(z_ref[0, :, h * dk:(h + 1) * dk])

    @pl.when(ci == pl.num_programs(1) - 1)
    def _():
        sout_ref[0] = s_ref[...]


def _gdn(proj, small, conv_buf, conv_w, a_log, dt_bias, norm_g, s0, *, c, t_valid, gate_group, gate_off):
    b, t = proj.shape[:2]
    nh, dk = A_HEADS, A_DK
    hw = nh * dk
    return pl.pallas_call(
        functools.partial(_gdn_kernel, c=c, t_valid=t_valid, gate_off=gate_off),
        grid=(b, t // c),
        in_specs=[pl.BlockSpec((1, c, 3 * hw), lambda bi, ci: (bi, ci, 0)),
                  pl.BlockSpec((1, c, hw), lambda bi, ci: (bi, ci, 3)),
                  pl.BlockSpec((1, c, LANES), lambda bi, ci: (bi, ci, gate_group)),
                  pl.BlockSpec((1, A_CONV - 1, 3 * hw), lambda bi, ci: (bi, 0, 0)),
                  pl.BlockSpec((A_CONV, 3 * hw), lambda bi, ci: (0, 0)),
                  pl.BlockSpec((1, nh), lambda bi, ci: (0, 0)),
                  pl.BlockSpec((1, nh), lambda bi, ci: (0, 0)),
                  pl.BlockSpec((1, dk), lambda bi, ci: (0, 0)),
                  pl.BlockSpec((1, nh, dk, dk), lambda bi, ci: (bi, 0, 0, 0))],
        out_specs=[pl.BlockSpec((1, c, hw), lambda bi, ci: (bi, ci, 0)),
                   pl.BlockSpec((1, nh, dk, dk), lambda bi, ci: (bi, 0, 0, 0))],
        out_shape=[jax.ShapeDtypeStruct((b, t, hw), F32), jax.ShapeDtypeStruct((b, nh, dk, dk), F32)],
        scratch_shapes=[pltpu.VMEM((8 + c, 3 * hw), F32), pltpu.VMEM((nh, dk, dk), F32)],
        compiler_params=_cp(("arbitrary", "arbitrary"), 40),
        name="gdn",
    )(proj, proj, small, conv_buf, conv_w, a_log.reshape(1, nh), dt_bias.reshape(1, nh), norm_g.reshape(1, dk), s0)


def _idx_prep_kernel(x_ref, q_ref, k_ref):
    tm = x_ref.shape[0]
    lo_half = lax.broadcasted_iota(I32, (tm, LANES), 1) < 64

    def split(xg):
        hi = xg.astype(BF16).astype(F32)
        return hi, xg - hi

    for gq in range(IDX_HEADS // 2):
        hi, lo = split(x_ref[:, gq * LANES:(gq + 1) * LANES])
        hi_r = pltpu.roll(hi, 64, 1)
        lo_r = pltpu.roll(lo, 64, 1)
        base = 2 * gq * 256
        q_ref[:, base:base + 128] = jnp.where(lo_half, hi, hi_r).astype(BF16)
        q_ref[:, base + 128:base + 256] = jnp.where(lo_half, lo, lo_r).astype(BF16)
        q_ref[:, base + 256:base + 384] = jnp.where(lo_half, hi_r, hi).astype(BF16)
        q_ref[:, base + 384:base + 512] = jnp.where(lo_half, lo_r, lo).astype(BF16)
    hi, lo = split(x_ref[:, 4 * LANES:5 * LANES])
    kk = jnp.where(lo_half, hi, pltpu.roll(lo, 64, 1)).astype(BF16)
    k_ref[:, 0:128] = kk
    k_ref[:, 128:256] = kk


def _idx_prep(small_r, tm=512):
    t = small_r.shape[0]
    return pl.pallas_call(
        _idx_prep_kernel,
        grid=(t // tm,),
        in_specs=[pl.BlockSpec((tm, 5 * LANES), lambda i: (i, 0))],
        out_specs=[pl.BlockSpec((tm, IDX_HEADS * 256), lambda i: (i, 0)),
                   pl.BlockSpec((tm, 256), lambda i: (i, 0))],
        out_shape=[jax.ShapeDtypeStruct((t, IDX_HEADS * 256), BF16), jax.ShapeDtypeStruct((t, 256), BF16)],
        compiler_params=_cp(("arbitrary",)),
        name="idx_prep",
    )(small_r)


IDX_SCALE = IDX_HEADS ** -0.5 * IDX_HD ** -0.5
W_OFF = 64


def _idx_scores_kernel(q_ref, k_ref, w_ref, o_ref, *, tq, ts):
    i = pl.program_id(0)
    j = pl.program_id(1)
    live = j * ts <= i * tq + tq - 1

    @pl.when(live)
    def _():
        k = k_ref[...]
        wg = w_ref[...] * IDX_SCALE
        acc = jnp.zeros((tq, ts), F32)
        for h in range(IDX_HEADS):
            s = _dot(q_ref[:, h * 256:(h + 1) * 256], k, NT)
            acc = acc + wg[:, W_OFF + h:W_OFF + h + 1] * jnp.maximum(s, 0.0)
        row = i * tq + lax.broadcasted_iota(I32, (tq, ts), 0)
        col = j * ts + lax.broadcasted_iota(I32, (tq, ts), 1)
        o_ref[...] = jnp.where(col <= row, acc, -jnp.inf)

    @pl.when(jnp.logical_not(live))
    def _():
        o_ref[...] = jnp.full((tq, ts), -jnp.inf, F32)


def _idx_scores(qcat, kcat, small_r, *, tq=256, ts=512):
    t = qcat.shape[0]
    return pl.pallas_call(
        functools.partial(_idx_scores_kernel, tq=tq, ts=ts),
        grid=(t // tq, t // ts),
        in_specs=[pl.BlockSpec((tq, IDX_HEADS * 256), lambda i, j: (i, 0)),
                  pl.BlockSpec((ts, 256), lambda i, j: (j, 0)),
                  pl.BlockSpec((tq, LANES), lambda i, j: (i, 4))],
        out_specs=pl.BlockSpec((tq, ts), lambda i, j: (i, j)),
        out_shape=jax.ShapeDtypeStruct((t, t), F32),
        compiler_params=_cp(("arbitrary", "arbitrary")),
        name="idx_scores",
    )(qcat, kcat, small_r)


def _thr_kernel(s_ref, thr_ref, need_ref, tie_ref, key_ref, *, k, cw, causal):
    tr, s_cols = s_ref.shape
    nchunks = ((pl.program_id(0) + 1) * tr + cw - 1) // cw if causal else s_cols // cw
    kf = float(k)

    def fill(ci, carry):
        off = pl.multiple_of(ci * cw, cw)
        bits = pltpu.bitcast(s_ref[:, pl.ds(off, cw)] + 0.0, I32)
        key_ref[:, pl.ds(off, cw)] = jnp.where(bits < 0, bits ^ 0x7FFFFFFF, bits)
        return carry

    lax.fori_loop(0, nchunks, fill, 0)

    def count(cand, strict):
        def cbody(ci, acc):
            off = pl.multiple_of(ci * cw, cw)
            blk = key_ref[:, pl.ds(off, cw)]
            for t in range(cw // LANES):
                kk = blk[:, t * LANES:(t + 1) * LANES]
                acc = acc + jnp.where(kk > cand if strict else kk >= cand, 1.0, 0.0)
            return acc

        acc = lax.fori_loop(0, nchunks, cbody, jnp.zeros((tr, LANES), F32))
        return jnp.sum(acc, axis=-1, keepdims=True)

    p0 = jnp.where(count(0, False) >= kf, 0, INT_MIN).astype(I32)

    def body(b, p):
        cand = p | jnp.left_shift(jnp.int32(1), 30 - b)
        return jnp.where(count(cand, False) >= kf, cand, p)

    p = lax.fori_loop(0, 31, body, p0)
    thr_ref[...] = pltpu.bitcast(jnp.where(p < 0, p ^ 0x7FFFFFFF, p), F32)
    need_ref[...] = kf - count(p, True)
    tie_ref[...] = jnp.where(jnp.logical_and(count(p, False) > kf, p != KEY_NEG_INF), 1, 0).astype(I32)


def _topk_thr(scores, *, k, cw, causal, tr):
    r, s = scores.shape
    tr = min(tr, r)
    assert cw >= k and s % cw == 0
    return pl.pallas_call(
        functools.partial(_thr_kernel, k=k, cw=cw, causal=causal),
        grid=(r // tr,),
        in_specs=[pl.BlockSpec((tr, s), lambda i: (i, 0))],
        out_specs=[pl.BlockSpec((tr, 1), lambda i: (i, 0))] * 3,
        out_shape=[jax.ShapeDtypeStruct((r, 1), F32), jax.ShapeDtypeStruct((r, 1), F32),
                   jax.ShapeDtypeStruct((r, 1), I32)],
        scratch_shapes=[pltpu.VMEM((tr, s), I32)],
        compiler_params=_cp(("arbitrary",), 32),
        name="topk_thr",
    )(scores)


def _select_mask(sc, thr, need, causal, eq_before, tie):
    if not tie:
        return jnp.logical_and(sc >= thr, causal), None
    eq = sc == thr
    n = sc.shape[1]
    upper = jnp.where(lax.broadcasted_iota(I32, (n, n), 0) < lax.broadcasted_iota(I32, (n, n), 1), 1.0, 0.0)
    eqf = jnp.where(eq, 1.0, 0.0)
    rank = eq_before + _dot(eqf.astype(BF16), upper.astype(BF16))
    sel = jnp.logical_or(sc > thr, jnp.logical_and(eq, rank < need))
    return jnp.logical_and(sel, causal), eq_before + jnp.sum(eqf, axis=-1, keepdims=True)


def _flash_update(s, m_ref, acc_ref, r0, rows, vblk):
    ts = s.shape[1]
    m_old = m_ref[r0:r0 + rows, :]
    m_new = jnp.maximum(m_old, jnp.max(s, axis=-1, keepdims=True))
    p = jnp.exp(s - jnp.tile(m_new, (1, ts // LANES)))
    alpha = jnp.exp(m_old - m_new)
    acc_ref[r0:r0 + rows, :] = (jnp.tile(alpha, (1, acc_ref.shape[1] // LANES)) * acc_ref[r0:r0 + rows, :]
                                + _dot(p.astype(BF16), vblk))
    m_ref[r0:r0 + rows, :] = m_new


def _dsa_attn_kernel(tie_ref, q_ref, k_ref, v_ref, sc_ref, thr_ref, need_ref, o_ref,
                     kb_ref, vb_ref, qs_ref, m_ref, acc_ref, eq_ref, *, tq, ts):
    i = pl.program_id(1)
    hg = B_HEADS // B_KV
    d = B_HD

    @pl.when(i == 0)
    def _():
        kb_ref[...] = k_ref[...].astype(BF16)
        vb_ref[:, 0:d] = v_ref[...].astype(BF16)
        vb_ref[:, d:2 * d] = jnp.ones((vb_ref.shape[0], d), BF16)

    for h in range(hg):
        qs_ref[h * tq:(h + 1) * tq, :] = (q_ref[:, h * d:(h + 1) * d] * d ** -0.5).astype(BF16)
    m_ref[...] = jnp.full(m_ref.shape, NEG_BIG, F32)
    acc_ref[...] = jnp.zeros(acc_ref.shape, F32)
    eq_ref[...] = jnp.zeros(eq_ref.shape, F32)
    thr = thr_ref[...]
    need = need_ref[...]
    nblk = (i * tq + tq + ts - 1) // ts

    def run(tie):
        def body(jb, carry):
            off = pl.multiple_of(jb * ts, ts)
            sc = sc_ref[:, pl.ds(off, ts)]
            row = i * tq + lax.broadcasted_iota(I32, (tq, ts), 0)
            col = off + lax.broadcasted_iota(I32, (tq, ts), 1)
            mask, eq_new = _select_mask(sc, thr, need, col <= row, eq_ref[...], tie)
            if tie:
                eq_ref[...] = eq_new
            kblk = kb_ref[pl.ds(off, ts), :]
            vblk = vb_ref[pl.ds(off, ts), :]
            for h in range(hg):
                s = jnp.where(mask, _dot(qs_ref[h * tq:(h + 1) * tq, :], kblk, NT), NEG_BIG)
                _flash_update(s, m_ref, acc_ref, h * tq, tq, vblk)
            return carry

        lax.fori_loop(0, nblk, body, 0)

    has_tie = tie_ref[i] > 0

    @pl.when(has_tie)
    def _():
        run(True)

    @pl.when(jnp.logical_not(has_tie))
    def _():
        run(False)

    for h in range(hg):
        o_ref[:, h * d:(h + 1) * d] = acc_ref[h * tq:(h + 1) * tq, 0:d] / acc_ref[h * tq:(h + 1) * tq, d:2 * d]


def _dsa_attn(tie_blk, q_r, k_r, v_src, v_col0, scores, thr, need, *, tq, ts=512):
    t = q_r.shape[0]
    hg = B_HEADS // B_KV
    once = dict(pipeline_mode=pl.Buffered(1))
    grid_spec = pltpu.PrefetchScalarGridSpec(
        num_scalar_prefetch=1,
        grid=(B_KV, t // tq),
        in_specs=[pl.BlockSpec((tq, hg * B_HD), lambda g, i, tie: (i, g)),
                  pl.BlockSpec((t, B_HD), lambda g, i, tie: (0, g), **once),
                  pl.BlockSpec((t, B_HD), lambda g, i, tie: (0, v_col0 + g), **once),
                  pl.BlockSpec((tq, t), lambda g, i, tie: (i, 0)),
                  pl.BlockSpec((tq, 1), lambda g, i, tie: (i, 0)),
                  pl.BlockSpec((tq, 1), lambda g, i, tie: (i, 0))],
        out_specs=pl.BlockSpec((tq, hg * B_HD), lambda g, i, tie: (i, g)),
        scratch_shapes=[pltpu.VMEM((t, B_HD), BF16), pltpu.VMEM((t, 2 * B_HD), BF16),
                        pltpu.VMEM((hg * tq, B_HD), BF16), pltpu.VMEM((hg * tq, LANES), F32),
                        pltpu.VMEM((hg * tq, 2 * B_HD), F32), pltpu.VMEM((tq, 1), F32)])
    return pl.pallas_call(
        functools.partial(_dsa_attn_kernel, tq=tq, ts=ts),
        grid_spec=grid_spec,
        out_shape=jax.ShapeDtypeStruct((t, B_HEADS * B_HD), F32),
        compiler_params=_cp(("arbitrary", "arbitrary"), 56),
        name="dsa_attn",
    )(tie_blk, q_r, k_r, v_src, scores, thr, need)


def _lambda_value(lam_ref, lam_init):
    lp = lam_ref[...]
    a = jnp.sum(lp[0:1, :] * lp[1:2, :], axis=-1, keepdims=True)
    b = jnp.sum(lp[2:3, :] * lp[3:4, :], axis=-1, keepdims=True)
    return jnp.exp(a) - jnp.exp(b) + lam_init


def _diff_attn_kernel(q_ref, k_ref, v_ref, lam_ref, ng_ref, o_ref, kb_ref, vb_ref, qs_ref, m_ref, acc_ref,
                      *, tq, ts, rsub, lam_init):
    i = pl.program_id(1)
    d2 = 2 * C_HD
    rows = 2 * tq

    @pl.when(i == 0)
    def _():
        kb_ref[...] = k_ref[...].astype(BF16)
        vb_ref[:, 0:d2] = v_ref[...].astype(BF16)
        vb_ref[:, d2:2 * d2] = jnp.ones((vb_ref.shape[0], d2), BF16)

    q = q_ref[...] * C_HD ** -0.5
    first = lax.broadcasted_iota(I32, (tq, d2), 1) < C_HD
    qs_ref[0:tq, :] = jnp.where(first, q, 0.0).astype(BF16)
    qs_ref[tq:rows, :] = jnp.where(first, 0.0, q).astype(BF16)
    m_ref[...] = jnp.full(m_ref.shape, NEG_BIG, F32)
    acc_ref[...] = jnp.zeros(acc_ref.shape, F32)

    def step(jb, diag_col0):
        off = pl.multiple_of(jb * ts, ts)
        kblk = kb_ref[pl.ds(off, ts), :]
        vblk = vb_ref[pl.ds(off, ts), :]
        for r0 in range(0, rows, rsub):
            s = _dot(qs_ref[r0:r0 + rsub, :], kblk, NT)
            if diag_col0 is not None:
                qrow = (r0 % tq) + lax.broadcasted_iota(I32, (rsub, ts), 0)
                s = jnp.where(diag_col0 + lax.broadcasted_iota(I32, (rsub, ts), 1) <= qrow, s, NEG_BIG)
            _flash_update(s, m_ref, acc_ref, r0, rsub, vblk)

    def body(jb, carry):
        step(jb, None)
        return carry

    nfull = (i * tq) // ts
    lax.fori_loop(0, nfull, body, 0)
    for dj in range(tq // ts):
        step(nfull + dj, dj * ts)

    lam = _lambda_value(lam_ref, lam_init)
    o = (acc_ref[0:tq, 0:d2] / acc_ref[0:tq, d2:2 * d2]
         - lam * (acc_ref[tq:rows, 0:d2] / acc_ref[tq:rows, d2:2 * d2]))
    on = o * lax.rsqrt(jnp.mean(o * o, axis=-1, keepdims=True) + EPS) * ng_ref[...]
    o_ref[...] = on * (1.0 - lam_init)


def _diff_attn(qk_r, v_src, v_col0, c_lambda, c_norm_g, *, lam_init, tq=1024, ts=512, rsub=256):
    t = qk_r.shape[0]
    tq = min(tq, t)
    d2 = 2 * C_HD
    once = dict(pipeline_mode=pl.Buffered(1))
    return pl.pallas_call(
        functools.partial(_diff_attn_kernel, tq=tq, ts=ts, rsub=rsub, lam_init=lam_init),
        grid=(C_HEADS, t // tq),
        in_specs=[pl.BlockSpec((tq, d2), lambda h, i: (i, h)),
                  pl.BlockSpec((t, d2), lambda h, i: (0, C_HEADS + h), **once),
                  pl.BlockSpec((t, d2), lambda h, i: (0, v_col0 + h), **once),
                  pl.BlockSpec((4, C_HD), lambda h, i: (0, 0)),
                  pl.BlockSpec((1, d2), lambda h, i: (0, 0))],
        out_specs=pl.BlockSpec((tq, d2), lambda h, i: (i, h)),
        out_shape=jax.ShapeDtypeStruct((t, C_HEADS * d2), F32),
        scratch_shapes=[pltpu.VMEM((t, d2), BF16), pltpu.VMEM((t, 2 * d2), BF16), pltpu.VMEM((2 * tq, d2), BF16),
                        pltpu.VMEM((2 * tq, LANES), F32), pltpu.VMEM((2 * tq, 2 * d2), F32)],
        compiler_params=_cp(("arbitrary", "arbitrary"), 48),
        name="diff_attn",
    )(qk_r, qk_r, v_src, c_lambda, c_norm_g.reshape(1, d2))


TS_PAD = 8


def _page_specs(n_pg, blk, layer, n_prefetch):
    def spec(k):
        if n_prefetch == 1:
            return pl.BlockSpec((1, 1) + blk, lambda bi, p, pt: (layer, pt[bi, p * n_pg + k], 0, 0))
        return pl.BlockSpec((1, 1) + blk, lambda bi, p, pt, tie: (layer, pt[bi, p * n_pg + k], 0, 0))

    return [spec(k) for k in range(n_pg)]


def _idx_scores_s_kernel(pt_ref, q_ref, w_ref, *refs, n_pg, t_new):
    pools = refs[:n_pg]
    new_ref, o_ref, onew_ref = refs[n_pg:]
    q = q_ref[0]
    w = w_ref[0] * IDX_SCALE

    def weigh(s):
        acc = jnp.zeros((TS_PAD, s.shape[1]), F32)
        for h in range(IDX_HEADS):
            acc = acc + w[h * TS_PAD:(h + 1) * TS_PAD, :] * jnp.maximum(s[h * TS_PAD:(h + 1) * TS_PAD, :], 0.0)
        return acc

    for k in range(n_pg):
        o_ref[0, :, k * PAGE:(k + 1) * PAGE] = weigh(_dot3(q, pools[k][0, 0]))

    @pl.when(pl.program_id(1) == pl.num_programs(1) - 1)
    def _():
        tq = lax.broadcasted_iota(I32, (TS_PAD, PAGE), 0)
        tk = lax.broadcasted_iota(I32, (TS_PAD, PAGE), 1)
        ok = jnp.logical_and(tk <= tq, tk < t_new)
        onew_ref[0] = jnp.where(ok, weigh(_dot3(q, new_ref[0], NT)), -jnp.inf)


def _idx_scores_s(page_table, q_s, w_s, pool_ki_t, ki_new, *, layer, t_new, n_pg=8):
    b, n_pages = page_table.shape
    grid_spec = pltpu.PrefetchScalarGridSpec(
        num_scalar_prefetch=1,
        grid=(b, n_pages // n_pg),
        in_specs=[pl.BlockSpec((1, IDX_HEADS * TS_PAD, IDX_HD), lambda bi, p, pt: (bi, 0, 0)),
                  pl.BlockSpec((1, IDX_HEADS * TS_PAD, 1), lambda bi, p, pt: (bi, 0, 0))]
        + _page_specs(n_pg, (IDX_HD, PAGE), layer, 1)
        + [pl.BlockSpec((1, PAGE, IDX_HD), lambda bi, p, pt: (bi, 0, 0))],
        out_specs=[pl.BlockSpec((1, TS_PAD, n_pg * PAGE), lambda bi, p, pt: (bi, 0, p)),
                   pl.BlockSpec((1, TS_PAD, PAGE), lambda bi, p, pt: (bi, 0, 0))])
    return pl.pallas_call(
        functools.partial(_idx_scores_s_kernel, n_pg=n_pg, t_new=t_new),
        grid_spec=grid_spec,
        out_shape=[jax.ShapeDtypeStruct((b, TS_PAD, n_pages * PAGE), F32),
                   jax.ShapeDtypeStruct((b, TS_PAD, PAGE), F32)],
        compiler_params=_cp(("arbitrary", "arbitrary")),
        name="idx_scores_sample",
    )(page_table, q_s, w_s, *([pool_ki_t] * n_pg), ki_new)


def _dsa_attn_s_kernel(pt_ref, tie_ref, q_ref, *refs, n_pg):
    pk = refs[:n_pg]
    pv = refs[n_pg:2 * n_pg]
    nk_ref, nv_ref, sc_ref, scn_ref, thr_ref, need_ref, o_ref, m_ref, l_ref, acc_ref, eq_ref = refs[2 * n_pg:]
    bi = pl.program_id(0)
    p = pl.program_id(1)
    hg = B_HEADS // B_KV

    @pl.when(p == 0)
    def _():
        m_ref[...] = jnp.full(m_ref.shape, NEG_BIG, F32)
        l_ref[...] = jnp.zeros(l_ref.shape, F32)
        acc_ref[...] = jnp.zeros(acc_ref.shape, F32)
        eq_ref[...] = jnp.zeros(eq_ref.shape, F32)

    thr = thr_ref[0]
    need = need_ref[0]
    scale = B_HD ** -0.5

    def attend(sc, keys_of, vals_of, tie):
        mask8, eq_new = _select_mask(sc, thr, need, sc > -jnp.inf, eq_ref[...], tie)
        if tie:
            eq_ref[...] = eq_new
        mask = jnp.concatenate([mask8] * hg, axis=0)
        for g in range(B_KV):
            s = jnp.where(mask, _dot3(q_ref[0, g] * scale, keys_of(g), NT), NEG_BIG)
            m_old = m_ref[g]
            m_new = jnp.maximum(m_old, jnp.max(s, axis=-1, keepdims=True))
            pr = jnp.where(mask, jnp.exp(s - m_new), 0.0)
            alpha = jnp.exp(m_old - m_new)
            l_ref[g] = alpha * l_ref[g] + jnp.sum(pr, axis=-1, keepdims=True)
            acc_ref[g] = alpha * acc_ref[g] + _dot3(pr, vals_of(g))
            m_ref[g] = m_new

    def paged(refs_):
        return lambda g: jnp.concatenate([r[0, 0, pl.ds(g, PAGE, stride=B_KV), :] for r in refs_], axis=0)

    def run(tie):
        attend(sc_ref[0], paged(pk), paged(pv), tie)

        @pl.when(p == pl.num_programs(1) - 1)
        def _():
            attend(scn_ref[0], lambda g: nk_ref[0, :, g * B_HD:(g + 1) * B_HD],
                   lambda g: nv_ref[0, :, g * B_HD:(g + 1) * B_HD], tie)

    has_tie = tie_ref[bi] > 0

    @pl.when(has_tie)
    def _():
        run(True)

    @pl.when(jnp.logical_not(has_tie))
    def _():
        run(False)

    @pl.when(p == pl.num_programs(1) - 1)
    def _():
        for g in range(B_KV):
            o_ref[0, g] = acc_ref[g] / l_ref[g]


def _dsa_attn_s(page_table, tie_b, q_s, pool_k2, pool_v2, k_new, v_new, scores, scores_new, thr, need,
                *, layer, n_pg=4):
    b, n_pages = page_table.shape
    hg = B_HEADS // B_KV
    rows = hg * TS_PAD
    kvw = B_KV * B_HD
    grid_spec = pltpu.PrefetchScalarGridSpec(
        num_scalar_prefetch=2,
        grid=(b, n_pages // n_pg),
        in_specs=[pl.BlockSpec((1, B_KV, rows, B_HD), lambda bi, p, pt, tie: (bi, 0, 0, 0))]
        + _page_specs(n_pg, (PAGE * B_KV, B_HD), layer, 2) + _page_specs(n_pg, (PAGE * B_KV, B_HD), layer, 2)
        + [pl.BlockSpec((1, PAGE, kvw), lambda bi, p, pt, tie: (bi, 0, 0)),
           pl.BlockSpec((1, PAGE, kvw), lambda bi, p, pt, tie: (bi, 0, 0)),
           pl.BlockSpec((1, TS_PAD, n_pg * PAGE), lambda bi, p, pt, tie: (bi, 0, p)),
           pl.BlockSpec((1, TS_PAD, PAGE), lambda bi, p, pt, tie: (bi, 0, 0)),
           pl.BlockSpec((1, TS_PAD, 1), lambda bi, p, pt, tie: (bi, 0, 0)),
           pl.BlockSpec((1, TS_PAD, 1), lambda bi, p, pt, tie: (bi, 0, 0))],
        out_specs=pl.BlockSpec((1, B_KV, rows, B_HD), lambda bi, p, pt, tie: (bi, 0, 0, 0)),
        scratch_shapes=[pltpu.VMEM((B_KV, rows, 1), F32), pltpu.VMEM((B_KV, rows, 1), F32),
                        pltpu.VMEM((B_KV, rows, B_HD), F32), pltpu.VMEM((TS_PAD, 1), F32)])
    return pl.pallas_call(
        functools.partial(_dsa_attn_s_kernel, n_pg=n_pg),
        grid_spec=grid_spec,
        out_shape=jax.ShapeDtypeStruct((b, B_KV, rows, B_HD), F32),
        compiler_params=_cp(("arbitrary", "arbitrary")),
        name="dsa_attn_sample",
    )(page_table, tie_b, q_s, *([pool_k2] * n_pg), *([pool_v2] * n_pg), k_new, v_new, scores, scores_new, thr, need)


def _diff_attn_s_kernel(pt_ref, wt_ref, *refs, n_pg, t_new, lam_init):
    pk = refs[:n_pg]
    pv = refs[n_pg:2 * n_pg]
    nk_ref, nv_ref, lam_ref, ng_ref, o_ref, m_ref, l_ref, acc_ref = refs[2 * n_pg:]
    p = pl.program_id(1)
    nrow = wt_ref.shape[1]
    rph = 2 * t_new
    dv = 2 * C_HD

    @pl.when(p == 0)
    def _():
        m_ref[...] = jnp.full(m_ref.shape, NEG_BIG, F32)
        l_ref[...] = jnp.zeros(l_ref.shape, F32)
        acc_ref[...] = jnp.zeros(acc_ref.shape, F32)

    wt = wt_ref[0]

    def attend(s, vals_of):
        m_old = m_ref[...]
        m_new = jnp.maximum(m_old, jnp.max(s, axis=-1, keepdims=True))
        pr = jnp.exp(s - m_new)
        alpha = jnp.exp(m_old - m_new)
        l_ref[...] = alpha * l_ref[...] + jnp.sum(pr, axis=-1, keepdims=True)
        prb = pr.astype(BF16)
        for h in range(C_HEADS):
            r0 = h * rph
            acc_ref[r0:r0 + rph, :] = alpha[r0:r0 + rph, :] * acc_ref[r0:r0 + rph, :] + _dot(prb[r0:r0 + rph, :],
                                                                                           vals_of(h))
        m_ref[...] = m_new

    s_past = jnp.concatenate([_dot(wt, r[0, 0].astype(BF16)) for r in pk], axis=1)
    attend(s_past, lambda h: jnp.concatenate([r[0, 0, pl.ds(h, PAGE, stride=C_HEADS), :] for r in pv],
                                             axis=0).astype(BF16))

    @pl.when(p == pl.num_programs(1) - 1)
    def _():
        tq = lax.broadcasted_iota(I32, (nrow, TS_PAD), 0) & (t_new - 1)
        tk = lax.broadcasted_iota(I32, (nrow, TS_PAD), 1)
        s_new = jnp.where(tk <= tq, _dot(wt, nk_ref[0].astype(BF16), NT), NEG_BIG)
        attend(s_new, lambda h: nv_ref[0, :, h * dv:(h + 1) * dv].astype(BF16))
        lam = _lambda_value(lam_ref, lam_init)
        a = acc_ref[...] / l_ref[...]
        for h in range(C_HEADS):
            r0 = h * rph
            o = a[r0:r0 + t_new, :] - lam * a[r0 + t_new:r0 + rph, :]
            on = o * lax.rsqrt(jnp.mean(o * o, axis=-1, keepdims=True) + EPS) * ng_ref[...]
            o_ref[0, :, h * dv:(h + 1) * dv] = on * (1.0 - lam_init)


def _diff_attn_s(page_table, wt, pool_kt, pool_v2, k_new, v_new, c_lambda, c_norm_g, *, layer, t_new, lam_init,
                 n_pg=4):
    b, n_pages = page_table.shape
    nrow, width = wt.shape[1:]
    dv = 2 * C_HD
    grid_spec = pltpu.PrefetchScalarGridSpec(
        num_scalar_prefetch=1,
        grid=(b, n_pages // n_pg),
        in_specs=[pl.BlockSpec((1, nrow, width), lambda bi, p, pt: (bi, 0, 0))]
        + _page_specs(n_pg, (width, PAGE), layer, 1) + _page_specs(n_pg, (PAGE * C_HEADS, dv), layer, 1)
        + [pl.BlockSpec((1, TS_PAD, width), lambda bi, p, pt: (bi, 0, 0)),
           pl.BlockSpec((1, TS_PAD, width), lambda bi, p, pt: (bi, 0, 0)),
           pl.BlockSpec((4, C_HD), lambda bi, p, pt: (0, 0)),
           pl.BlockSpec((1, dv), lambda bi, p, pt: (0, 0))],
        out_specs=pl.BlockSpec((1, t_new, width), lambda bi, p, pt: (bi, 0, 0)),
        scratch_shapes=[pltpu.VMEM((nrow, 1), F32), pltpu.VMEM((nrow, 1), F32), pltpu.VMEM((nrow, dv), F32)])
    return pl.pallas_call(
        functools.partial(_diff_attn_s_kernel, n_pg=n_pg, t_new=t_new, lam_init=lam_init),
        grid_spec=grid_spec,
        out_shape=jax.ShapeDtypeStruct((b, t_new, width), F32),
        compiler_params=_cp(("arbitrary", "arbitrary"), 40),
        name="diff_attn_sample",
    )(page_table, wt, *([pool_kt] * n_pg), *([pool_v2] * n_pg), k_new, v_new, c_lambda,
      c_norm_g.reshape(1, dv))


def _split_ab_weights(w_in):
    n_a = 4 * A_HEADS * A_DK
    n_gate = 2 * A_HEADS
    n_b = (B_HEADS + 2 * B_KV) * B_HD
    n_idx = IDX_HEADS * IDX_HD + IDX_HD + IDX_HEADS
    big = jnp.concatenate([w_in[:, :n_a], w_in[:, n_a + n_gate:n_a + n_gate + n_b]], axis=1)
    pad = 5 * LANES - n_idx - n_gate
    small = jnp.concatenate([w_in[:, n_a + n_gate + n_b:], w_in[:, n_a:n_a + n_gate],
                             jnp.zeros((w_in.shape[0], pad), w_in.dtype)], axis=1)
    return big, small


PROMPT_TM = 1024
GATE_OFF = IDX_HD + IDX_HEADS
QB_COL0 = 4 * A_HEADS * A_DK // LANES
KB_COL0 = QB_COL0 + B_HEADS
VB_COL0 = KB_COL0 + B_KV


def _mlp(x, mods, norm_g, w1, w2, *, tm, hi):
    sh2, sc2, g2 = mods
    hid = _norm_matmul(x, norm_g[2:3], sc2, sh2, w1, tm=tm, tn=512, hi=hi, relu2=True,
                       out_dtype=F32 if hi else BF16)
    if hi:
        return _matmul_norm_res(hid, w2, norm_g[3:4], g2, x, tm=tm, tk=512, hi=True)
    return _matmul_norm_res(hid, w2, norm_g[3:4], g2, x, tm=512, tk=2048, vmem_mb=56)


def _prompt_trunk(x, mods, P):
    t = x.shape[0]
    pos = jnp.arange(t, dtype=I32)
    tab128, tab64 = _rope_tables(pos)
    sh1, sc1, g1, sh2, sc2, g2 = mods[0]
    ng = P["norm_g"][0]
    big = _norm_matmul(x, ng[0:1], sc1, sh1, P["w_big"], tm=PROMPT_TM, tn=512)
    small = _norm_matmul(x, ng[0:1], sc1, sh1, P["w_small"], tm=PROMPT_TM, tn=5 * LANES, hi=True)
    small_r = _rope(small, tab64, col0=0, ngroups=5, n_kind0=4, r1=32, r2=96, tm=1024)
    qb_r = _rope(big, tab128, col0=QB_COL0, ngroups=B_HEADS, n_kind0=B_HEADS, r1=64, r2=None, tm=1024)
    kb_r = _rope(big, tab128, col0=KB_COL0, ngroups=B_KV, n_kind0=B_KV, r1=64, r2=None, tm=1024)
    qcat, kcat = _idx_prep(small_r)
    scores = _idx_scores(qcat, kcat, small_r)
    thr, need, tie = _topk_thr(scores, k=min(TOPK, t // 4), cw=1024, causal=True, tr=128)
    tq = 256
    tie_blk = jnp.max(tie.reshape(t // tq, tq), axis=1)
    ob = _dsa_attn(tie_blk, qb_r, kb_r, big, VB_COL0, scores, thr, need, tq=tq)
    oa, s_new = _gdn(big[None], small_r[None], jnp.zeros((1, A_CONV - 1, 3 * A_HEADS * A_DK), F32), P["ab_conv_w"],
                     P["ab_A_log"], P["ab_dt_bias"], P["ab_norm_g"], jnp.zeros((1, A_HEADS, A_DK, A_DK), F32),
                     c=64, t_valid=64, gate_group=4, gate_off=GATE_OFF)
    mix = jnp.concatenate([oa[0], ob], axis=1)
    x = _matmul_norm_res(mix, P["ab_w_out"], ng[1:2], g1, x, tm=512, tk=2048, vmem_mb=56)
    x = _mlp(x, (sh2, sc2, g2), ng, P["mlp_w1"][0], P["mlp_w2"][0], tm=PROMPT_TM, hi=False)
    outs0 = (kb_r, big[:, VB_COL0 * LANES:], small_r[:, 4 * LANES:4 * LANES + IDX_HD], s_new,
             big[t - (A_CONV - 1):, :3 * A_HEADS * A_DK])
    sh1, sc1, g1, sh2, sc2, g2 = mods[1]
    ng = P["norm_g"][1]
    lam_init = 0.8 - 0.6 * math.exp(-0.3 * 1)
    proj = _norm_matmul(x, ng[0:1], sc1, sh1, P["c_w_in"], tm=PROMPT_TM, tn=512)
    qk_r = _rope(proj, tab64, col0=0, ngroups=2 * C_HEADS, n_kind0=2 * C_HEADS, r1=32, r2=96, tm=1024)
    o = _diff_attn(qk_r, proj, 2 * C_HEADS, P["c_lambda"], P["c_norm_g"], lam_init=lam_init)
    x = _matmul_norm_res(o, P["c_w_out"], ng[1:2], g1, x, tm=512, tk=2048, vmem_mb=56)
    x = _mlp(x, (sh2, sc2, g2), ng, P["mlp_w1"][1], P["mlp_w2"][1], tm=PROMPT_TM, hi=False)
    outs1 = (qk_r[:, 2 * C_HEADS * C_HD:], proj[:, 4 * C_HEADS * C_HD:])
    return x, outs0, outs1


def _sample_trunk(x, mods, P, cache, past_len):
    b, ts_, d = x.shape
    m = b * ts_
    x = x.reshape(m, d)
    pos = past_len + jnp.arange(ts_, dtype=I32)
    tab128, tab64 = _rope_tables(pos)
    tab128 = jnp.tile(tab128, (b, 1))
    tab64 = jnp.tile(tab64, (b, 1))
    page_table = cache["page_table"]
    n_pages = page_table.shape[1]
    hw = A_HEADS * A_DK

    def rows(a):
        return jnp.repeat(a, ts_, axis=0)

    def pad_t(a, n):
        return jnp.pad(a, ((0, 0), (0, n - ts_), (0, 0)))

    sh1, sc1, g1, sh2, sc2, g2 = (rows(a) for a in mods[0])
    ng = P["norm_g"][0]
    big = _norm_matmul(x, ng[0:1], sc1, sh1, P["w_big"], tm=m, tn=512, hi=True)
    small = _norm_matmul(x, ng[0:1], sc1, sh1, P["w_small"], tm=m, tn=5 * LANES, hi=True)
    small_r = _rope(small, tab64, col0=0, ngroups=5, n_kind0=4, r1=32, r2=96, tm=m)
    qb_r = _rope(big, tab128, col0=QB_COL0, ngroups=B_HEADS, n_kind0=B_HEADS, r1=64, r2=None, tm=m)
    kb_r = _rope(big, tab128, col0=KB_COL0, ngroups=B_KV, n_kind0=B_KV, r1=64, r2=None, tm=m)
    vb = big[:, VB_COL0 * LANES:]
    sm3 = small_r.reshape(b, ts_, 5 * LANES)
    qi = sm3[:, :, :IDX_HEADS * IDX_HD].reshape(b, ts_, IDX_HEADS, IDX_HD)
    qi = jnp.pad(jnp.swapaxes(qi, 1, 2), ((0, 0), (0, 0), (0, TS_PAD - ts_), (0, 0)))
    qi = qi.reshape(b, IDX_HEADS * TS_PAD, IDX_HD)
    wi = sm3[:, :, 4 * LANES + W_OFF:4 * LANES + W_OFF + IDX_HEADS]
    wi = jnp.pad(jnp.swapaxes(wi, 1, 2), ((0, 0), (0, 0), (0, TS_PAD - ts_))).reshape(b, IDX_HEADS * TS_PAD, 1)
    ki_r = sm3[:, :, 4 * LANES:4 * LANES + IDX_HD]
    pool_ki_t = jnp.swapaxes(cache["ab_kidx"], 2, 3)
    sc_past, sc_new = _idx_scores_s(page_table, qi, wi, pool_ki_t, pad_t(ki_r, PAGE), layer=0, t_new=ts_,
                                    n_pg=min(8, n_pages))
    scores = jnp.concatenate([sc_past, sc_new], axis=2)
    ncols = scores.shape[2]
    k_sel = min(TOPK, (past_len + ts_) // 4)
    chunk = min(d for d in range(1, ncols // LANES + 1) if (ncols // LANES) % d == 0 and d * LANES >= k_sel) * LANES
    thr, need, tie = _topk_thr(scores.reshape(b * TS_PAD, ncols), k=k_sel, cw=chunk, causal=False, tr=64)
    tie_b = jnp.max(tie.reshape(b, TS_PAD)[:, :ts_], axis=1)
    hg = B_HEADS // B_KV
    q4 = qb_r.reshape(b, ts_, B_KV, hg, B_HD)
    q4 = jnp.pad(jnp.transpose(q4, (0, 2, 3, 1, 4)), ((0, 0), (0, 0), (0, 0), (0, TS_PAD - ts_), (0, 0)))
    q4 = q4.reshape(b, B_KV, hg * TS_PAD, B_HD)
    kvw = B_KV * B_HD
    pool_k2 = cache["ab_k"].reshape(cache["ab_k"].shape[:2] + (PAGE * B_KV, B_HD))
    pool_v2 = cache["ab_v"].reshape(cache["ab_v"].shape[:2] + (PAGE * B_KV, B_HD))
    ob = _dsa_attn_s(page_table, tie_b, q4, pool_k2, pool_v2, pad_t(kb_r.reshape(b, ts_, kvw), PAGE),
                     pad_t(vb.reshape(b, ts_, kvw), PAGE), sc_past, sc_new, thr.reshape(b, TS_PAD, 1),
                     need.reshape(b, TS_PAD, 1), layer=0, n_pg=min(8, n_pages))
    ob = ob.reshape(b, B_KV, hg, TS_PAD, B_HD)[:, :, :, :ts_]
    ob = jnp.transpose(ob, (0, 3, 1, 2, 4)).reshape(m, B_HEADS * B_HD)
    oa, s_new = _gdn(pad_t(big.reshape(b, ts_, -1), TS_PAD), pad_t(sm3, TS_PAD), cache["ab_conv"][0], P["ab_conv_w"],
                     P["ab_A_log"], P["ab_dt_bias"], P["ab_norm_g"], cache["ab_delta"][0],
                     c=TS_PAD, t_valid=ts_, gate_group=4, gate_off=GATE_OFF)
    mix = jnp.concatenate([oa[:, :ts_].reshape(m, hw), ob], axis=1)
    x = _matmul_norm_res(mix, P["ab_w_out"], ng[1:2], g1, x, tm=m, tk=512, hi=True)
    x = _mlp(x, (sh2, sc2, g2), ng, P["mlp_w1"][0], P["mlp_w2"][0], tm=m, hi=True)
    conv_in = jnp.concatenate([cache["ab_conv"][0], big.reshape(b, ts_, -1)[:, :, :3 * hw]], axis=1)
    outs0 = (kb_r, vb, ki_r, s_new, conv_in[:, ts_:])
    sh1, sc1, g1, sh2, sc2, g2 = (rows(a) for a in mods[1])
    ng = P["norm_g"][1]
    lam_init = 0.8 - 0.6 * math.exp(-0.3 * 1)
    proj = _norm_matmul(x, ng[0:1], sc1, sh1, P["c_w_in"], tm=m, tn=512, hi=True)
    qk_r = _rope(proj, tab64, col0=0, ngroups=2 * C_HEADS, n_kind0=2 * C_HEADS, r1=32, r2=96, tm=m)
    cw = 2 * C_HEADS * C_HD
    q3 = qk_r[:, :cw].reshape(b, ts_, cw) * C_HD ** -0.5
    k3 = qk_r[:, cw:].reshape(b, ts_, cw)
    v3 = proj[:, 2 * cw:].reshape(b, ts_, cw)
    lane_hp = jnp.arange(cw, dtype=I32) // C_HD
    want = (2 * jnp.arange(C_HEADS, dtype=I32)[:, None] + jnp.arange(2, dtype=I32)[None, :])
    sel = (lane_hp[None, None, :] == want[:, :, None]).astype(F32)
    wt = (q3[:, None, None, :, :] * sel[None, :, :, None, :]).reshape(b, 2 * C_HEADS * ts_, cw).astype(BF16)
    ck = cache["c_k"]
    pool_ckt = jnp.transpose(ck, (0, 1, 3, 4, 5, 2)).reshape(ck.shape[:2] + (cw, PAGE))
    pool_cv2 = cache["c_v"].reshape(cache["c_v"].shape[:2] + (PAGE * C_HEADS, 2 * C_HD))
    o = _diff_attn_s(page_table, wt, pool_ckt, pool_cv2, pad_t(k3, TS_PAD), pad_t(v3, TS_PAD), P["c_lambda"],
                     P["c_norm_g"], layer=0, t_new=ts_, lam_init=lam_init, n_pg=min(4, n_pages))
    x = _matmul_norm_res(o.reshape(m, cw), P["c_w_out"], ng[1:2], g1, x, tm=m, tk=512, hi=True)
    x = _mlp(x, (sh2, sc2, g2), ng, P["mlp_w1"][1], P["mlp_w2"][1], tm=m, hi=True)
    outs1 = (qk_r[:, cw:], proj[:, 2 * cw:])
    return x.reshape(b, ts_, d), outs0, outs1


def kernel(x_prompt, x_sample, cache_ab_k, cache_ab_v, cache_ab_kidx, state_ab_delta, state_ab_conv, cache_c_k,
           cache_c_v, page_table, c_prompt, c_sample, ada_w, ada_b, norm_g, mlp_w1, mlp_w2, ab_w_in, ab_w_out,
           ab_conv_w, ab_A_log, ab_dt_bias, ab_norm_g, c_w_in, c_w_out, c_lambda, c_norm_g):
    bp, t, d = x_prompt.shape
    bs, ts_, _ = x_sample.shape
    assert bp == 1
    past_len = page_table.shape[1] * PAGE
    w_big, w_small = _split_ab_weights(ab_w_in[0])
    P = {"norm_g": norm_g, "mlp_w1": mlp_w1, "mlp_w2": mlp_w2, "w_big": w_big, "w_small": w_small,
         "ab_w_out": ab_w_out[0], "ab_conv_w": ab_conv_w[0], "ab_A_log": ab_A_log[0], "ab_dt_bias": ab_dt_bias[0],
         "ab_norm_g": ab_norm_g[0], "c_w_in": c_w_in[0], "c_w_out": c_w_out[0], "c_lambda": c_lambda[0],
         "c_norm_g": c_norm_g[0]}
    n_seq = bp + bs
    mc = -(-n_seq // 8) * 8
    c_all = jnp.pad(jnp.concatenate([c_prompt, c_sample], axis=0), ((0, mc - n_seq), (0, 0)))
    mod = _ada(c_all, ada_w, ada_b)
    mods_p = [tuple(mod[i, 0:bp, n * d:(n + 1) * d] for n in range(6)) for i in range(2)]
    mods_s = [tuple(mod[i, bp:n_seq, n * d:(n + 1) * d] for n in range(6)) for i in range(2)]

    y_p, p0, p1 = _prompt_trunk(x_prompt[0], mods_p, P)
    cache = {"ab_k": cache_ab_k, "ab_v": cache_ab_v, "ab_kidx": cache_ab_kidx, "ab_delta": state_ab_delta,
             "ab_conv": state_ab_conv, "c_k": cache_c_k, "c_v": cache_c_v, "page_table": page_table}
    y_s, s0, s1 = _sample_trunk(x_sample, mods_s, P, cache, past_len)

    return (y_p[None], y_s,
            p0[0].reshape(1, 1, t, B_KV, B_HD), p0[1].reshape(1, 1, t, B_KV, B_HD), p0[2].reshape(1, 1, t, IDX_HD),
            p0[3][None], p0[4].reshape(1, 1, A_CONV - 1, -1),
            p1[0].reshape(1, 1, t, C_HEADS, 2, C_HD), p1[1].reshape(1, 1, t, C_HEADS, 2 * C_HD),
            s0[0].reshape(1, bs, ts_, B_KV, B_HD), s0[1].reshape(1, bs, ts_, B_KV, B_HD),
            s0[2].reshape(1, bs, ts_, IDX_HD), s0[3][None], s0[4][None],
            s1[0].reshape(1, bs, ts_, C_HEADS, 2, C_HD), s1[1].reshape(1, bs, ts_, C_HEADS, 2 * C_HD))
```

```python
import functools
import math

import jax
import jax.numpy as jnp
from jax import lax
from jax.experimental import pallas as pl
from jax.experimental.pallas import tpu as pltpu

F32 = jnp.float32
BF16 = jnp.bfloat16
I32 = jnp.int32

EPS = 1e-6
NEG_BIG = -1e30
ROPE_THETA = 10000.0
PAGE = 128
A_HEADS = 8
A_DK = 128
A_CONV = 4
B_HEADS = 8
B_KV = 2
B_HD = 128
IDX_HEADS = 8
IDX_HD = 64
TOPK = 256
C_HEADS = 16
C_HD = 64
LANES = 128

NN = (((1,), (0,)), ((), ()))
NT = (((1,), (1,)), ((), ()))
TN = (((0,), (0,)), ((), ()))
BNN = (((2,), (1,)), ((0,), (0,)))
BNT = (((2,), (2,)), ((0,), (0,)))
INT_MIN = -2147483648
KEY_NEG_INF = -2139095041


def _cp(dims, vmem_mb=None):
    kw = dict(dimension_semantics=dims)
    if vmem_mb is not None:
        kw["vmem_limit_bytes"] = vmem_mb << 20
    return pltpu.CompilerParams(**kw)


def _dot(a, b, dims=NN):
    return lax.dot_general(a, b, dims, preferred_element_type=F32)


def _split2(a):
    hi = a.astype(BF16)
    return hi, (a - hi.astype(F32)).astype(BF16)


def _split3(a):
    hi = a.astype(BF16)
    r = a - hi.astype(F32)
    mid = r.astype(BF16)
    return hi, mid, (r - mid.astype(F32)).astype(BF16)


def _dot1(a, b, dims=NN):
    return _dot(a.astype(BF16), b.astype(BF16), dims)


def _dot3(a, b, dims=NN):
    ah, al = _split2(a)
    bh, bl = _split2(b)
    return _dot(ah, bh, dims) + (_dot(ah, bl, dims) + _dot(al, bh, dims))


def _dot_exact_left(ones_bf16, b, dims=NN):
    b0, b1, b2 = _split3(b)
    return _dot(ones_bf16, b0, dims) + (_dot(ones_bf16, b1, dims) + _dot(ones_bf16, b2, dims))


def _sigmoid(x):
    return 1.0 / (1.0 + jnp.exp(-x))


def _silu(x):
    return x * _sigmoid(x)


def _ada_kernel(c_ref, w_ref, b_ref, o_ref):
    o_ref[0] = _dot3(_silu(c_ref[...]), w_ref[0]) + b_ref[0]


def _ada(c_all, ada_w, ada_b, tn=512):
    nl, d, n = ada_w.shape
    mc = c_all.shape[0]
    return pl.pallas_call(
        _ada_kernel,
        grid=(nl, n // tn),
        in_specs=[pl.BlockSpec((mc, d), lambda l, j: (0, 0)),
                  pl.BlockSpec((1, d, tn), lambda l, j: (l, 0, j)),
                  pl.BlockSpec((1, 1, tn), lambda l, j: (l, 0, j))],
        out_specs=pl.BlockSpec((1, mc, tn), lambda l, j: (l, 0, j)),
        out_shape=jax.ShapeDtypeStruct((nl, mc, n), F32),
        compiler_params=_cp(("arbitrary", "arbitrary"), 40),
        name="ada_mod",
    )(c_all, ada_w, ada_b.reshape(nl, 1, n))


def _norm_mm_kernel(x_ref, g_ref, sc_ref, sh_ref, w_ref, o_ref, *scratch, hi, relu2):
    hh_ref = scratch[0]

    @pl.when(pl.program_id(1) == 0)
    def _():
        x = x_ref[...]
        y = x * lax.rsqrt(jnp.mean(x * x, axis=-1, keepdims=True) + EPS)
        h = (y * g_ref[...]) * (1.0 + sc_ref[...]) + sh_ref[...]
        hh = h.astype(BF16)
        hh_ref[...] = hh
        if hi:
            scratch[1][...] = (h - hh.astype(F32)).astype(BF16)

    w = w_ref[0]
    wh = w.astype(BF16)
    acc = _dot(hh_ref[...], wh)
    if hi:
        wl = (w - wh.astype(F32)).astype(BF16)
        acc = acc + (_dot(hh_ref[...], wl) + _dot(scratch[1][...], wh))
    if relu2:
        acc = jnp.square(jnp.maximum(acc, 0.0))
    o_ref[...] = acc.astype(o_ref.dtype)


def _layer_weight(w):
    return w if isinstance(w, tuple) else (w[None], 0)


def _norm_matmul(x, g, sc, sh, w, *, tm, tn, hi=False, relu2=False, out_dtype=F32, vmem_mb=48):
    m, d = x.shape
    tm = min(tm, m)
    w, layer = _layer_weight(w)
    n = w.shape[2]
    per_row = sc.shape[0] != 1
    mod_spec = pl.BlockSpec((tm, d), lambda i, j: (i, 0)) if per_row else pl.BlockSpec((1, d), lambda i, j: (0, 0))
    scratch = [pltpu.VMEM((tm, d), BF16)] + ([pltpu.VMEM((tm, d), BF16)] if hi else [])
    return pl.pallas_call(
        functools.partial(_norm_mm_kernel, hi=hi, relu2=relu2),
        grid=(m // tm, n // tn),
        in_specs=[pl.BlockSpec((tm, d), lambda i, j: (i, 0), pipeline_mode=pl.Buffered(1)),
                  pl.BlockSpec((1, d), lambda i, j: (0, 0)),
                  mod_spec, mod_spec,
                  pl.BlockSpec((1, d, tn), lambda i, j: (layer, 0, j))],
        out_specs=pl.BlockSpec((tm, tn), lambda i, j: (i, j)),
        out_shape=jax.ShapeDtypeStruct((m, n), out_dtype),
        scratch_shapes=scratch,
        compiler_params=_cp(("arbitrary", "arbitrary"), vmem_mb),
        name="norm_matmul",
    )(x, g, sc, sh, w)


def _mm_norm_res_kernel(a_ref, w_ref, ng_ref, gate_ref, res_ref, o_ref, *, hi):
    k = pl.program_id(1)

    @pl.when(k == 0)
    def _():
        o_ref[...] = jnp.zeros(o_ref.shape, F32)

    if hi:
        o_ref[...] += _dot3(a_ref[...].astype(F32), w_ref[0])
    else:
        o_ref[...] += _dot1(a_ref[...], w_ref[0])

    @pl.when(k == pl.num_programs(1) - 1)
    def _():
        m = o_ref[...]
        y = m * lax.rsqrt(jnp.mean(m * m, axis=-1, keepdims=True) + EPS)
        o_ref[...] = res_ref[...] + gate_ref[...] * (y * ng_ref[...])


def _matmul_norm_res(a, w, ng, gate, res, *, tm, tk, hi=False, vmem_mb=48):
    m, kdim = a.shape
    w, layer = _layer_weight(w)
    n = w.shape[2]
    per_row = gate.shape[0] != 1
    gate_spec = pl.BlockSpec((tm, n), lambda i, k: (i, 0)) if per_row else pl.BlockSpec((1, n), lambda i, k: (0, 0))
    return pl.pallas_call(
        functools.partial(_mm_norm_res_kernel, hi=hi),
        grid=(m // tm, kdim // tk),
        in_specs=[pl.BlockSpec((tm, tk), lambda i, k: (i, k)),
                  pl.BlockSpec((1, tk, n), lambda i, k: (layer, k, 0)),
                  pl.BlockSpec((1, n), lambda i, k: (0, 0)),
                  gate_spec,
                  pl.BlockSpec((tm, n), lambda i, k: (i, 0), pipeline_mode=pl.Buffered(1))],
        out_specs=pl.BlockSpec((tm, n), lambda i, k: (i, 0)),
        out_shape=jax.ShapeDtypeStruct((m, n), F32),
        compiler_params=_cp(("arbitrary", "arbitrary"), vmem_mb),
        name="matmul_norm_res",
    )(a, w, ng, gate, res)


def _rope_kernel(x_ref, tab_ref, o_ref, *, r1, r2, gw):
    for k in range(gw):
        x = x_ref[:, k * LANES:(k + 1) * LANES]
        out = x * tab_ref[:, 0:LANES] + pltpu.roll(x, r1, 1) * tab_ref[:, LANES:2 * LANES]
        if r2 is not None:
            out = out + pltpu.roll(x, r2, 1) * tab_ref[:, 2 * LANES:3 * LANES]
        o_ref[:, k * LANES:(k + 1) * LANES] = out


def _rope(x, tab, *, col0, ngroups, n_kind0, r1, r2, tm, gw=1):
    t = x.shape[0]
    assert col0 % gw == 0 and ngroups % gw == 0 and n_kind0 % gw == 0
    return pl.pallas_call(
        functools.partial(_rope_kernel, r1=r1, r2=r2, gw=gw),
        grid=(t // tm, ngroups // gw),
        in_specs=[pl.BlockSpec((tm, gw * LANES), lambda i, j: (i, col0 // gw + j)),
                  pl.BlockSpec((tm, 3 * LANES), lambda i, j: (i, jnp.where(j * gw >= n_kind0, 1, 0)))],
        out_specs=pl.BlockSpec((tm, gw * LANES), lambda i, j: (i, j)),
        out_shape=jax.ShapeDtypeStruct((t, ngroups * LANES), F32),
        compiler_params=_cp(("arbitrary", "arbitrary")),
        name="rope",
    )(x, tab)


def _rope_tables(pos):
    p = pos.astype(F32)[:, None]

    def cs(half):
        inv = ROPE_THETA ** (-jnp.arange(half, dtype=F32) / half)
        ang = p * inv[None, :]
        return jnp.cos(ang), jnp.sin(ang)

    c, s = cs(64)
    tab128 = jnp.concatenate([c, c, -s, s, jnp.zeros_like(c), jnp.zeros_like(c)], axis=1)
    c, s = cs(32)
    z = jnp.zeros_like(c)
    one64 = jnp.ones((pos.shape[0], 64), F32)
    z64 = jnp.zeros((pos.shape[0], 64), F32)
    kind0 = jnp.concatenate([c, c, c, c, z, s, z, s, -s, z, -s, z], axis=1)
    kind1 = jnp.concatenate([c, c, one64, z, s, z64, -s, z, z64], axis=1)
    return tab128, jnp.concatenate([kind0, kind1], axis=1)


def _gdn_kernel(qkv_ref, z_ref, gate_ref, buf_ref, cw_ref, alog_ref, dtb_ref, ng_ref, s0_ref,
                o_ref, sout_ref, xbuf_ref, s_ref, *, c, t_valid, gate_off):
    ci = pl.program_id(1)
    nh, dk = A_HEADS, A_DK
    hw = nh * dk

    @pl.when(ci == 0)
    def _():
        s_ref[...] = s0_ref[0]
        xbuf_ref[0:8, :] = jnp.zeros((8, 3 * hw), F32)
        xbuf_ref[8 - (A_CONV - 1):8, :] = buf_ref[0]

    xbuf_ref[8:8 + c, :] = qkv_ref[0]
    y = xbuf_ref[5:5 + c, :] * cw_ref[0:1, :]
    for j in range(1, A_CONV):
        y = y + xbuf_ref[5 + j:5 + j + c, :] * cw_ref[j:j + 1, :]
    tail = xbuf_ref[8 + c - 3:8 + c, :]
    xbuf_ref[5:8, :] = tail
    y = _silu(y)

    gt = gate_ref[0]
    ba = gt[:, gate_off:gate_off + nh]
    aa = gt[:, gate_off + nh:gate_off + 2 * nh]
    beta = _sigmoid(ba)
    xs = aa + dtb_ref[...]
    softplus = jnp.maximum(xs, 0.0) + jnp.log1p(jnp.exp(-jnp.abs(xs)))
    g = -jnp.exp(alog_ref[...]) * softplus
    row = lax.broadcasted_iota(I32, (c, c), 0)
    col = lax.broadcasted_iota(I32, (c, c), 1)
    if t_valid < c:
        valid = lax.broadcasted_iota(I32, (c, nh), 0) < t_valid
        beta = jnp.where(valid, beta, 0.0)
        g = jnp.where(valid, g, 0.0)
    incl = (row >= col)[None]
    strict = (row > col)[None]
    tri = jnp.where(row >= col, 1.0, 0.0).astype(BF16)
    gc = _dot_exact_left(tri, g)
    eye_h = jnp.where(lax.broadcasted_iota(I32, (nh, nh), 0) == lax.broadcasted_iota(I32, (nh, nh), 1),
                      1.0, 0.0).astype(BF16)
    gct = _dot_exact_left(eye_h, gc, NT)
    eye_c = jnp.where(row == col, 1.0, 0.0)[None]

    def heads(a, off):
        return jnp.stack([a[:, off + h * dk:off + (h + 1) * dk] for h in range(nh)], axis=0)

    def cols(a):
        return jnp.stack([a[:, h:h + 1] for h in range(nh)], axis=0)

    q3 = heads(y, 0)
    k3 = heads(y, hw)
    v3 = heads(y, 2 * hw)
    q3 = q3 * lax.rsqrt(jnp.sum(q3 * q3, axis=-1, keepdims=True) + 1e-6)
    k3 = k3 * lax.rsqrt(jnp.sum(k3 * k3, axis=-1, keepdims=True) + 1e-6)
    b3 = cols(beta)
    gch = cols(gc)
    gct3 = jnp.stack([gct[h:h + 1, :] for h in range(nh)], axis=0)
    decay = jnp.where(incl, jnp.exp(jnp.minimum(gch - gct3, 0.0)), 0.0)
    kb = k3 * b3
    vb = v3 * b3
    a_mat = jnp.where(strict, _dot3(kb, k3, BNT) * decay, 0.0)
    def same_block(s):
        sh = int(round(math.log2(s)))
        return (jnp.right_shift(row, sh) == jnp.right_shift(col, sh))[None]

    s_blk = min(8, c)
    pw = jnp.where(same_block(s_blk), -a_mat, 0.0)
    t_inv = eye_c + pw
    for _ in range(int(round(math.log2(s_blk))) - 1):
        pw = _dot3(pw, pw, BNN)
        t_inv = t_inv + _dot3(t_inv, pw, BNN)
    while s_blk < c:
        off = jnp.where(jnp.logical_and(same_block(2 * s_blk), jnp.logical_not(same_block(s_blk))), a_mat, 0.0)
        t_inv = t_inv - _dot3(t_inv, _dot3(off, t_inv, BNN), BNN)
        s_blk *= 2
    egc = jnp.exp(gch)
    u = _dot3(t_inv, vb, BNN)
    w = _dot3(t_inv, kb * egc, BNN)
    qs = q3 * dk ** -0.5
    qk = _dot3(qs, k3, BNT) * decay
    g_last = gch[:, c - 1:c, :]
    k_dec = k3 * jnp.exp(jnp.minimum(g_last - gch, 0.0))
    s_old = s_ref[...]
    v_new = u - _dot3(w, s_old, BNN)
    o = _dot3(qs * egc, s_old, BNN) + _dot3(qk, v_new, BNN)
    eg_last = jnp.exp(g_last)
    on = o * lax.rsqrt(jnp.mean(o * o, axis=-1, keepdims=True) + EPS) * ng_ref[...]
    for h in range(nh):
        s_ref[h] = s_old[h] * eg_last[h] + _dot3(k_dec[h], v_new[h], TN)
        o_ref[0, :, h * dk:(h + 1) * dk] = on[h] * _silu(z_ref[0, :, h * dk:(h + 1) * dk])

    @pl.when(ci == pl.num_programs(1) - 1)
    def _():
        sout_ref[0] = s_ref[...]


def _gdn(proj, small, conv_buf, conv_w, a_log, dt_bias, norm_g, s0, *, c, t_valid, gate_group, gate_off):
    b, t = proj.shape[:2]
    nh, dk = A_HEADS, A_DK
    hw = nh * dk
    return pl.pallas_call(
        functools.partial(_gdn_kernel, c=c, t_valid=t_valid, gate_off=gate_off),
        grid=(b, t // c),
        in_specs=[pl.BlockSpec((1, c, 3 * hw), lambda bi, ci: (bi, ci, 0)),
                  pl.BlockSpec((1, c, hw), lambda bi, ci: (bi, ci, 3)),
                  pl.BlockSpec((1, c, LANES), lambda bi, ci: (bi, ci, gate_group)),
                  pl.BlockSpec((1, A_CONV - 1, 3 * hw), lambda bi, ci: (bi, 0, 0)),
                  pl.BlockSpec((A_CONV, 3 * hw), lambda bi, ci: (0, 0)),
                  pl.BlockSpec((1, nh), lambda bi, ci: (0, 0)),
                  pl.BlockSpec((1, nh), lambda bi, ci: (0, 0)),
                  pl.BlockSpec((1, dk), lambda bi, ci: (0, 0)),
                  pl.BlockSpec((1, nh, dk, dk), lambda bi, ci: (bi, 0, 0, 0))],
        out_specs=[pl.BlockSpec((1, c, hw), lambda bi, ci: (bi, ci, 0)),
                   pl.BlockSpec((1, nh, dk, dk), lambda bi, ci: (bi, 0, 0, 0))],
        out_shape=[jax.ShapeDtypeStruct((b, t, hw), F32), jax.ShapeDtypeStruct((b, nh, dk, dk), F32)],
        scratch_shapes=[pltpu.VMEM((8 + c, 3 * hw), F32), pltpu.VMEM((nh, dk, dk), F32)],
        compiler_params=_cp(("arbitrary", "arbitrary"), 40),
        name="gdn",
    )(proj, proj, small, conv_buf, conv_w, a_log.reshape(1, nh), dt_bias.reshape(1, nh), norm_g.reshape(1, dk), s0)


def _idx_prep_kernel(x_ref, q_ref, k_ref):
    tm = x_ref.shape[0]
    lo_half = lax.broadcasted_iota(I32, (tm, LANES), 1) < 64

    def split(xg):
        hi = xg.astype(BF16).astype(F32)
        return hi, xg - hi

    for gq in range(IDX_HEADS // 2):
        hi, lo = split(x_ref[:, gq * LANES:(gq + 1) * LANES])
        hi_r = pltpu.roll(hi, 64, 1)
        lo_r = pltpu.roll(lo, 64, 1)
        base = 2 * gq * 256
        q_ref[:, base:base + 128] = jnp.where(lo_half, hi, hi_r).astype(BF16)
        q_ref[:, base + 128:base + 256] = jnp.where(lo_half, lo, lo_r).astype(BF16)
        q_ref[:, base + 256:base + 384] = jnp.where(lo_half, hi_r, hi).astype(BF16)
        q_ref[:, base + 384:base + 512] = jnp.where(lo_half, lo_r, lo).astype(BF16)
    hi, lo = split(x_ref[:, 4 * LANES:5 * LANES])
    kk = jnp.where(lo_half, hi, pltpu.roll(lo, 64, 1)).astype(BF16)
    k_ref[:, 0:128] = kk
    k_ref[:, 128:256] = kk


def _idx_prep(small_r, tm=512):
    t = small_r.shape[0]
    return pl.pallas_call(
        _idx_prep_kernel,
        grid=(t // tm,),
        in_specs=[pl.BlockSpec((tm, 5 * LANES), lambda i: (i, 0))],
        out_specs=[pl.BlockSpec((tm, IDX_HEADS * 256), lambda i: (i, 0)),
                   pl.BlockSpec((tm, 256), lambda i: (i, 0))],
        out_shape=[jax.ShapeDtypeStruct((t, IDX_HEADS * 256), BF16), jax.ShapeDtypeStruct((t, 256), BF16)],
        compiler_params=_cp(("arbitrary",)),
        name="idx_prep",
    )(small_r)


IDX_SCALE = IDX_HEADS ** -0.5 * IDX_HD ** -0.5
W_OFF = 64


def _idx_scores_kernel(q_ref, k_ref, w_ref, o_ref, *, tq, ts):
    i = pl.program_id(0)
    j = pl.program_id(1)
    live = j * ts <= i * tq + tq - 1

    @pl.when(live)
    def _():
        k = k_ref[...]
        wg = w_ref[...] * IDX_SCALE
        acc = jnp.zeros((tq, ts), F32)
        for h in range(IDX_HEADS):
            s = _dot(q_ref[:, h * 256:(h + 1) * 256], k, NT)
            acc = acc + wg[:, W_OFF + h:W_OFF + h + 1] * jnp.maximum(s, 0.0)
        row = i * tq + lax.broadcasted_iota(I32, (tq, ts), 0)
        col = j * ts + lax.broadcasted_iota(I32, (tq, ts), 1)
        o_ref[...] = jnp.where(col <= row, acc, -jnp.inf)

    @pl.when(jnp.logical_not(live))
    def _():
        o_ref[...] = jnp.full((tq, ts), -jnp.inf, F32)


def _idx_scores(qcat, kcat, small_r, *, tq=256, ts=512):
    t = qcat.shape[0]
    return pl.pallas_call(
        functools.partial(_idx_scores_kernel, tq=tq, ts=ts),
        grid=(t // tq, t // ts),
        in_specs=[pl.BlockSpec((tq, IDX_HEADS * 256), lambda i, j: (i, 0)),
                  pl.BlockSpec((ts, 256), lambda i, j: (j, 0)),
                  pl.BlockSpec((tq, LANES), lambda i, j: (i, 4))],
        out_specs=pl.BlockSpec((tq, ts), lambda i, j: (i, j)),
        out_shape=jax.ShapeDtypeStruct((t, t), F32),
        compiler_params=_cp(("arbitrary", "arbitrary")),
        name="idx_scores",
    )(qcat, kcat, small_r)


def _thr_kernel(s_ref, thr_ref, need_ref, tie_ref, key_ref, *, k, cw, causal):
    tr, s_cols = s_ref.shape
    nchunks = ((pl.program_id(0) + 1) * tr + cw - 1) // cw if causal else s_cols // cw
    kf = float(k)

    def fill(ci, carry):
        off = pl.multiple_of(ci * cw, cw)
        bits = pltpu.bitcast(s_ref[:, pl.ds(off, cw)] + 0.0, I32)
        key_ref[:, pl.ds(off, cw)] = jnp.where(bits < 0, bits ^ 0x7FFFFFFF, bits)
        return carry

    lax.fori_loop(0, nchunks, fill, 0)

    def count(cand, strict):
        def cbody(ci, acc):
            off = pl.multiple_of(ci * cw, cw)
            blk = key_ref[:, pl.ds(off, cw)]
            for t in range(cw // LANES):
                kk = blk[:, t * LANES:(t + 1) * LANES]
                acc = acc + jnp.where(kk > cand if strict else kk >= cand, 1.0, 0.0)
            return acc

        acc = lax.fori_loop(0, nchunks, cbody, jnp.zeros((tr, LANES), F32))
        return jnp.sum(acc, axis=-1, keepdims=True)

    p0 = jnp.where(count(0, False) >= kf, 0, INT_MIN).astype(I32)

    def body(b, p):
        cand = p | jnp.left_shift(jnp.int32(1), 30 - b)
        return jnp.where(count(cand, False) >= kf, cand, p)

    p = lax.fori_loop(0, 31, body, p0)
    thr_ref[...] = pltpu.bitcast(jnp.where(p < 0, p ^ 0x7FFFFFFF, p), F32)
    need_ref[...] = kf - count(p, True)
    tie_ref[...] = jnp.where(jnp.logical_and(count(p, False) > kf, p != KEY_NEG_INF), 1, 0).astype(I32)


def _topk_thr(scores, *, k, cw, causal, tr):
    r, s = scores.shape
    tr = min(tr, r)
    assert cw >= k and s % cw == 0
    return pl.pallas_call(
        functools.partial(_thr_kernel, k=k, cw=cw, causal=causal),
        grid=(r // tr,),
        in_specs=[pl.BlockSpec((tr, s), lambda i: (i, 0))],
        out_specs=[pl.BlockSpec((tr, 1), lambda i: (i, 0))] * 3,
        out_shape=[jax.ShapeDtypeStruct((r, 1), F32), jax.ShapeDtypeStruct((r, 1), F32),
                   jax.ShapeDtypeStruct((r, 1), I32)],
        scratch_shapes=[pltpu.VMEM((tr, s), I32)],
        compiler_params=_cp(("arbitrary",), 32),
        name="topk_thr",
    )(scores)


def _select_mask(sc, thr, need, causal, eq_before, tie):
    if not tie:
        return jnp.logical_and(sc >= thr, causal), None
    eq = sc == thr
    n = sc.shape[1]
    upper = jnp.where(lax.broadcasted_iota(I32, (n, n), 0) < lax.broadcasted_iota(I32, (n, n), 1), 1.0, 0.0)
    eqf = jnp.where(eq, 1.0, 0.0)
    rank = eq_before + _dot(eqf.astype(BF16), upper.astype(BF16))
    sel = jnp.logical_or(sc > thr, jnp.logical_and(eq, rank < need))
    return jnp.logical_and(sel, causal), eq_before + jnp.sum(eqf, axis=-1, keepdims=True)


def _flash_update(s, m_ref, acc_ref, r0, rows, vblk):
    ts = s.shape[1]
    m_old = m_ref[r0:r0 + rows, :]
    m_new = jnp.maximum(m_old, jnp.max(s, axis=-1, keepdims=True))
    p = jnp.exp(s - jnp.tile(m_new, (1, ts // LANES)))
    alpha = jnp.exp(m_old - m_new)
    acc_ref[r0:r0 + rows, :] = (jnp.tile(alpha, (1, acc_ref.shape[1] // LANES)) * acc_ref[r0:r0 + rows, :]
                                + _dot(p.astype(BF16), vblk))
    m_ref[r0:r0 + rows, :] = m_new


def _dsa_attn_kernel(tie_ref, q_ref, k_ref, v_ref, sc_ref, thr_ref, need_ref, o_ref,
                     kb_ref, vb_ref, qs_ref, m_ref, acc_ref, eq_ref, *, tq, ts):
    i = pl.program_id(1)
    hg = B_HEADS // B_KV
    d = B_HD

    @pl.when(i == 0)
    def _():
        kb_ref[...] = k_ref[...].astype(BF16)
        vb_ref[:, 0:d] = v_ref[...].astype(BF16)
        vb_ref[:, d:2 * d] = jnp.ones((vb_ref.shape[0], d), BF16)

    for h in range(hg):
        qs_ref[h * tq:(h + 1) * tq, :] = (q_ref[:, h * d:(h + 1) * d] * d ** -0.5).astype(BF16)
    m_ref[...] = jnp.full(m_ref.shape, NEG_BIG, F32)
    acc_ref[...] = jnp.zeros(acc_ref.shape, F32)
    eq_ref[...] = jnp.zeros(eq_ref.shape, F32)
    thr = thr_ref[...]
    need = need_ref[...]
    nblk = (i * tq + tq + ts - 1) // ts

    def run(tie):
        def body(jb, carry):
            off = pl.multiple_of(jb * ts, ts)
            sc = sc_ref[:, pl.ds(off, ts)]
            row = i * tq + lax.broadcasted_iota(I32, (tq, ts), 0)
            col = off + lax.broadcasted_iota(I32, (tq, ts), 1)
            mask, eq_new = _select_mask(sc, thr, need, col <= row, eq_ref[...], tie)
            if tie:
                eq_ref[...] = eq_new
            kblk = kb_ref[pl.ds(off, ts), :]
            vblk = vb_ref[pl.ds(off, ts), :]
            for h in range(hg):
                s = jnp.where(mask, _dot(qs_ref[h * tq:(h + 1) * tq, :], kblk, NT), NEG_BIG)
                _flash_update(s, m_ref, acc_ref, h * tq, tq, vblk)
            return carry

        lax.fori_loop(0, nblk, body, 0)

    has_tie = tie_ref[i] > 0

    @pl.when(has_tie)
    def _():
        run(True)

    @pl.when(jnp.logical_not(has_tie))
    def _():
        run(False)

    for h in range(hg):
        o_ref[:, h * d:(h + 1) * d] = acc_ref[h * tq:(h + 1) * tq, 0:d] / acc_ref[h * tq:(h + 1) * tq, d:2 * d]


def _dsa_attn(tie_blk, q_r, k_r, v_src, v_col0, scores, thr, need, *, tq, ts=512):
    t = q_r.shape[0]
    hg = B_HEADS // B_KV
    once = dict(pipeline_mode=pl.Buffered(1))
    grid_spec = pltpu.PrefetchScalarGridSpec(
        num_scalar_prefetch=1,
        grid=(B_KV, t // tq),
        in_specs=[pl.BlockSpec((tq, hg * B_HD), lambda g, i, tie: (i, g)),
                  pl.BlockSpec((t, B_HD), lambda g, i, tie: (0, g), **once),
                  pl.BlockSpec((t, B_HD), lambda g, i, tie: (0, v_col0 + g), **once),
                  pl.BlockSpec((tq, t), lambda g, i, tie: (i, 0)),
                  pl.BlockSpec((tq, 1), lambda g, i, tie: (i, 0)),
                  pl.BlockSpec((tq, 1), lambda g, i, tie: (i, 0))],
        out_specs=pl.BlockSpec((tq, hg * B_HD), lambda g, i, tie: (i, g)),
        scratch_shapes=[pltpu.VMEM((t, B_HD), BF16), pltpu.VMEM((t, 2 * B_HD), BF16),
                        pltpu.VMEM((hg * tq, B_HD), BF16), pltpu.VMEM((hg * tq, LANES), F32),
                        pltpu.VMEM((hg * tq, 2 * B_HD), F32), pltpu.VMEM((tq, 1), F32)])
    return pl.pallas_call(
        functools.partial(_dsa_attn_kernel, tq=tq, ts=ts),
        grid_spec=grid_spec,
        out_shape=jax.ShapeDtypeStruct((t, B_HEADS * B_HD), F32),
        compiler_params=_cp(("arbitrary", "arbitrary"), 56),
        name="dsa_attn",
    )(tie_blk, q_r, k_r, v_src, scores, thr, need)


def _lambda_value(lam_ref, lam_init):
    lp = lam_ref[...]
    a = jnp.sum(lp[0:1, :] * lp[1:2, :], axis=-1, keepdims=True)
    b = jnp.sum(lp[2:3, :] * lp[3:4, :], axis=-1, keepdims=True)
    return jnp.exp(a) - jnp.exp(b) + lam_init


def _diff_attn_kernel(q_ref, k_ref, v_ref, lam_ref, ng_ref, o_ref, kb_ref, vb_ref, qs_ref, m_ref, acc_ref,
                      *, tq, ts, rsub, lam_init):
    i = pl.program_id(1)
    d2 = 2 * C_HD
    rows = 2 * tq

    @pl.when(i == 0)
    def _():
        kb_ref[...] = k_ref[...].astype(BF16)
        vb_ref[:, 0:d2] = v_ref[...].astype(BF16)
        vb_ref[:, d2:2 * d2] = jnp.ones((vb_ref.shape[0], d2), BF16)

    q = q_ref[...] * C_HD ** -0.5
    first = lax.broadcasted_iota(I32, (tq, d2), 1) < C_HD
    qs_ref[0:tq, :] = jnp.where(first, q, 0.0).astype(BF16)
    qs_ref[tq:rows, :] = jnp.where(first, 0.0, q).astype(BF16)
    m_ref[...] = jnp.full(m_ref.shape, NEG_BIG, F32)
    acc_ref[...] = jnp.zeros(acc_ref.shape, F32)

    def step(jb, diag_col0):
        off = pl.multiple_of(jb * ts, ts)
        kblk = kb_ref[pl.ds(off, ts), :]
        vblk = vb_ref[pl.ds(off, ts), :]
        for r0 in range(0, rows, rsub):
            s = _dot(qs_ref[r0:r0 + rsub, :], kblk, NT)
            if diag_col0 is not None:
                qrow = (r0 % tq) + lax.broadcasted_iota(I32, (rsub, ts), 0)
                s = jnp.where(diag_col0 + lax.broadcasted_iota(I32, (rsub, ts), 1) <= qrow, s, NEG_BIG)
            _flash_update(s, m_ref, acc_ref, r0, rsub, vblk)

    def body(jb, carry):
        step(jb, None)
        return carry

    nfull = (i * tq) // ts
    lax.fori_loop(0, nfull, body, 0)
    for dj in range(tq // ts):
        step(nfull + dj, dj * ts)

    lam = _lambda_value(lam_ref, lam_init)
    o = (acc_ref[0:tq, 0:d2] / acc_ref[0:tq, d2:2 * d2]
         - lam * (acc_ref[tq:rows, 0:d2] / acc_ref[tq:rows, d2:2 * d2]))
    on = o * lax.rsqrt(jnp.mean(o * o, axis=-1, keepdims=True) + EPS) * ng_ref[...]
    o_ref[...] = on * (1.0 - lam_init)


def _diff_attn(qk_r, v_src, v_col0, c_lambda, c_norm_g, *, lam_init, tq=1024, ts=512, rsub=256):
    t = qk_r.shape[0]
    tq = min(tq, t)
    d2 = 2 * C_HD
    once = dict(pipeline_mode=pl.Buffered(1))
    return pl.pallas_call(
        functools.partial(_diff_attn_kernel, tq=tq, ts=ts, rsub=rsub, lam_init=lam_init),
        grid=(C_HEADS, t // tq),
        in_specs=[pl.BlockSpec((tq, d2), lambda h, i: (i, h)),
                  pl.BlockSpec((t, d2), lambda h, i: (0, C_HEADS + h), **once),
                  pl.BlockSpec((t, d2), lambda h, i: (0, v_col0 + h), **once),
                  pl.BlockSpec((4, C_HD), lambda h, i: (0, 0)),
                  pl.BlockSpec((1, d2), lambda h, i: (0, 0))],
        out_specs=pl.BlockSpec((tq, d2), lambda h, i: (i, h)),
        out_shape=jax.ShapeDtypeStruct((t, C_HEADS * d2), F32),
        scratch_shapes=[pltpu.VMEM((t, d2), BF16), pltpu.VMEM((t, 2 * d2), BF16), pltpu.VMEM((2 * tq, d2), BF16),
                        pltpu.VMEM((2 * tq, LANES), F32), pltpu.VMEM((2 * tq, 2 * d2), F32)],
        compiler_params=_cp(("arbitrary", "arbitrary"), 48),
        name="diff_attn",
    )(qk_r, qk_r, v_src, c_lambda, c_norm_g.reshape(1, d2))


TS_PAD = 8


def _page_specs(n_pg, blk, layer, n_prefetch):
    def spec(k):
        if n_prefetch == 1:
            return pl.BlockSpec((1, 1) + blk, lambda bi, p, pt: (layer, pt[bi, p * n_pg + k], 0, 0))
        return pl.BlockSpec((1, 1) + blk, lambda bi, p, pt, tie: (layer, pt[bi, p * n_pg + k], 0, 0))

    return [spec(k) for k in range(n_pg)]


def _idx_scores_s_kernel(pt_ref, q_ref, w_ref, *refs, n_pg, t_new):
    pools = refs[:n_pg]
    new_ref, o_ref, onew_ref = refs[n_pg:]
    q = q_ref[0]
    w = w_ref[0] * IDX_SCALE

    def weigh(s):
        acc = jnp.zeros((TS_PAD, s.shape[1]), F32)
        for h in range(IDX_HEADS):
            acc = acc + w[h * TS_PAD:(h + 1) * TS_PAD, :] * jnp.maximum(s[h * TS_PAD:(h + 1) * TS_PAD, :], 0.0)
        return acc

    for k in range(n_pg):
        o_ref[0, :, k * PAGE:(k + 1) * PAGE] = weigh(_dot3(q, pools[k][0, 0]))

    @pl.when(pl.program_id(1) == pl.num_programs(1) - 1)
    def _():
        tq = lax.broadcasted_iota(I32, (TS_PAD, PAGE), 0)
        tk = lax.broadcasted_iota(I32, (TS_PAD, PAGE), 1)
        ok = jnp.logical_and(tk <= tq, tk < t_new)
        onew_ref[0] = jnp.where(ok, weigh(_dot3(q, new_ref[0], NT)), -jnp.inf)


def _idx_scores_s(page_table, q_s, w_s, pool_ki_t, ki_new, *, layer, t_new, n_pg=8):
    b, n_pages = page_table.shape
    grid_spec = pltpu.PrefetchScalarGridSpec(
        num_scalar_prefetch=1,
        grid=(b, n_pages // n_pg),
        in_specs=[pl.BlockSpec((1, IDX_HEADS * TS_PAD, IDX_HD), lambda bi, p, pt: (bi, 0, 0)),
                  pl.BlockSpec((1, IDX_HEADS * TS_PAD, 1), lambda bi, p, pt: (bi, 0, 0))]
        + _page_specs(n_pg, (IDX_HD, PAGE), layer, 1)
        + [pl.BlockSpec((1, PAGE, IDX_HD), lambda bi, p, pt: (bi, 0, 0))],
        out_specs=[pl.BlockSpec((1, TS_PAD, n_pg * PAGE), lambda bi, p, pt: (bi, 0, p)),
                   pl.BlockSpec((1, TS_PAD, PAGE), lambda bi, p, pt: (bi, 0, 0))])
    return pl.pallas_call(
        functools.partial(_idx_scores_s_kernel, n_pg=n_pg, t_new=t_new),
        grid_spec=grid_spec,
        out_shape=[jax.ShapeDtypeStruct((b, TS_PAD, n_pages * PAGE), F32),
                   jax.ShapeDtypeStruct((b, TS_PAD, PAGE), F32)],
        compiler_params=_cp(("arbitrary", "arbitrary")),
        name="idx_scores_sample",
    )(page_table, q_s, w_s, *([pool_ki_t] * n_pg), ki_new)


def _dsa_attn_s_kernel(pt_ref, tie_ref, q_ref, *refs, n_pg):
    pk = refs[:n_pg]
    pv = refs[n_pg:2 * n_pg]
    nk_ref, nv_ref, sc_ref, scn_ref, thr_ref, need_ref, o_ref, m_ref, l_ref, acc_ref, eq_ref = refs[2 * n_pg:]
    bi = pl.program_id(0)
    p = pl.program_id(1)
    hg = B_HEADS // B_KV

    @pl.when(p == 0)
    def _():
        m_ref[...] = jnp.full(m_ref.shape, NEG_BIG, F32)
        l_ref[...] = jnp.zeros(l_ref.shape, F32)
        acc_ref[...] = jnp.zeros(acc_ref.shape, F32)
        eq_ref[...] = jnp.zeros(eq_ref.shape, F32)

    thr = thr_ref[0]
    need = need_ref[0]
    scale = B_HD ** -0.5

    def attend(sc, keys_of, vals_of, tie):
        mask8, eq_new = _select_mask(sc, thr, need, sc > -jnp.inf, eq_ref[...], tie)
        if tie:
            eq_ref[...] = eq_new
        mask = jnp.concatenate([mask8] * hg, axis=0)
        for g in range(B_KV):
            s = jnp.where(mask, _dot3(q_ref[0, g] * scale, keys_of(g), NT), NEG_BIG)
            m_old = m_ref[g]
            m_new = jnp.maximum(m_old, jnp.max(s, axis=-1, keepdims=True))
            pr = jnp.where(mask, jnp.exp(s - m_new), 0.0)
            alpha = jnp.exp(m_old - m_new)
            l_ref[g] = alpha * l_ref[g] + jnp.sum(pr, axis=-1, keepdims=True)
            acc_ref[g] = alpha * acc_ref[g] + _dot3(pr, vals_of(g))
            m_ref[g] = m_new

    def paged(refs_):
        return lambda g: jnp.concatenate([r[0, 0, pl.ds(g, PAGE, stride=B_KV), :] for r in refs_], axis=0)

    def run(tie):
        attend(sc_ref[0], paged(pk), paged(pv), tie)

        @pl.when(p == pl.num_programs(1) - 1)
        def _():
            attend(scn_ref[0], lambda g: nk_ref[0, :, g * B_HD:(g + 1) * B_HD],
                   lambda g: nv_ref[0, :, g * B_HD:(g + 1) * B_HD], tie)

    has_tie = tie_ref[bi] > 0

    @pl.when(has_tie)
    def _():
        run(True)

    @pl.when(jnp.logical_not(has_tie))
    def _():
        run(False)

    @pl.when(p == pl.num_programs(1) - 1)
    def _():
        for g in range(B_KV):
            o_ref[0, g] = acc_ref[g] / l_ref[g]


def _dsa_attn_s(page_table, tie_b, q_s, pool_k2, pool_v2, k_new, v_new, scores, scores_new, thr, need,
                *, layer, n_pg=4):
    b, n_pages = page_table.shape
    hg = B_HEADS // B_KV
    rows = hg * TS_PAD
    kvw = B_KV * B_HD
    grid_spec = pltpu.PrefetchScalarGridSpec(
        num_scalar_prefetch=2,
        grid=(b, n_pages // n_pg),
        in_specs=[pl.BlockSpec((1, B_KV, rows, B_HD), lambda bi, p, pt, tie: (bi, 0, 0, 0))]
        + _page_specs(n_pg, (PAGE * B_KV, B_HD), layer, 2) + _page_specs(n_pg, (PAGE * B_KV, B_HD), layer, 2)
        + [pl.BlockSpec((1, PAGE, kvw), lambda bi, p, pt, tie: (bi, 0, 0)),
           pl.BlockSpec((1, PAGE, kvw), lambda bi, p, pt, tie: (bi, 0, 0)),
           pl.BlockSpec((1, TS_PAD, n_pg * PAGE), lambda bi, p, pt, tie: (bi, 0, p)),
           pl.BlockSpec((1, TS_PAD, PAGE), lambda bi, p, pt, tie: (bi, 0, 0)),
           pl.BlockSpec((1, TS_PAD, 1), lambda bi, p, pt, tie: (bi, 0, 0)),
           pl.BlockSpec((1, TS_PAD, 1), lambda bi, p, pt, tie: (bi, 0, 0))],
        out_specs=pl.BlockSpec((1, B_KV, rows, B_HD), lambda bi, p, pt, tie: (bi, 0, 0, 0)),
        scratch_shapes=[pltpu.VMEM((B_KV, rows, 1), F32), pltpu.VMEM((B_KV, rows, 1), F32),
                        pltpu.VMEM((B_KV, rows, B_HD), F32), pltpu.VMEM((TS_PAD, 1), F32)])
    return pl.pallas_call(
        functools.partial(_dsa_attn_s_kernel, n_pg=n_pg),
        grid_spec=grid_spec,
        out_shape=jax.ShapeDtypeStruct((b, B_KV, rows, B_HD), F32),
        compiler_params=_cp(("arbitrary", "arbitrary")),
        name="dsa_attn_sample",
    )(page_table, tie_b, q_s, *([pool_k2] * n_pg), *([pool_v2] * n_pg), k_new, v_new, scores, scores_new, thr, need)


def _diff_attn_s_kernel(pt_ref, wt_ref, *refs, n_pg, t_new, lam_init):
    pk = refs[:n_pg]
    pv = refs[n_pg:2 * n_pg]
    nk_ref, nv_ref, lam_ref, ng_ref, o_ref, m_ref, l_ref, acc_ref = refs[2 * n_pg:]
    p = pl.program_id(1)
    nrow = wt_ref.shape[1]
    rph = 2 * t_new
    dv = 2 * C_HD

    @pl.when(p == 0)
    def _():
        m_ref[...] = jnp.full(m_ref.shape, NEG_BIG, F32)
        l_ref[...] = jnp.zeros(l_ref.shape, F32)
        acc_ref[...] = jnp.zeros(acc_ref.shape, F32)

    wt = wt_ref[0]

    def attend(s, vals_of):
        m_old = m_ref[...]
        m_new = jnp.maximum(m_old, jnp.max(s, axis=-1, keepdims=True))
        pr = jnp.exp(s - m_new)
        alpha = jnp.exp(m_old - m_new)
        l_ref[...] = alpha * l_ref[...] + jnp.sum(pr, axis=-1, keepdims=True)
        prb = pr.astype(BF16)
        for h in range(C_HEADS):
            r0 = h * rph
            acc_ref[r0:r0 + rph, :] = alpha[r0:r0 + rph, :] * acc_ref[r0:r0 + rph, :] + _dot(prb[r0:r0 + rph, :],
                                                                                           vals_of(h))
        m_ref[...] = m_new

    s_past = jnp.concatenate([_dot(wt, r[0, 0].astype(BF16)) for r in pk], axis=1)
    attend(s_past, lambda h: jnp.concatenate([r[0, 0, pl.ds(h, PAGE, stride=C_HEADS), :] for r in pv],
                                             axis=0).astype(BF16))

    @pl.when(p == pl.num_programs(1) - 1)
    def _():
        tq = lax.broadcasted_iota(I32, (nrow, TS_PAD), 0) & (t_new - 1)
        tk = lax.broadcasted_iota(I32, (nrow, TS_PAD), 1)
        s_new = jnp.where(tk <= tq, _dot(wt, nk_ref[0].astype(BF16), NT), NEG_BIG)
        attend(s_new, lambda h: nv_ref[0, :, h * dv:(h + 1) * dv].astype(BF16))
        lam = _lambda_value(lam_ref, lam_init)
        a = acc_ref[...] / l_ref[...]
        for h in range(C_HEADS):
            r0 = h * rph
            o = a[r0:r0 + t_new, :] - lam * a[r0 + t_new:r0 + rph, :]
            on = o * lax.rsqrt(jnp.mean(o * o, axis=-1, keepdims=True) + EPS) * ng_ref[...]
            o_ref[0, :, h * dv:(h + 1) * dv] = on * (1.0 - lam_init)


def _diff_attn_s(page_table, wt, pool_kt, pool_v2, k_new, v_new, c_lambda, c_norm_g, *, layer, t_new, lam_init,
                 n_pg=4):
    b, n_pages = page_table.shape
    nrow, width = wt.shape[1:]
    dv = 2 * C_HD
    grid_spec = pltpu.PrefetchScalarGridSpec(
        num_scalar_prefetch=1,
        grid=(b, n_pages // n_pg),
        in_specs=[pl.BlockSpec((1, nrow, width), lambda bi, p, pt: (bi, 0, 0))]
        + _page_specs(n_pg, (width, PAGE), layer, 1) + _page_specs(n_pg, (PAGE * C_HEADS, dv), layer, 1)
        + [pl.BlockSpec((1, TS_PAD, width), lambda bi, p, pt: (bi, 0, 0)),
           pl.BlockSpec((1, TS_PAD, width), lambda bi, p, pt: (bi, 0, 0)),
           pl.BlockSpec((4, C_HD), lambda bi, p, pt: (0, 0)),
           pl.BlockSpec((1, dv), lambda bi, p, pt: (0, 0))],
        out_specs=pl.BlockSpec((1, t_new, width), lambda bi, p, pt: (bi, 0, 0)),
        scratch_shapes=[pltpu.VMEM((nrow, 1), F32), pltpu.VMEM((nrow, 1), F32), pltpu.VMEM((nrow, dv), F32)])
    return pl.pallas_call(
        functools.partial(_diff_attn_s_kernel, n_pg=n_pg, t_new=t_new, lam_init=lam_init),
        grid_spec=grid_spec,
        out_shape=jax.ShapeDtypeStruct((b, t_new, width), F32),
        compiler_params=_cp(("arbitrary", "arbitrary"), 56),
        name="diff_attn_sample",
    )(page_table, wt, *([pool_kt] * n_pg), *([pool_v2] * n_pg), k_new, v_new, c_lambda,
      c_norm_g.reshape(1, dv))


def _split_ab_weights(w_in):
    n_a = 4 * A_HEADS * A_DK
    n_gate = 2 * A_HEADS
    n_b = (B_HEADS + 2 * B_KV) * B_HD
    n_idx = IDX_HEADS * IDX_HD + IDX_HD + IDX_HEADS
    big = jnp.concatenate([w_in[:, :n_a], w_in[:, n_a + n_gate:n_a + n_gate + n_b]], axis=1)
    pad = 5 * LANES - n_idx - n_gate
    small = jnp.concatenate([w_in[:, n_a + n_gate + n_b:], w_in[:, n_a:n_a + n_gate],
                             jnp.zeros((w_in.shape[0], pad), w_in.dtype)], axis=1)
    return big, small


PROMPT_TM = 2048
GATE_OFF = IDX_HD + IDX_HEADS
QB_COL0 = 4 * A_HEADS * A_DK // LANES
KB_COL0 = QB_COL0 + B_HEADS
VB_COL0 = KB_COL0 + B_KV


def _mlp(x, mods, norm_g, w1, w2, *, tm, hi):
    sh2, sc2, g2 = mods
    hid = _norm_matmul(x, norm_g[2:3], sc2, sh2, w1, tm=tm, tn=512, hi=hi, relu2=True,
                       out_dtype=F32 if hi else BF16, vmem_mb=56)
    if hi:
        return _matmul_norm_res(hid, w2, norm_g[3:4], g2, x, tm=tm, tk=512, hi=True)
    return _matmul_norm_res(hid, w2, norm_g[3:4], g2, x, tm=512, tk=2048, vmem_mb=56)


def _prompt_trunk(x, mods, P):
    t = x.shape[0]
    pos = jnp.arange(t, dtype=I32)
    tab128, tab64 = _rope_tables(pos)
    sh1, sc1, g1, sh2, sc2, g2 = mods[0]
    ng = P["norm_g"][0]
    big = _norm_matmul(x, ng[0:1], sc1, sh1, P["w_big"], tm=PROMPT_TM, tn=256, vmem_mb=56)
    small = _norm_matmul(x, ng[0:1], sc1, sh1, P["w_small"], tm=PROMPT_TM // 2, tn=5 * LANES, hi=True)
    small_r = _rope(small, tab64, col0=0, ngroups=5, n_kind0=4, r1=32, r2=96, tm=1024)
    qb_r = _rope(big, tab128, col0=QB_COL0, ngroups=B_HEADS, n_kind0=B_HEADS, r1=64, r2=None, tm=1024, gw=4)
    kb_r = _rope(big, tab128, col0=KB_COL0, ngroups=B_KV, n_kind0=B_KV, r1=64, r2=None, tm=1024)
    qcat, kcat = _idx_prep(small_r)
    scores = _idx_scores(qcat, kcat, small_r)
    thr, need, tie = _topk_thr(scores, k=min(TOPK, t // 4), cw=1024, causal=True, tr=128)
    tq = 256
    tie_blk = jnp.max(tie.reshape(t // tq, tq), axis=1)
    ob = _dsa_attn(tie_blk, qb_r, kb_r, big, VB_COL0, scores, thr, need, tq=tq)
    oa, s_new = _gdn(big[None], small_r[None], jnp.zeros((1, A_CONV - 1, 3 * A_HEADS * A_DK), F32), P["ab_conv_w"],
                     P["ab_A_log"], P["ab_dt_bias"], P["ab_norm_g"], jnp.zeros((1, A_HEADS, A_DK, A_DK), F32),
                     c=128, t_valid=128, gate_group=4, gate_off=GATE_OFF)
    mix = jnp.concatenate([oa[0], ob], axis=1)
    x = _matmul_norm_res(mix, P["ab_w_out"], ng[1:2], g1, x, tm=512, tk=2048, vmem_mb=56)
    x = _mlp(x, (sh2, sc2, g2), ng, (P["mlp_w1"], 0), (P["mlp_w2"], 0), tm=PROMPT_TM, hi=False)
    outs0 = (kb_r, big[:, VB_COL0 * LANES:], small_r[:, 4 * LANES:4 * LANES + IDX_HD], s_new,
             big[t - (A_CONV - 1):, :3 * A_HEADS * A_DK])
    sh1, sc1, g1, sh2, sc2, g2 = mods[1]
    ng = P["norm_g"][1]
    lam_init = 0.8 - 0.6 * math.exp(-0.3 * 1)
    proj = _norm_matmul(x, ng[0:1], sc1, sh1, P["c_w_in"], tm=PROMPT_TM, tn=256, vmem_mb=56)
    qk_r = _rope(proj, tab64, col0=0, ngroups=2 * C_HEADS, n_kind0=2 * C_HEADS, r1=32, r2=96, tm=1024, gw=4)
    o = _diff_attn(qk_r, proj, 2 * C_HEADS, P["c_lambda"], P["c_norm_g"], lam_init=lam_init)
    x = _matmul_norm_res(o, P["c_w_out"], ng[1:2], g1, x, tm=512, tk=2048, vmem_mb=56)
    x = _mlp(x, (sh2, sc2, g2), ng, (P["mlp_w1"], 1), (P["mlp_w2"], 1), tm=PROMPT_TM, hi=False)
    outs1 = (qk_r[:, 2 * C_HEADS * C_HD:], proj[:, 4 * C_HEADS * C_HD:])
    return x, outs0, outs1


def _sample_trunk(x, mods, P, cache, past_len):
    b, ts_, d = x.shape
    m = b * ts_
    x = x.reshape(m, d)
    pos = past_len + jnp.arange(ts_, dtype=I32)
    tab128, tab64 = _rope_tables(pos)
    tab128 = jnp.tile(tab128, (b, 1))
    tab64 = jnp.tile(tab64, (b, 1))
    page_table = cache["page_table"]
    n_pages = page_table.shape[1]
    hw = A_HEADS * A_DK

    def rows(a):
        return jnp.repeat(a, ts_, axis=0)

    def pad_t(a, n):
        return jnp.pad(a, ((0, 0), (0, n - ts_), (0, 0)))

    sh1, sc1, g1, sh2, sc2, g2 = (rows(a) for a in mods[0])
    ng = P["norm_g"][0]
    big = _norm_matmul(x, ng[0:1], sc1, sh1, P["w_big"], tm=m, tn=512, hi=True)
    small = _norm_matmul(x, ng[0:1], sc1, sh1, P["w_small"], tm=m, tn=5 * LANES, hi=True)
    small_r = _rope(small, tab64, col0=0, ngroups=5, n_kind0=4, r1=32, r2=96, tm=m)
    qb_r = _rope(big, tab128, col0=QB_COL0, ngroups=B_HEADS, n_kind0=B_HEADS, r1=64, r2=None, tm=m)
    kb_r = _rope(big, tab128, col0=KB_COL0, ngroups=B_KV, n_kind0=B_KV, r1=64, r2=None, tm=m)
    vb = big[:, VB_COL0 * LANES:]
    sm3 = small_r.reshape(b, ts_, 5 * LANES)
    qi = sm3[:, :, :IDX_HEADS * IDX_HD].reshape(b, ts_, IDX_HEADS, IDX_HD)
    qi = jnp.pad(jnp.swapaxes(qi, 1, 2), ((0, 0), (0, 0), (0, TS_PAD - ts_), (0, 0)))
    qi = qi.reshape(b, IDX_HEADS * TS_PAD, IDX_HD)
    wi = sm3[:, :, 4 * LANES + W_OFF:4 * LANES + W_OFF + IDX_HEADS]
    wi = jnp.pad(jnp.swapaxes(wi, 1, 2), ((0, 0), (0, 0), (0, TS_PAD - ts_))).reshape(b, IDX_HEADS * TS_PAD, 1)
    ki_r = sm3[:, :, 4 * LANES:4 * LANES + IDX_HD]
    pool_ki_t = jnp.swapaxes(cache["ab_kidx"], 2, 3)
    sc_past, sc_new = _idx_scores_s(page_table, qi, wi, pool_ki_t, pad_t(ki_r, PAGE), layer=0, t_new=ts_,
                                    n_pg=min(8, n_pages))
    scores = jnp.concatenate([sc_past, sc_new], axis=2)
    ncols = scores.shape[2]
    k_sel = min(TOPK, (past_len + ts_) // 4)
    chunk = min(d for d in range(1, ncols // LANES + 1) if (ncols // LANES) % d == 0 and d * LANES >= k_sel) * LANES
    thr, need, tie = _topk_thr(scores.reshape(b * TS_PAD, ncols), k=k_sel, cw=chunk, causal=False, tr=64)
    tie_b = jnp.max(tie.reshape(b, TS_PAD)[:, :ts_], axis=1)
    hg = B_HEADS // B_KV
    q4 = qb_r.reshape(b, ts_, B_KV, hg, B_HD)
    q4 = jnp.pad(jnp.transpose(q4, (0, 2, 3, 1, 4)), ((0, 0), (0, 0), (0, 0), (0, TS_PAD - ts_), (0, 0)))
    q4 = q4.reshape(b, B_KV, hg * TS_PAD, B_HD)
    kvw = B_KV * B_HD
    pool_k2 = cache["ab_k"].reshape(cache["ab_k"].shape[:2] + (PAGE * B_KV, B_HD))
    pool_v2 = cache["ab_v"].reshape(cache["ab_v"].shape[:2] + (PAGE * B_KV, B_HD))
    ob = _dsa_attn_s(page_table, tie_b, q4, pool_k2, pool_v2, pad_t(kb_r.reshape(b, ts_, kvw), PAGE),
                     pad_t(vb.reshape(b, ts_, kvw), PAGE), sc_past, sc_new, thr.reshape(b, TS_PAD, 1),
                     need.reshape(b, TS_PAD, 1), layer=0, n_pg=min(8, n_pages))
    ob = ob.reshape(b, B_KV, hg, TS_PAD, B_HD)[:, :, :, :ts_]
    ob = jnp.transpose(ob, (0, 3, 1, 2, 4)).reshape(m, B_HEADS * B_HD)
    oa, s_new = _gdn(pad_t(big.reshape(b, ts_, -1), TS_PAD), pad_t(sm3, TS_PAD), cache["ab_conv"][0], P["ab_conv_w"],
                     P["ab_A_log"], P["ab_dt_bias"], P["ab_norm_g"], cache["ab_delta"][0],
                     c=TS_PAD, t_valid=ts_, gate_group=4, gate_off=GATE_OFF)
    mix = jnp.concatenate([oa[:, :ts_].reshape(m, hw), ob], axis=1)
    x = _matmul_norm_res(mix, P["ab_w_out"], ng[1:2], g1, x, tm=m, tk=512, hi=True)
    x = _mlp(x, (sh2, sc2, g2), ng, (P["mlp_w1"], 0), (P["mlp_w2"], 0), tm=m, hi=True)
    conv_in = jnp.concatenate([cache["ab_conv"][0], big.reshape(b, ts_, -1)[:, :, :3 * hw]], axis=1)
    outs0 = (kb_r, vb, ki_r, s_new, conv_in[:, ts_:])
    sh1, sc1, g1, sh2, sc2, g2 = (rows(a) for a in mods[1])
    ng = P["norm_g"][1]
    lam_init = 0.8 - 0.6 * math.exp(-0.3 * 1)
    proj = _norm_matmul(x, ng[0:1], sc1, sh1, P["c_w_in"], tm=m, tn=512, hi=True)
    qk_r = _rope(proj, tab64, col0=0, ngroups=2 * C_HEADS, n_kind0=2 * C_HEADS, r1=32, r2=96, tm=m)
    cw = 2 * C_HEADS * C_HD
    q3 = qk_r[:, :cw].reshape(b, ts_, cw) * C_HD ** -0.5
    k3 = qk_r[:, cw:].reshape(b, ts_, cw)
    v3 = proj[:, 2 * cw:].reshape(b, ts_, cw)
    lane_hp = jnp.arange(cw, dtype=I32) // C_HD
    want = (2 * jnp.arange(C_HEADS, dtype=I32)[:, None] + jnp.arange(2, dtype=I32)[None, :])
    sel = (lane_hp[None, None, :] == want[:, :, None]).astype(F32)
    wt = (q3[:, None, None, :, :] * sel[None, :, :, None, :]).reshape(b, 2 * C_HEADS * ts_, cw).astype(BF16)
    ck = cache["c_k"]
    pool_ckt = jnp.transpose(ck, (0, 1, 3, 4, 5, 2)).reshape(ck.shape[:2] + (cw, PAGE))
    pool_cv2 = cache["c_v"].reshape(cache["c_v"].shape[:2] + (PAGE * C_HEADS, 2 * C_HD))
    o = _diff_attn_s(page_table, wt, pool_ckt, pool_cv2, pad_t(k3, TS_PAD), pad_t(v3, TS_PAD), P["c_lambda"],
                     P["c_norm_g"], layer=0, t_new=ts_, lam_init=lam_init, n_pg=min(8, n_pages))
    x = _matmul_norm_res(o.reshape(m, cw), P["c_w_out"], ng[1:2], g1, x, tm=m, tk=512, hi=True)
    x = _mlp(x, (sh2, sc2, g2), ng, (P["mlp_w1"], 1), (P["mlp_w2"], 1), tm=m, hi=True)
    outs1 = (qk_r[:, cw:], proj[:, 2 * cw:])
    return x.reshape(b, ts_, d), outs0, outs1


def kernel(x_prompt, x_sample, cache_ab_k, cache_ab_v, cache_ab_kidx, state_ab_delta, state_ab_conv, cache_c_k,
           cache_c_v, page_table, c_prompt, c_sample, ada_w, ada_b, norm_g, mlp_w1, mlp_w2, ab_w_in, ab_w_out,
           ab_conv_w, ab_A_log, ab_dt_bias, ab_norm_g, c_w_in, c_w_out, c_lambda, c_norm_g):
    bp, t, d = x_prompt.shape
    bs, ts_, _ = x_sample.shape
    assert bp == 1
    past_len = page_table.shape[1] * PAGE
    w_big, w_small = _split_ab_weights(ab_w_in[0])
    P = {"norm_g": norm_g, "mlp_w1": mlp_w1, "mlp_w2": mlp_w2, "w_big": w_big, "w_small": w_small,
         "ab_w_out": ab_w_out[0], "ab_conv_w": ab_conv_w[0], "ab_A_log": ab_A_log[0], "ab_dt_bias": ab_dt_bias[0],
         "ab_norm_g": ab_norm_g[0], "c_w_in": c_w_in[0], "c_w_out": c_w_out[0], "c_lambda": c_lambda[0],
         "c_norm_g": c_norm_g[0]}
    n_seq = bp + bs
    mc = -(-n_seq // 8) * 8
    c_all = jnp.pad(jnp.concatenate([c_prompt, c_sample], axis=0), ((0, mc - n_seq), (0, 0)))
    mod = _ada(c_all, ada_w, ada_b)
    mods_p = [tuple(mod[i, 0:bp, n * d:(n + 1) * d] for n in range(6)) for i in range(2)]
    mods_s = [tuple(mod[i, bp:n_seq, n * d:(n + 1) * d] for n in range(6)) for i in range(2)]

    y_p, p0, p1 = _prompt_trunk(x_prompt[0], mods_p, P)
    cache = {"ab_k": cache_ab_k, "ab_v": cache_ab_v, "ab_kidx": cache_ab_kidx, "ab_delta": state_ab_delta,
             "ab_conv": state_ab_conv, "c_k": cache_c_k, "c_v": cache_c_v, "page_table": page_table}
    y_s, s0, s1 = _sample_trunk(x_sample, mods_s, P, cache, past_len)

    return (y_p[None], y_s,
            p0[0].reshape(1, 1, t, B_KV, B_HD), p0[1].reshape(1, 1, t, B_KV, B_HD), p0[2].reshape(1, 1, t, IDX_HD),
            p0[3][None], p0[4].reshape(1, 1, A_CONV - 1, -1),
            p1[0].reshape(1, 1, t, C_HEADS, 2, C_HD), p1[1].reshape(1, 1, t, C_HEADS, 2 * C_HD),
            s0[0].reshape(1, bs, ts_, B_KV, B_HD), s0[1].reshape(1, bs, ts_, B_KV, B_HD),
            s0[2].reshape(1, bs, ts_, IDX_HD), s0[3][None], s0[4][None],
            s1[0].reshape(1, bs, ts_, C_HEADS, 2, C_HD), s1[1].reshape(1, bs, ts_, C_HEADS, 2 * C_HD))
```

```python
import functools
import math

import jax
import jax.numpy as jnp
from jax import lax
from jax.experimental import pallas as pl
from jax.experimental.pallas import tpu as pltpu

F32 = jnp.float32
BF16 = jnp.bfloat16
I32 = jnp.int32

EPS = 1e-6
NEG_BIG = -1e30
ROPE_THETA = 10000.0
PAGE = 128
A_HEADS = 8
A_DK = 128
A_CONV = 4
B_HEADS = 8
B_KV = 2
B_HD = 128
IDX_HEADS = 8
IDX_HD = 64
TOPK = 256
C_HEADS = 16
C_HD = 64
LANES = 128

NN = (((1,), (0,)), ((), ()))
NT = (((1,), (1,)), ((), ()))
TN = (((0,), (0,)), ((), ()))
BNN = (((2,), (1,)), ((0,), (0,)))
BNT = (((2,), (2,)), ((0,), (0,)))
INT_MIN = -2147483648
KEY_NEG_INF = -2139095041


def _cp(dims, vmem_mb=None):
    kw = dict(dimension_semantics=dims)
    if vmem_mb is not None:
        kw["vmem_limit_bytes"] = vmem_mb << 20
    return pltpu.CompilerParams(**kw)


def _dot(a, b, dims=NN):
    return lax.dot_general(a, b, dims, preferred_element_type=F32)


def _split2(a):
    hi = a.astype(BF16)
    return hi, (a - hi.astype(F32)).astype(BF16)


def _split3(a):
    hi = a.astype(BF16)
    r = a - hi.astype(F32)
    mid = r.astype(BF16)
    return hi, mid, (r - mid.astype(F32)).astype(BF16)


def _dot1(a, b, dims=NN):
    return _dot(a.astype(BF16), b.astype(BF16), dims)


def _dot3(a, b, dims=NN):
    ah, al = _split2(a)
    bh, bl = _split2(b)
    return _dot(ah, bh, dims) + (_dot(ah, bl, dims) + _dot(al, bh, dims))


def _dot_exact_left(ones_bf16, b, dims=NN):
    b0, b1, b2 = _split3(b)
    return _dot(ones_bf16, b0, dims) + (_dot(ones_bf16, b1, dims) + _dot(ones_bf16, b2, dims))


def _sigmoid(x):
    return 1.0 / (1.0 + jnp.exp(-x))


def _silu(x):
    return x * _sigmoid(x)


def _ada_kernel(c_ref, w_ref, b_ref, o_ref):
    o_ref[0] = _dot3(_silu(c_ref[...]), w_ref[0]) + b_ref[0]


def _ada(c_all, ada_w, ada_b, tn=512):
    nl, d, n = ada_w.shape
    mc = c_all.shape[0]
    return pl.pallas_call(
        _ada_kernel,
        grid=(nl, n // tn),
        in_specs=[pl.BlockSpec((mc, d), lambda l, j: (0, 0)),
                  pl.BlockSpec((1, d, tn), lambda l, j: (l, 0, j)),
                  pl.BlockSpec((1, 1, tn), lambda l, j: (l, 0, j))],
        out_specs=pl.BlockSpec((1, mc, tn), lambda l, j: (l, 0, j)),
        out_shape=jax.ShapeDtypeStruct((nl, mc, n), F32),
        compiler_params=_cp(("arbitrary", "arbitrary"), 40),
        name="ada_mod",
    )(c_all, ada_w, ada_b.reshape(nl, 1, n))


def _norm_mm_kernel(x_ref, g_ref, sc_ref, sh_ref, w_ref, o_ref, *scratch, hi, relu2):
    hh_ref = scratch[0]

    @pl.when(pl.program_id(1) == 0)
    def _():
        x = x_ref[...]
        y = x * lax.rsqrt(jnp.mean(x * x, axis=-1, keepdims=True) + EPS)
        h = (y * g_ref[...]) * (1.0 + sc_ref[...]) + sh_ref[...]
        hh = h.astype(BF16)
        hh_ref[...] = hh
        if hi:
            scratch[1][...] = (h - hh.astype(F32)).astype(BF16)

    w = w_ref[0]
    wh = w.astype(BF16)
    acc = _dot(hh_ref[...], wh)
    if hi:
        wl = (w - wh.astype(F32)).astype(BF16)
        acc = acc + (_dot(hh_ref[...], wl) + _dot(scratch[1][...], wh))
    if relu2:
        acc = jnp.square(jnp.maximum(acc, 0.0))
    o_ref[...] = acc.astype(o_ref.dtype)


def _layer_weight(w):
    return w if isinstance(w, tuple) else (w[None], 0)


def _norm_matmul(x, g, sc, sh, w, *, tm, tn, hi=False, relu2=False, out_dtype=F32, vmem_mb=48):
    m, d = x.shape
    tm = min(tm, m)
    w, layer = _layer_weight(w)
    n = w.shape[2]
    per_row = sc.shape[0] != 1
    mod_spec = pl.BlockSpec((tm, d), lambda i, j: (i, 0)) if per_row else pl.BlockSpec((1, d), lambda i, j: (0, 0))
    scratch = [pltpu.VMEM((tm, d), BF16)] + ([pltpu.VMEM((tm, d), BF16)] if hi else [])
    return pl.pallas_call(
        functools.partial(_norm_mm_kernel, hi=hi, relu2=relu2),
        grid=(m // tm, n // tn),
        in_specs=[pl.BlockSpec((tm, d), lambda i, j: (i, 0), pipeline_mode=pl.Buffered(1)),
                  pl.BlockSpec((1, d), lambda i, j: (0, 0)),
                  mod_spec, mod_spec,
                  pl.BlockSpec((1, d, tn), lambda i, j: (layer, 0, j))],
        out_specs=pl.BlockSpec((tm, tn), lambda i, j: (i, j)),
        out_shape=jax.ShapeDtypeStruct((m, n), out_dtype),
        scratch_shapes=scratch,
        compiler_params=_cp(("arbitrary", "arbitrary"), vmem_mb),
        name="norm_matmul",
    )(x, g, sc, sh, w)


def _mm_norm_res_kernel(a_ref, w_ref, ng_ref, gate_ref, res_ref, o_ref, *, hi):
    k = pl.program_id(1)

    @pl.when(k == 0)
    def _():
        o_ref[...] = jnp.zeros(o_ref.shape, F32)

    if hi:
        o_ref[...] += _dot3(a_ref[...].astype(F32), w_ref[0])
    else:
        o_ref[...] += _dot1(a_ref[...], w_ref[0])

    @pl.when(k == pl.num_programs(1) - 1)
    def _():
        m = o_ref[...]
        y = m * lax.rsqrt(jnp.mean(m * m, axis=-1, keepdims=True) + EPS)
        o_ref[...] = res_ref[...] + gate_ref[...] * (y * ng_ref[...])


def _matmul_norm_res(a, w, ng, gate, res, *, tm, tk, hi=False, vmem_mb=48):
    m, kdim = a.shape
    w, layer = _layer_weight(w)
    n = w.shape[2]
    per_row = gate.shape[0] != 1
    gate_spec = pl.BlockSpec((tm, n), lambda i, k: (i, 0)) if per_row else pl.BlockSpec((1, n), lambda i, k: (0, 0))
    return pl.pallas_call(
        functools.partial(_mm_norm_res_kernel, hi=hi),
        grid=(m // tm, kdim // tk),
        in_specs=[pl.BlockSpec((tm, tk), lambda i, k: (i, k)),
                  pl.BlockSpec((1, tk, n), lambda i, k: (layer, k, 0)),
                  pl.BlockSpec((1, n), lambda i, k: (0, 0)),
                  gate_spec,
                  pl.BlockSpec((tm, n), lambda i, k: (i, 0), pipeline_mode=pl.Buffered(1))],
        out_specs=pl.BlockSpec((tm, n), lambda i, k: (i, 0)),
        out_shape=jax.ShapeDtypeStruct((m, n), F32),
        compiler_params=_cp(("arbitrary", "arbitrary"), vmem_mb),
        name="matmul_norm_res",
    )(a, w, ng, gate, res)


def _rope_kernel(x_ref, tab_ref, o_ref, *, r1, r2, gw):
    for k in range(gw):
        x = x_ref[:, k * LANES:(k + 1) * LANES]
        out = x * tab_ref[:, 0:LANES] + pltpu.roll(x, r1, 1) * tab_ref[:, LANES:2 * LANES]
        if r2 is not None:
            out = out + pltpu.roll(x, r2, 1) * tab_ref[:, 2 * LANES:3 * LANES]
        o_ref[:, k * LANES:(k + 1) * LANES] = out


def _rope(x, tab, *, col0, ngroups, n_kind0, r1, r2, tm, gw=1):
    t = x.shape[0]
    assert col0 % gw == 0 and ngroups % gw == 0 and n_kind0 % gw == 0
    return pl.pallas_call(
        functools.partial(_rope_kernel, r1=r1, r2=r2, gw=gw),
        grid=(t // tm, ngroups // gw),
        in_specs=[pl.BlockSpec((tm, gw * LANES), lambda i, j: (i, col0 // gw + j)),
                  pl.BlockSpec((tm, 3 * LANES), lambda i, j: (i, jnp.where(j * gw >= n_kind0, 1, 0)))],
        out_specs=pl.BlockSpec((tm, gw * LANES), lambda i, j: (i, j)),
        out_shape=jax.ShapeDtypeStruct((t, ngroups * LANES), F32),
        compiler_params=_cp(("arbitrary", "arbitrary")),
        name="rope",
    )(x, tab)


def _rope_tables(pos):
    p = pos.astype(F32)[:, None]

    def cs(half):
        inv = ROPE_THETA ** (-jnp.arange(half, dtype=F32) / half)
        ang = p * inv[None, :]
        return jnp.cos(ang), jnp.sin(ang)

    c, s = cs(64)
    tab128 = jnp.concatenate([c, c, -s, s, jnp.zeros_like(c), jnp.zeros_like(c)], axis=1)
    c, s = cs(32)
    z = jnp.zeros_like(c)
    one64 = jnp.ones((pos.shape[0], 64), F32)
    z64 = jnp.zeros((pos.shape[0], 64), F32)
    kind0 = jnp.concatenate([c, c, c, c, z, s, z, s, -s, z, -s, z], axis=1)
    kind1 = jnp.concatenate([c, c, one64, z, s, z64, -s, z, z64], axis=1)
    return tab128, jnp.concatenate([kind0, kind1], axis=1)


def _gdn_kernel(qkv_ref, z_ref, gate_ref, buf_ref, cw_ref, alog_ref, dtb_ref, ng_ref, s0_ref,
                o_ref, sout_ref, xbuf_ref, s_ref, *, c, t_valid, gate_off):
    ci = pl.program_id(1)
    nh, dk = A_HEADS, A_DK
    hw = nh * dk

    @pl.when(ci == 0)
    def _():
        s_ref[...] = s0_ref[0]
        xbuf_ref[0:8, :] = jnp.zeros((8, 3 * hw), F32)
        xbuf_ref[8 - (A_CONV - 1):8, :] = buf_ref[0]

    xbuf_ref[8:8 + c, :] = qkv_ref[0]
    y = xbuf_ref[5:5 + c, :] * cw_ref[0:1, :]
    for j in range(1, A_CONV):
        y = y + xbuf_ref[5 + j:5 + j + c, :] * cw_ref[j:j + 1, :]
    tail = xbuf_ref[8 + c - 3:8 + c, :]
    xbuf_ref[5:8, :] = tail
    y = _silu(y)

    gt = gate_ref[0]
    ba = gt[:, gate_off:gate_off + nh]
    aa = gt[:, gate_off + nh:gate_off + 2 * nh]
    beta = _sigmoid(ba)
    xs = aa + dtb_ref[...]
    softplus = jnp.maximum(xs, 0.0) + jnp.log1p(jnp.exp(-jnp.abs(xs)))
    g = -jnp.exp(alog_ref[...]) * softplus
    row = lax.broadcasted_iota(I32, (c, c), 0)
    col = lax.broadcasted_iota(I32, (c, c), 1)
    if t_valid < c:
        valid = lax.broadcasted_iota(I32, (c, nh), 0) < t_valid
        beta = jnp.where(valid, beta, 0.0)
        g = jnp.where(valid, g, 0.0)
    incl = (row >= col)[None]
    strict = (row > col)[None]
    tri = jnp.where(row >= col, 1.0, 0.0).astype(BF16)
    gc = _dot_exact_left(tri, g)
    eye_h = jnp.where(lax.broadcasted_iota(I32, (nh, nh), 0) == lax.broadcasted_iota(I32, (nh, nh), 1),
                      1.0, 0.0).astype(BF16)
    gct = _dot_exact_left(eye_h, gc, NT)
    eye_c = jnp.where(row == col, 1.0, 0.0)[None]

    def heads(a, off):
        return jnp.stack([a[:, off + h * dk:off + (h + 1) * dk] for h in range(nh)], axis=0)

    def cols(a):
        return jnp.stack([a[:, h:h + 1] for h in range(nh)], axis=0)

    q3 = heads(y, 0)
    k3 = heads(y, hw)
    v3 = heads(y, 2 * hw)
    q3 = q3 * lax.rsqrt(jnp.sum(q3 * q3, axis=-1, keepdims=True) + 1e-6)
    k3 = k3 * lax.rsqrt(jnp.sum(k3 * k3, axis=-1, keepdims=True) + 1e-6)
    b3 = cols(beta)
    gch = cols(gc)
    gct3 = jnp.stack([gct[h:h + 1, :] for h in range(nh)], axis=0)
    decay = jnp.where(incl, jnp.exp(jnp.minimum(gch - gct3, 0.0)), 0.0)
    kb = k3 * b3
    vb = v3 * b3
    a_mat = jnp.where(strict, _dot3(kb, k3, BNT) * decay, 0.0)
    def same_block(s):
        sh = int(round(math.log2(s)))
        return (jnp.right_shift(row, sh) == jnp.right_shift(col, sh))[None]

    s_blk = min(8, c)
    pw = jnp.where(same_block(s_blk), -a_mat, 0.0)
    t_inv = eye_c + pw
    for _ in range(int(round(math.log2(s_blk))) - 1):
        pw = _dot3(pw, pw, BNN)
        t_inv = t_inv + _dot3(t_inv, pw, BNN)
    while s_blk < c:
        off = jnp.where(jnp.logical_and(same_block(2 * s_blk), jnp.logical_not(same_block(s_blk))), a_mat, 0.0)
        t_inv = t_inv - _dot3(t_inv, _dot3(off, t_inv, BNN), BNN)
        s_blk *= 2
    egc = jnp.exp(gch)
    u = _dot3(t_inv, vb, BNN)
    w = _dot3(t_inv, kb * egc, BNN)
    qs = q3 * dk ** -0.5
    qk = _dot3(qs, k3, BNT) * decay
    g_last = gch[:, c - 1:c, :]
    k_dec = k3 * jnp.exp(jnp.minimum(g_last - gch, 0.0))
    s_old = s_ref[...]
    v_new = u - _dot3(w, s_old, BNN)
    o = _dot3(qs * egc, s_old, BNN) + _dot3(qk, v_new, BNN)
    eg_last = jnp.exp(g_last)
    on = o * lax.rsqrt(jnp.mean(o * o, axis=-1, keepdims=True) + EPS) * ng_ref[...]
    for h in range(nh):
        s_ref[h] = s_old[h] * eg_last[h] + _dot3(k_dec[h], v_new[h], TN)
        o_ref[0, :, h * dk:(h + 1) * dk] = on[h] * _silu(z_ref[0, :, h * dk:(h + 1) * dk])

    @pl.when(ci == pl.num_programs(1) - 1)
    def _():
        sout_ref[0] = s_ref[...]


def _gdn(proj, small, conv_buf, conv_w, a_log, dt_bias, norm_g, s0, *, c, t_valid, gate_group, gate_off):
    b, t = proj.shape[:2]
    nh, dk = A_HEADS, A_DK
    hw = nh * dk
    return pl.pallas_call(
        functools.partial(_gdn_kernel, c=c, t_valid=t_valid, gate_off=gate_off),
        grid=(b, t // c),
        in_specs=[pl.BlockSpec((1, c, 3 * hw), lambda bi, ci: (bi, ci, 0)),
                  pl.BlockSpec((1, c, hw), lambda bi, ci: (bi, ci, 3)),
                  pl.BlockSpec((1, c, LANES), lambda bi, ci: (bi, ci, gate_group)),
                  pl.BlockSpec((1, A_CONV - 1, 3 * hw), lambda bi, ci: (bi, 0, 0)),
                  pl.BlockSpec((A_CONV, 3 * hw), lambda bi, ci: (0, 0)),
                  pl.BlockSpec((1, nh), lambda bi, ci: (0, 0)),
                  pl.BlockSpec((1, nh), lambda bi, ci: (0, 0)),
                  pl.BlockSpec((1, dk), lambda bi, ci: (0, 0)),
                  pl.BlockSpec((1, nh, dk, dk), lambda bi, ci: (bi, 0, 0, 0))],
        out_specs=[pl.BlockSpec((1, c, hw), lambda bi, ci: (bi, ci, 0)),
                   pl.BlockSpec((1, nh, dk, dk), lambda bi, ci: (bi, 0, 0, 0))],
        out_shape=[jax.ShapeDtypeStruct((b, t, hw), F32), jax.ShapeDtypeStruct((b, nh, dk, dk), F32)],
        scratch_shapes=[pltpu.VMEM((8 + c, 3 * hw), F32), pltpu.VMEM((nh, dk, dk), F32)],
        compiler_params=_cp(("arbitrary", "arbitrary"), 40),
        name="gdn",
    )(proj, proj, small, conv_buf, conv_w, a_log.reshape(1, nh), dt_bias.reshape(1, nh), norm_g.reshape(1, dk), s0)


def _idx_prep_kernel(x_ref, q_ref, k_ref):
    tm = x_ref.shape[0]
    lo_half = lax.broadcasted_iota(I32, (tm, LANES), 1) < 64

    def split(xg):
        hi = xg.astype(BF16).astype(F32)
        return hi, xg - hi

    for gq in range(IDX_HEADS // 2):
        hi, lo = split(x_ref[:, gq * LANES:(gq + 1) * LANES])
        hi_r = pltpu.roll(hi, 64, 1)
        lo_r = pltpu.roll(lo, 64, 1)
        base = 2 * gq * 256
        q_ref[:, base:base + 128] = jnp.where(lo_half, hi, hi_r).astype(BF16)
        q_ref[:, base + 128:base + 256] = jnp.where(lo_half, lo, lo_r).astype(BF16)
        q_ref[:, base + 256:base + 384] = jnp.where(lo_half, hi_r, hi).astype(BF16)
        q_ref[:, base + 384:base + 512] = jnp.where(lo_half, lo_r, lo).astype(BF16)
    hi, lo = split(x_ref[:, 4 * LANES:5 * LANES])
    kk = jnp.where(lo_half, hi, pltpu.roll(lo, 64, 1)).astype(BF16)
    k_ref[:, 0:128] = kk
    k_ref[:, 128:256] = kk


def _idx_prep(small_r, tm=512):
    t = small_r.shape[0]
    return pl.pallas_call(
        _idx_prep_kernel,
        grid=(t // tm,),
        in_specs=[pl.BlockSpec((tm, 5 * LANES), lambda i: (i, 0))],
        out_specs=[pl.BlockSpec((tm, IDX_HEADS * 256), lambda i: (i, 0)),
                   pl.BlockSpec((tm, 256), lambda i: (i, 0))],
        out_shape=[jax.ShapeDtypeStruct((t, IDX_HEADS * 256), BF16), jax.ShapeDtypeStruct((t, 256), BF16)],
        compiler_params=_cp(("arbitrary",)),
        name="idx_prep",
    )(small_r)


IDX_SCALE = IDX_HEADS ** -0.5 * IDX_HD ** -0.5
W_OFF = 64


def _idx_scores_kernel(q_ref, k_ref, w_ref, o_ref, *, tq, ts):
    i = pl.program_id(0)
    j = pl.program_id(1)
    live = j * ts <= i * tq + tq - 1

    @pl.when(live)
    def _():
        k = k_ref[...]
        wg = w_ref[...] * IDX_SCALE
        acc = jnp.zeros((tq, ts), F32)
        for h in range(IDX_HEADS):
            s = _dot(q_ref[:, h * 256:(h + 1) * 256], k, NT)
            acc = acc + wg[:, W_OFF + h:W_OFF + h + 1] * jnp.maximum(s, 0.0)
        row = i * tq + lax.broadcasted_iota(I32, (tq, ts), 0)
        col = j * ts + lax.broadcasted_iota(I32, (tq, ts), 1)
        o_ref[...] = jnp.where(col <= row, acc, -jnp.inf)

    @pl.when(jnp.logical_not(live))
    def _():
        o_ref[...] = jnp.full((tq, ts), -jnp.inf, F32)


def _idx_scores(qcat, kcat, small_r, *, tq=256, ts=512):
    t = qcat.shape[0]
    return pl.pallas_call(
        functools.partial(_idx_scores_kernel, tq=tq, ts=ts),
        grid=(t // tq, t // ts),
        in_specs=[pl.BlockSpec((tq, IDX_HEADS * 256), lambda i, j: (i, 0)),
                  pl.BlockSpec((ts, 256), lambda i, j: (j, 0)),
                  pl.BlockSpec((tq, LANES), lambda i, j: (i, 4))],
        out_specs=pl.BlockSpec((tq, ts), lambda i, j: (i, j)),
        out_shape=jax.ShapeDtypeStruct((t, t), F32),
        compiler_params=_cp(("arbitrary", "arbitrary")),
        name="idx_scores",
    )(qcat, kcat, small_r)


def _thr_kernel(s_ref, thr_ref, need_ref, tie_ref, key_ref, *, k, cw, causal):
    tr, s_cols = s_ref.shape
    nchunks = ((pl.program_id(0) + 1) * tr + cw - 1) // cw if causal else s_cols // cw
    kf = float(k)

    def fill(ci, carry):
        off = pl.multiple_of(ci * cw, cw)
        bits = pltpu.bitcast(s_ref[:, pl.ds(off, cw)] + 0.0, I32)
        key_ref[:, pl.ds(off, cw)] = jnp.where(bits < 0, bits ^ 0x7FFFFFFF, bits)
        return carry

    lax.fori_loop(0, nchunks, fill, 0)

    def count(cand, strict):
        def cbody(ci, acc):
            off = pl.multiple_of(ci * cw, cw)
            blk = key_ref[:, pl.ds(off, cw)]
            for t in range(cw // LANES):
                kk = blk[:, t * LANES:(t + 1) * LANES]
                acc = acc + jnp.where(kk > cand if strict else kk >= cand, 1.0, 0.0)
            return acc

        acc = lax.fori_loop(0, nchunks, cbody, jnp.zeros((tr, LANES), F32))
        return jnp.sum(acc, axis=-1, keepdims=True)

    cnt0 = count(0, False)
    p0 = jnp.where(cnt0 >= kf, 0, INT_MIN).astype(I32)
    visited = (nchunks * cw).astype(F32) if causal else float(s_cols)
    cnt_p0 = jnp.where(cnt0 >= kf, cnt0, visited)

    def cond(state):
        b, _, cnt_p = state
        return jnp.logical_and(b < 31, jnp.max(cnt_p) > kf)

    def body(state):
        b, p, cnt_p = state
        cand = p | jnp.left_shift(jnp.int32(1), 30 - b)
        cnt = count(cand, False)
        take = cnt >= kf
        return b + 1, jnp.where(take, cand, p), jnp.where(take, cnt, cnt_p)

    _, p, _ = lax.while_loop(cond, body, (jnp.int32(0), p0, cnt_p0))
    p = jnp.maximum(p, KEY_NEG_INF)
    thr_ref[...] = pltpu.bitcast(jnp.where(p < 0, p ^ 0x7FFFFFFF, p), F32)
    need_ref[...] = kf - count(p, True)
    tie_ref[...] = jnp.where(jnp.logical_and(count(p, False) > kf, p != KEY_NEG_INF), 1, 0).astype(I32)


def _topk_thr(scores, *, k, cw, causal, tr):
    r, s = scores.shape
    tr = min(tr, r)
    assert cw >= k and s % cw == 0
    return pl.pallas_call(
        functools.partial(_thr_kernel, k=k, cw=cw, causal=causal),
        grid=(r // tr,),
        in_specs=[pl.BlockSpec((tr, s), lambda i: (i, 0))],
        out_specs=[pl.BlockSpec((tr, 1), lambda i: (i, 0))] * 3,
        out_shape=[jax.ShapeDtypeStruct((r, 1), F32), jax.ShapeDtypeStruct((r, 1), F32),
                   jax.ShapeDtypeStruct((r, 1), I32)],
        scratch_shapes=[pltpu.VMEM((tr, s), I32)],
        compiler_params=_cp(("arbitrary",), 32),
        name="topk_thr",
    )(scores)


def _select_mask(sc, thr, need, causal, eq_before, tie):
    if not tie:
        return jnp.logical_and(sc >= thr, causal), None
    eq = sc == thr
    n = sc.shape[1]
    upper = jnp.where(lax.broadcasted_iota(I32, (n, n), 0) < lax.broadcasted_iota(I32, (n, n), 1), 1.0, 0.0)
    eqf = jnp.where(eq, 1.0, 0.0)
    rank = eq_before + _dot(eqf.astype(BF16), upper.astype(BF16))
    sel = jnp.logical_or(sc > thr, jnp.logical_and(eq, rank < need))
    return jnp.logical_and(sel, causal), eq_before + jnp.sum(eqf, axis=-1, keepdims=True)


def _flash_update(s, m_ref, acc_ref, r0, rows, vblk):
    ts = s.shape[1]
    m_old = m_ref[r0:r0 + rows, :]
    m_new = jnp.maximum(m_old, jnp.max(s, axis=-1, keepdims=True))
    p = jnp.exp(s - jnp.tile(m_new, (1, ts // LANES)))
    alpha = jnp.exp(m_old - m_new)
    acc_ref[r0:r0 + rows, :] = (jnp.tile(alpha, (1, acc_ref.shape[1] // LANES)) * acc_ref[r0:r0 + rows, :]
                                + _dot(p.astype(BF16), vblk))
    m_ref[r0:r0 + rows, :] = m_new


def _dsa_attn_kernel(tie_ref, q_ref, k_ref, v_ref, sc_ref, thr_ref, need_ref, o_ref,
                     kb_ref, vb_ref, qs_ref, m_ref, acc_ref, eq_ref, *, tq, ts):
    i = pl.program_id(1)
    hg = B_HEADS // B_KV
    d = B_HD

    @pl.when(i == 0)
    def _():
        kb_ref[...] = k_ref[...].astype(BF16)
        vb_ref[:, 0:d] = v_ref[...].astype(BF16)
        vb_ref[:, d:2 * d] = jnp.ones((vb_ref.shape[0], d), BF16)

    for h in range(hg):
        qs_ref[h * tq:(h + 1) * tq, :] = (q_ref[:, h * d:(h + 1) * d] * d ** -0.5).astype(BF16)
    m_ref[...] = jnp.full(m_ref.shape, NEG_BIG, F32)
    acc_ref[...] = jnp.zeros(acc_ref.shape, F32)
    eq_ref[...] = jnp.zeros(eq_ref.shape, F32)
    thr = thr_ref[...]
    need = need_ref[...]
    nblk = (i * tq + tq + ts - 1) // ts

    def run(tie):
        def body(jb, carry):
            off = pl.multiple_of(jb * ts, ts)
            sc = sc_ref[:, pl.ds(off, ts)]
            row = i * tq + lax.broadcasted_iota(I32, (tq, ts), 0)
            col = off + lax.broadcasted_iota(I32, (tq, ts), 1)
            mask, eq_new = _select_mask(sc, thr, need, col <= row, eq_ref[...], tie)
            if tie:
                eq_ref[...] = eq_new
            kblk = kb_ref[pl.ds(off, ts), :]
            vblk = vb_ref[pl.ds(off, ts), :]
            for h in range(hg):
                s = jnp.where(mask, _dot(qs_ref[h * tq:(h + 1) * tq, :], kblk, NT), NEG_BIG)
                _flash_update(s, m_ref, acc_ref, h * tq, tq, vblk)
            return carry

        lax.fori_loop(0, nblk, body, 0)

    has_tie = tie_ref[i] > 0

    @pl.when(has_tie)
    def _():
        run(True)

    @pl.when(jnp.logical_not(has_tie))
    def _():
        run(False)

    for h in range(hg):
        o_ref[:, h * d:(h + 1) * d] = acc_ref[h * tq:(h + 1) * tq, 0:d] / acc_ref[h * tq:(h + 1) * tq, d:2 * d]


def _dsa_attn(tie_blk, q_r, k_r, v_src, v_col0, scores, thr, need, *, tq, ts=512):
    t = q_r.shape[0]
    hg = B_HEADS // B_KV
    once = dict(pipeline_mode=pl.Buffered(1))
    grid_spec = pltpu.PrefetchScalarGridSpec(
        num_scalar_prefetch=1,
        grid=(B_KV, t // tq),
        in_specs=[pl.BlockSpec((tq, hg * B_HD), lambda g, i, tie: (i, g)),
                  pl.BlockSpec((t, B_HD), lambda g, i, tie: (0, g), **once),
                  pl.BlockSpec((t, B_HD), lambda g, i, tie: (0, v_col0 + g), **once),
                  pl.BlockSpec((tq, t), lambda g, i, tie: (i, 0)),
                  pl.BlockSpec((tq, 1), lambda g, i, tie: (i, 0)),
                  pl.BlockSpec((tq, 1), lambda g, i, tie: (i, 0))],
        out_specs=pl.BlockSpec((tq, hg * B_HD), lambda g, i, tie: (i, g)),
        scratch_shapes=[pltpu.VMEM((t, B_HD), BF16), pltpu.VMEM((t, 2 * B_HD), BF16),
                        pltpu.VMEM((hg * tq, B_HD), BF16), pltpu.VMEM((hg * tq, LANES), F32),
                        pltpu.VMEM((hg * tq, 2 * B_HD), F32), pltpu.VMEM((tq, 1), F32)])
    return pl.pallas_call(
        functools.partial(_dsa_attn_kernel, tq=tq, ts=ts),
        grid_spec=grid_spec,
        out_shape=jax.ShapeDtypeStruct((t, B_HEADS * B_HD), F32),
        compiler_params=_cp(("arbitrary", "arbitrary"), 56),
        name="dsa_attn",
    )(tie_blk, q_r, k_r, v_src, scores, thr, need)


def _lambda_value(lam_ref, lam_init):
    lp = lam_ref[...]
    a = jnp.sum(lp[0:1, :] * lp[1:2, :], axis=-1, keepdims=True)
    b = jnp.sum(lp[2:3, :] * lp[3:4, :], axis=-1, keepdims=True)
    return jnp.exp(a) - jnp.exp(b) + lam_init


def _diff_attn_kernel(q_ref, k_ref, v_ref, lam_ref, ng_ref, o_ref, kb_ref, vb_ref, qs_ref, m_ref, acc_ref,
                      *, tq, ts, rsub, lam_init):
    i = pl.program_id(1)
    d2 = 2 * C_HD
    rows = 2 * tq

    @pl.when(i == 0)
    def _():
        kb_ref[...] = k_ref[...].astype(BF16)
        vb_ref[:, 0:d2] = v_ref[...].astype(BF16)
        vb_ref[:, d2:2 * d2] = jnp.ones((vb_ref.shape[0], d2), BF16)

    q = q_ref[...] * C_HD ** -0.5
    first = lax.broadcasted_iota(I32, (tq, d2), 1) < C_HD
    qs_ref[0:tq, :] = jnp.where(first, q, 0.0).astype(BF16)
    qs_ref[tq:rows, :] = jnp.where(first, 0.0, q).astype(BF16)
    m_ref[...] = jnp.full(m_ref.shape, NEG_BIG, F32)
    acc_ref[...] = jnp.zeros(acc_ref.shape, F32)

    def step(jb, diag_col0):
        off = pl.multiple_of(jb * ts, ts)
        kblk = kb_ref[pl.ds(off, ts), :]
        vblk = vb_ref[pl.ds(off, ts), :]
        for r0 in range(0, rows, rsub):
            q0 = r0 % tq
            if diag_col0 is not None and q0 + rsub - 1 < diag_col0:
                continue
            s = _dot(qs_ref[r0:r0 + rsub, :], kblk, NT)
            if diag_col0 is not None and q0 < diag_col0 + ts - 1:
                qrow = q0 + lax.broadcasted_iota(I32, (rsub, ts), 0)
                s = jnp.where(diag_col0 + lax.broadcasted_iota(I32, (rsub, ts), 1) <= qrow, s, NEG_BIG)
            _flash_update(s, m_ref, acc_ref, r0, rsub, vblk)

    def body(jb, carry):
        step(jb, None)
        return carry

    nfull = (i * tq) // ts
    lax.fori_loop(0, nfull, body, 0)
    for dj in range(tq // ts):
        step(nfull + dj, dj * ts)

    lam = _lambda_value(lam_ref, lam_init)
    o = (acc_ref[0:tq, 0:d2] / acc_ref[0:tq, d2:2 * d2]
         - lam * (acc_ref[tq:rows, 0:d2] / acc_ref[tq:rows, d2:2 * d2]))
    on = o * lax.rsqrt(jnp.mean(o * o, axis=-1, keepdims=True) + EPS) * ng_ref[...]
    o_ref[...] = on * (1.0 - lam_init)


def _diff_attn(qk_r, v_src, v_col0, c_lambda, c_norm_g, *, lam_init, tq=2048, ts=1024, rsub=512):
    t = qk_r.shape[0]
    tq = min(tq, t)
    d2 = 2 * C_HD
    once = dict(pipeline_mode=pl.Buffered(1))
    return pl.pallas_call(
        functools.partial(_diff_attn_kernel, tq=tq, ts=ts, rsub=rsub, lam_init=lam_init),
        grid=(C_HEADS, t // tq),
        in_specs=[pl.BlockSpec((tq, d2), lambda h, i: (i, h)),
                  pl.BlockSpec((t, d2), lambda h, i: (0, C_HEADS + h), **once),
                  pl.BlockSpec((t, d2), lambda h, i: (0, v_col0 + h), **once),
                  pl.BlockSpec((4, C_HD), lambda h, i: (0, 0)),
                  pl.BlockSpec((1, d2), lambda h, i: (0, 0))],
        out_specs=pl.BlockSpec((tq, d2), lambda h, i: (i, h)),
        out_shape=jax.ShapeDtypeStruct((t, C_HEADS * d2), F32),
        scratch_shapes=[pltpu.VMEM((t, d2), BF16), pltpu.VMEM((t, 2 * d2), BF16), pltpu.VMEM((2 * tq, d2), BF16),
                        pltpu.VMEM((2 * tq, LANES), F32), pltpu.VMEM((2 * tq, 2 * d2), F32)],
        compiler_params=_cp(("arbitrary", "arbitrary"), 48),
        name="diff_attn",
    )(qk_r, qk_r, v_src, c_lambda, c_norm_g.reshape(1, d2))


TS_PAD = 8


def _page_specs(n_pg, blk, layer, n_prefetch):
    def spec(k):
        if n_prefetch == 1:
            return pl.BlockSpec((1, 1) + blk, lambda bi, p, pt: (layer, pt[bi, p * n_pg + k], 0, 0))
        return pl.BlockSpec((1, 1) + blk, lambda bi, p, pt, tie: (layer, pt[bi, p * n_pg + k], 0, 0))

    return [spec(k) for k in range(n_pg)]


def _idx_scores_s_kernel(pt_ref, q_ref, w_ref, *refs, n_pg, t_new):
    pools = refs[:n_pg]
    new_ref, o_ref, onew_ref = refs[n_pg:]
    q = q_ref[0]
    w = w_ref[0] * IDX_SCALE

    def weigh(s):
        acc = jnp.zeros((TS_PAD, s.shape[1]), F32)
        for h in range(IDX_HEADS):
            acc = acc + w[h * TS_PAD:(h + 1) * TS_PAD, :] * jnp.maximum(s[h * TS_PAD:(h + 1) * TS_PAD, :], 0.0)
        return acc

    for k in range(n_pg):
        o_ref[0, :, k * PAGE:(k + 1) * PAGE] = weigh(_dot3(q, pools[k][0, 0]))

    @pl.when(pl.program_id(1) == pl.num_programs(1) - 1)
    def _():
        tq = lax.broadcasted_iota(I32, (TS_PAD, PAGE), 0)
        tk = lax.broadcasted_iota(I32, (TS_PAD, PAGE), 1)
        ok = jnp.logical_and(tk <= tq, tk < t_new)
        onew_ref[0] = jnp.where(ok, weigh(_dot3(q, new_ref[0], NT)), -jnp.inf)


def _idx_scores_s(page_table, q_s, w_s, pool_ki_t, ki_new, *, layer, t_new, n_pg=8):
    b, n_pages = page_table.shape
    grid_spec = pltpu.PrefetchScalarGridSpec(
        num_scalar_prefetch=1,
        grid=(b, n_pages // n_pg),
        in_specs=[pl.BlockSpec((1, IDX_HEADS * TS_PAD, IDX_HD), lambda bi, p, pt: (bi, 0, 0)),
                  pl.BlockSpec((1, IDX_HEADS * TS_PAD, 1), lambda bi, p, pt: (bi, 0, 0))]
        + _page_specs(n_pg, (IDX_HD, PAGE), layer, 1)
        + [pl.BlockSpec((1, PAGE, IDX_HD), lambda bi, p, pt: (bi, 0, 0))],
        out_specs=[pl.BlockSpec((1, TS_PAD, n_pg * PAGE), lambda bi, p, pt: (bi, 0, p)),
                   pl.BlockSpec((1, TS_PAD, PAGE), lambda bi, p, pt: (bi, 0, 0))])
    return pl.pallas_call(
        functools.partial(_idx_scores_s_kernel, n_pg=n_pg, t_new=t_new),
        grid_spec=grid_spec,
        out_shape=[jax.ShapeDtypeStruct((b, TS_PAD, n_pages * PAGE), F32),
                   jax.ShapeDtypeStruct((b, TS_PAD, PAGE), F32)],
        compiler_params=_cp(("arbitrary", "arbitrary")),
        name="idx_scores_sample",
    )(page_table, q_s, w_s, *([pool_ki_t] * n_pg), ki_new)


def _dsa_attn_s_kernel(pt_ref, tie_ref, q_ref, *refs, n_pg):
    pk = refs[:n_pg]
    pv = refs[n_pg:2 * n_pg]
    nk_ref, nv_ref, sc_ref, scn_ref, thr_ref, need_ref, o_ref, m_ref, l_ref, acc_ref, eq_ref = refs[2 * n_pg:]
    bi = pl.program_id(0)
    p = pl.program_id(1)
    hg = B_HEADS // B_KV

    @pl.when(p == 0)
    def _():
        m_ref[...] = jnp.full(m_ref.shape, NEG_BIG, F32)
        l_ref[...] = jnp.zeros(l_ref.shape, F32)
        acc_ref[...] = jnp.zeros(acc_ref.shape, F32)
        eq_ref[...] = jnp.zeros(eq_ref.shape, F32)

    thr = thr_ref[0]
    need = need_ref[0]
    scale = B_HD ** -0.5

    def attend(sc, keys_of, vals_of, tie):
        mask8, eq_new = _select_mask(sc, thr, need, sc > -jnp.inf, eq_ref[...], tie)
        if tie:
            eq_ref[...] = eq_new
        mask = jnp.concatenate([mask8] * hg, axis=0)
        for g in range(B_KV):
            s = jnp.where(mask, _dot3(q_ref[0, g] * scale, keys_of(g), NT), NEG_BIG)
            m_old = m_ref[g]
            m_new = jnp.maximum(m_old, jnp.max(s, axis=-1, keepdims=True))
            pr = jnp.where(mask, jnp.exp(s - m_new), 0.0)
            alpha = jnp.exp(m_old - m_new)
            l_ref[g] = alpha * l_ref[g] + jnp.sum(pr, axis=-1, keepdims=True)
            acc_ref[g] = alpha * acc_ref[g] + _dot3(pr, vals_of(g))
            m_ref[g] = m_new

    def paged(refs_):
        return lambda g: jnp.concatenate([r[0, 0, pl.ds(g, PAGE, stride=B_KV), :] for r in refs_], axis=0)

    def run(tie):
        attend(sc_ref[0], paged(pk), paged(pv), tie)

        @pl.when(p == pl.num_programs(1) - 1)
        def _():
            attend(scn_ref[0], lambda g: nk_ref[0, :, g * B_HD:(g + 1) * B_HD],
                   lambda g: nv_ref[0, :, g * B_HD:(g + 1) * B_HD], tie)

    has_tie = tie_ref[bi] > 0

    @pl.when(has_tie)
    def _():
        run(True)

    @pl.when(jnp.logical_not(has_tie))
    def _():
        run(False)

    @pl.when(p == pl.num_programs(1) - 1)
    def _():
        for g in range(B_KV):
            o_ref[0, g] = acc_ref[g] / l_ref[g]


def _dsa_attn_s(page_table, tie_b, q_s, pool_k2, pool_v2, k_new, v_new, scores, scores_new, thr, need,
                *, layer, n_pg=4):
    b, n_pages = page_table.shape
    hg = B_HEADS // B_KV
    rows = hg * TS_PAD
    kvw = B_KV * B_HD
    grid_spec = pltpu.PrefetchScalarGridSpec(
        num_scalar_prefetch=2,
        grid=(b, n_pages // n_pg),
        in_specs=[pl.BlockSpec((1, B_KV, rows, B_HD), lambda bi, p, pt, tie: (bi, 0, 0, 0))]
        + _page_specs(n_pg, (PAGE * B_KV, B_HD), layer, 2) + _page_specs(n_pg, (PAGE * B_KV, B_HD), layer, 2)
        + [pl.BlockSpec((1, PAGE, kvw), lambda bi, p, pt, tie: (bi, 0, 0)),
           pl.BlockSpec((1, PAGE, kvw), lambda bi, p, pt, tie: (bi, 0, 0)),
           pl.BlockSpec((1, TS_PAD, n_pg * PAGE), lambda bi, p, pt, tie: (bi, 0, p)),
           pl.BlockSpec((1, TS_PAD, PAGE), lambda bi, p, pt, tie: (bi, 0, 0)),
           pl.BlockSpec((1, TS_PAD, 1), lambda bi, p, pt, tie: (bi, 0, 0)),
           pl.BlockSpec((1, TS_PAD, 1), lambda bi, p, pt, tie: (bi, 0, 0))],
        out_specs=pl.BlockSpec((1, B_KV, rows, B_HD), lambda bi, p, pt, tie: (bi, 0, 0, 0)),
        scratch_shapes=[pltpu.VMEM((B_KV, rows, 1), F32), pltpu.VMEM((B_KV, rows, 1), F32),
                        pltpu.VMEM((B_KV, rows, B_HD), F32), pltpu.VMEM((TS_PAD, 1), F32)])
    return pl.pallas_call(
        functools.partial(_dsa_attn_s_kernel, n_pg=n_pg),
        grid_spec=grid_spec,
        out_shape=jax.ShapeDtypeStruct((b, B_KV, rows, B_HD), F32),
        compiler_params=_cp(("arbitrary", "arbitrary")),
        name="dsa_attn_sample",
    )(page_table, tie_b, q_s, *([pool_k2] * n_pg), *([pool_v2] * n_pg), k_new, v_new, scores, scores_new, thr, need)


def _diff_attn_s_kernel(pt_ref, wt_ref, *refs, n_pg, t_new, lam_init):
    pk = refs[:n_pg]
    pv = refs[n_pg:2 * n_pg]
    nk_ref, nv_ref, lam_ref, ng_ref, o_ref, m_ref, l_ref, acc_ref = refs[2 * n_pg:]
    p = pl.program_id(1)
    nrow = wt_ref.shape[1]
    rph = 2 * t_new
    dv = 2 * C_HD

    @pl.when(p == 0)
    def _():
        m_ref[...] = jnp.full(m_ref.shape, NEG_BIG, F32)
        l_ref[...] = jnp.zeros(l_ref.shape, F32)
        acc_ref[...] = jnp.zeros(acc_ref.shape, F32)

    wt = wt_ref[0]

    def attend(s, vals_of):
        m_old = m_ref[...]
        m_new = jnp.maximum(m_old, jnp.max(s, axis=-1, keepdims=True))
        pr = jnp.exp(s - m_new)
        alpha = jnp.exp(m_old - m_new)
        l_ref[...] = alpha * l_ref[...] + jnp.sum(pr, axis=-1, keepdims=True)
        prb = pr.astype(BF16)
        for h in range(C_HEADS):
            r0 = h * rph
            acc_ref[r0:r0 + rph, :] = alpha[r0:r0 + rph, :] * acc_ref[r0:r0 + rph, :] + _dot(prb[r0:r0 + rph, :],
                                                                                           vals_of(h))
        m_ref[...] = m_new

    s_past = _dot(wt, jnp.concatenate([r[0, 0] for r in pk], axis=1).astype(BF16))
    attend(s_past, lambda h: jnp.concatenate([r[0, 0, pl.ds(h, PAGE, stride=C_HEADS), :] for r in pv],
                                             axis=0).astype(BF16))

    @pl.when(p == pl.num_programs(1) - 1)
    def _():
        tq = lax.broadcasted_iota(I32, (nrow, TS_PAD), 0) & (t_new - 1)
        tk = lax.broadcasted_iota(I32, (nrow, TS_PAD), 1)
        s_new = jnp.where(tk <= tq, _dot(wt, nk_ref[0].astype(BF16), NT), NEG_BIG)
        attend(s_new, lambda h: nv_ref[0, :, h * dv:(h + 1) * dv].astype(BF16))
        lam = _lambda_value(lam_ref, lam_init)
        a = acc_ref[...] / l_ref[...]
        for h in range(C_HEADS):
            r0 = h * rph
            o = a[r0:r0 + t_new, :] - lam * a[r0 + t_new:r0 + rph, :]
            on = o * lax.rsqrt(jnp.mean(o * o, axis=-1, keepdims=True) + EPS) * ng_ref[...]
            o_ref[0, :, h * dv:(h + 1) * dv] = on * (1.0 - lam_init)


def _diff_attn_s(page_table, wt, pool_kt, pool_v2, k_new, v_new, c_lambda, c_norm_g, *, layer, t_new, lam_init,
                 n_pg=4):
    b, n_pages = page_table.shape
    nrow, width = wt.shape[1:]
    dv = 2 * C_HD
    grid_spec = pltpu.PrefetchScalarGridSpec(
        num_scalar_prefetch=1,
        grid=(b, n_pages // n_pg),
        in_specs=[pl.BlockSpec((1, nrow, width), lambda bi, p, pt: (bi, 0, 0))]
        + _page_specs(n_pg, (width, PAGE), layer, 1) + _page_specs(n_pg, (PAGE * C_HEADS, dv), layer, 1)
        + [pl.BlockSpec((1, TS_PAD, width), lambda bi, p, pt: (bi, 0, 0)),
           pl.BlockSpec((1, TS_PAD, width), lambda bi, p, pt: (bi, 0, 0)),
           pl.BlockSpec((4, C_HD), lambda bi, p, pt: (0, 0)),
           pl.BlockSpec((1, dv), lambda bi, p, pt: (0, 0))],
        out_specs=pl.BlockSpec((1, t_new, width), lambda bi, p, pt: (bi, 0, 0)),
        scratch_shapes=[pltpu.VMEM((nrow, 1), F32), pltpu.VMEM((nrow, 1), F32), pltpu.VMEM((nrow, dv), F32)])
    return pl.pallas_call(
        functools.partial(_diff_attn_s_kernel, n_pg=n_pg, t_new=t_new, lam_init=lam_init),
        grid_spec=grid_spec,
        out_shape=jax.ShapeDtypeStruct((b, t_new, width), F32),
        compiler_params=_cp(("arbitrary", "arbitrary"), 56),
        name="diff_attn_sample",
    )(page_table, wt, *([pool_kt] * n_pg), *([pool_v2] * n_pg), k_new, v_new, c_lambda,
      c_norm_g.reshape(1, dv))


def _split_ab_weights(w_in):
    n_a = 4 * A_HEADS * A_DK
    n_gate = 2 * A_HEADS
    n_b = (B_HEADS + 2 * B_KV) * B_HD
    n_idx = IDX_HEADS * IDX_HD + IDX_HD + IDX_HEADS
    big = jnp.concatenate([w_in[:, :n_a], w_in[:, n_a + n_gate:n_a + n_gate + n_b]], axis=1)
    pad = 5 * LANES - n_idx - n_gate
    small = jnp.concatenate([w_in[:, n_a + n_gate + n_b:], w_in[:, n_a:n_a + n_gate],
                             jnp.zeros((w_in.shape[0], pad), w_in.dtype)], axis=1)
    return big, small


PROMPT_TM = 2048
GATE_OFF = IDX_HD + IDX_HEADS
QB_COL0 = 4 * A_HEADS * A_DK // LANES
KB_COL0 = QB_COL0 + B_HEADS
VB_COL0 = KB_COL0 + B_KV


def _mlp(x, mods, norm_g, w1, w2, *, tm, hi, tn=512, tk=2048):
    sh2, sc2, g2 = mods
    hid = _norm_matmul(x, norm_g[2:3], sc2, sh2, w1, tm=tm, tn=tn, hi=hi, relu2=True,
                       out_dtype=F32 if hi else BF16, vmem_mb=56)
    if hi:
        return _matmul_norm_res(hid, w2, norm_g[3:4], g2, x, tm=tm, tk=512, hi=True)
    return _matmul_norm_res(hid, w2, norm_g[3:4], g2, x, tm=512, tk=tk, vmem_mb=56)


def _prompt_trunk(x, mods, P):
    t = x.shape[0]
    pos = jnp.arange(t, dtype=I32)
    tab128, tab64 = _rope_tables(pos)
    sh1, sc1, g1, sh2, sc2, g2 = mods[0]
    ng = P["norm_g"][0]
    big = _norm_matmul(x, ng[0:1], sc1, sh1, P["w_big"], tm=PROMPT_TM, tn=256, vmem_mb=56)
    small = _norm_matmul(x, ng[0:1], sc1, sh1, P["w_small"], tm=PROMPT_TM // 2, tn=5 * LANES, hi=True)
    small_r = _rope(small, tab64, col0=0, ngroups=5, n_kind0=4, r1=32, r2=96, tm=1024)
    qb_r = _rope(big, tab128, col0=QB_COL0, ngroups=B_HEADS, n_kind0=B_HEADS, r1=64, r2=None, tm=1024, gw=4)
    kb_r = _rope(big, tab128, col0=KB_COL0, ngroups=B_KV, n_kind0=B_KV, r1=64, r2=None, tm=1024)
    qcat, kcat = _idx_prep(small_r)
    scores = _idx_scores(qcat, kcat, small_r)
    thr, need, tie = _topk_thr(scores, k=min(TOPK, t // 4), cw=1024, causal=True, tr=128)
    tq = 256
    tie_blk = jnp.max(tie.reshape(t // tq, tq), axis=1)
    ob = _dsa_attn(tie_blk, qb_r, kb_r, big, VB_COL0, scores, thr, need, tq=tq)
    oa, s_new = _gdn(big[None], small_r[None], jnp.zeros((1, A_CONV - 1, 3 * A_HEADS * A_DK), F32), P["ab_conv_w"],
                     P["ab_A_log"], P["ab_dt_bias"], P["ab_norm_g"], jnp.zeros((1, A_HEADS, A_DK, A_DK), F32),
                     c=128, t_valid=128, gate_group=4, gate_off=GATE_OFF)
    mix = jnp.concatenate([oa[0], ob], axis=1)
    x = _matmul_norm_res(mix, P["ab_w_out"], ng[1:2], g1, x, tm=512, tk=2048, vmem_mb=56)
    x = _mlp(x, (sh2, sc2, g2), ng, (P["mlp_w1"], 0), (P["mlp_w2"], 0), tm=PROMPT_TM, hi=False)
    outs0 = (kb_r, big[:, VB_COL0 * LANES:], small_r[:, 4 * LANES:4 * LANES + IDX_HD], s_new,
             big[t - (A_CONV - 1):, :3 * A_HEADS * A_DK])
    sh1, sc1, g1, sh2, sc2, g2 = mods[1]
    ng = P["norm_g"][1]
    lam_init = 0.8 - 0.6 * math.exp(-0.3 * 1)
    proj = _norm_matmul(x, ng[0:1], sc1, sh1, P["c_w_in"], tm=PROMPT_TM, tn=256, vmem_mb=56)
    qk_r = _rope(proj, tab64, col0=0, ngroups=2 * C_HEADS, n_kind0=2 * C_HEADS, r1=32, r2=96, tm=1024, gw=4)
    o = _diff_attn(qk_r, proj, 2 * C_HEADS, P["c_lambda"], P["c_norm_g"], lam_init=lam_init)
    x = _matmul_norm_res(o, P["c_w_out"], ng[1:2], g1, x, tm=512, tk=2048, vmem_mb=56)
    x = _mlp(x, (sh2, sc2, g2), ng, (P["mlp_w1"], 1), (P["mlp_w2"], 1), tm=PROMPT_TM // 2, hi=False, tn=1024,
             tk=1024)
    outs1 = (qk_r[:, 2 * C_HEADS * C_HD:], proj[:, 4 * C_HEADS * C_HD:])
    return x, outs0, outs1


def _sample_trunk(x, mods, P, cache, past_len):
    b, ts_, d = x.shape
    m = b * ts_
    x = x.reshape(m, d)
    pos = past_len + jnp.arange(ts_, dtype=I32)
    tab128, tab64 = _rope_tables(pos)
    tab128 = jnp.tile(tab128, (b, 1))
    tab64 = jnp.tile(tab64, (b, 1))
    page_table = cache["page_table"]
    n_pages = page_table.shape[1]
    hw = A_HEADS * A_DK

    def rows(a):
        return jnp.repeat(a, ts_, axis=0)

    def pad_t(a, n):
        return jnp.pad(a, ((0, 0), (0, n - ts_), (0, 0)))

    sh1, sc1, g1, sh2, sc2, g2 = (rows(a) for a in mods[0])
    ng = P["norm_g"][0]
    big = _norm_matmul(x, ng[0:1], sc1, sh1, P["w_big"], tm=m, tn=512, hi=True)
    small = _norm_matmul(x, ng[0:1], sc1, sh1, P["w_small"], tm=m, tn=5 * LANES, hi=True)
    small_r = _rope(small, tab64, col0=0, ngroups=5, n_kind0=4, r1=32, r2=96, tm=m)
    qb_r = _rope(big, tab128, col0=QB_COL0, ngroups=B_HEADS, n_kind0=B_HEADS, r1=64, r2=None, tm=m)
    kb_r = _rope(big, tab128, col0=KB_COL0, ngroups=B_KV, n_kind0=B_KV, r1=64, r2=None, tm=m)
    vb = big[:, VB_COL0 * LANES:]
    sm3 = small_r.reshape(b, ts_, 5 * LANES)
    qi = sm3[:, :, :IDX_HEADS * IDX_HD].reshape(b, ts_, IDX_HEADS, IDX_HD)
    qi = jnp.pad(jnp.swapaxes(qi, 1, 2), ((0, 0), (0, 0), (0, TS_PAD - ts_), (0, 0)))
    qi = qi.reshape(b, IDX_HEADS * TS_PAD, IDX_HD)
    wi = sm3[:, :, 4 * LANES + W_OFF:4 * LANES + W_OFF + IDX_HEADS]
    wi = jnp.pad(jnp.swapaxes(wi, 1, 2), ((0, 0), (0, 0), (0, TS_PAD - ts_))).reshape(b, IDX_HEADS * TS_PAD, 1)
    ki_r = sm3[:, :, 4 * LANES:4 * LANES + IDX_HD]
    pool_ki_t = jnp.swapaxes(cache["ab_kidx"], 2, 3)
    sc_past, sc_new = _idx_scores_s(page_table, qi, wi, pool_ki_t, pad_t(ki_r, PAGE), layer=0, t_new=ts_,
                                    n_pg=min(8, n_pages))
    scores = jnp.concatenate([sc_past, sc_new], axis=2)
    ncols = scores.shape[2]
    k_sel = min(TOPK, (past_len + ts_) // 4)
    chunk = min(d for d in range(1, ncols // LANES + 1) if (ncols // LANES) % d == 0 and d * LANES >= k_sel) * LANES
    thr, need, tie = _topk_thr(scores.reshape(b * TS_PAD, ncols), k=k_sel, cw=chunk, causal=False, tr=64)
    tie_b = jnp.max(tie.reshape(b, TS_PAD)[:, :ts_], axis=1)
    hg = B_HEADS // B_KV
    q4 = qb_r.reshape(b, ts_, B_KV, hg, B_HD)
    q4 = jnp.pad(jnp.transpose(q4, (0, 2, 3, 1, 4)), ((0, 0), (0, 0), (0, 0), (0, TS_PAD - ts_), (0, 0)))
    q4 = q4.reshape(b, B_KV, hg * TS_PAD, B_HD)
    kvw = B_KV * B_HD
    pool_k2 = cache["ab_k"].reshape(cache["ab_k"].shape[:2] + (PAGE * B_KV, B_HD))
    pool_v2 = cache["ab_v"].reshape(cache["ab_v"].shape[:2] + (PAGE * B_KV, B_HD))
    ob = _dsa_attn_s(page_table, tie_b, q4, pool_k2, pool_v2, pad_t(kb_r.reshape(b, ts_, kvw), PAGE),
                     pad_t(vb.reshape(b, ts_, kvw), PAGE), sc_past, sc_new, thr.reshape(b, TS_PAD, 1),
                     need.reshape(b, TS_PAD, 1), layer=0, n_pg=min(8, n_pages))
    ob = ob.reshape(b, B_KV, hg, TS_PAD, B_HD)[:, :, :, :ts_]
    ob = jnp.transpose(ob, (0, 3, 1, 2, 4)).reshape(m, B_HEADS * B_HD)
    oa, s_new = _gdn(pad_t(big.reshape(b, ts_, -1), TS_PAD), pad_t(sm3, TS_PAD), cache["ab_conv"][0], P["ab_conv_w"],
                     P["ab_A_log"], P["ab_dt_bias"], P["ab_norm_g"], cache["ab_delta"][0],
                     c=TS_PAD, t_valid=ts_, gate_group=4, gate_off=GATE_OFF)
    mix = jnp.concatenate([oa[:, :ts_].reshape(m, hw), ob], axis=1)
    x = _matmul_norm_res(mix, P["ab_w_out"], ng[1:2], g1, x, tm=m, tk=512, hi=True)
    x = _mlp(x, (sh2, sc2, g2), ng, (P["mlp_w1"], 0), (P["mlp_w2"], 0), tm=m, hi=True)
    conv_in = jnp.concatenate([cache["ab_conv"][0], big.reshape(b, ts_, -1)[:, :, :3 * hw]], axis=1)
    outs0 = (kb_r, vb, ki_r, s_new, conv_in[:, ts_:])
    sh1, sc1, g1, sh2, sc2, g2 = (rows(a) for a in mods[1])
    ng = P["norm_g"][1]
    lam_init = 0.8 - 0.6 * math.exp(-0.3 * 1)
    proj = _norm_matmul(x, ng[0:1], sc1, sh1, P["c_w_in"], tm=m, tn=512, hi=True)
    qk_r = _rope(proj, tab64, col0=0, ngroups=2 * C_HEADS, n_kind0=2 * C_HEADS, r1=32, r2=96, tm=m)
    cw = 2 * C_HEADS * C_HD
    q3 = qk_r[:, :cw].reshape(b, ts_, cw) * C_HD ** -0.5
    k3 = qk_r[:, cw:].reshape(b, ts_, cw)
    v3 = proj[:, 2 * cw:].reshape(b, ts_, cw)
    lane_hp = jnp.arange(cw, dtype=I32) // C_HD
    want = (2 * jnp.arange(C_HEADS, dtype=I32)[:, None] + jnp.arange(2, dtype=I32)[None, :])
    sel = (lane_hp[None, None, :] == want[:, :, None]).astype(F32)
    wt = (q3[:, None, None, :, :] * sel[None, :, :, None, :]).reshape(b, 2 * C_HEADS * ts_, cw).astype(BF16)
    ck = cache["c_k"]
    pool_ckt = jnp.transpose(ck, (0, 1, 3, 4, 5, 2)).reshape(ck.shape[:2] + (cw, PAGE))
    pool_cv2 = cache["c_v"].reshape(cache["c_v"].shape[:2] + (PAGE * C_HEADS, 2 * C_HD))
    o = _diff_attn_s(page_table, wt, pool_ckt, pool_cv2, pad_t(k3, TS_PAD), pad_t(v3, TS_PAD), P["c_lambda"],
                     P["c_norm_g"], layer=0, t_new=ts_, lam_init=lam_init, n_pg=min(8, n_pages))
    x = _matmul_norm_res(o.reshape(m, cw), P["c_w_out"], ng[1:2], g1, x, tm=m, tk=512, hi=True)
    x = _mlp(x, (sh2, sc2, g2), ng, (P["mlp_w1"], 1), (P["mlp_w2"], 1), tm=m, hi=True)
    outs1 = (qk_r[:, cw:], proj[:, 2 * cw:])
    return x.reshape(b, ts_, d), outs0, outs1


def kernel(x_prompt, x_sample, cache_ab_k, cache_ab_v, cache_ab_kidx, state_ab_delta, state_ab_conv, cache_c_k,
           cache_c_v, page_table, c_prompt, c_sample, ada_w, ada_b, norm_g, mlp_w1, mlp_w2, ab_w_in, ab_w_out,
           ab_conv_w, ab_A_log, ab_dt_bias, ab_norm_g, c_w_in, c_w_out, c_lambda, c_norm_g):
    bp, t, d = x_prompt.shape
    bs, ts_, _ = x_sample.shape
    assert bp == 1
    past_len = page_table.shape[1] * PAGE
    w_big, w_small = _split_ab_weights(ab_w_in[0])
    P = {"norm_g": norm_g, "mlp_w1": mlp_w1, "mlp_w2": mlp_w2, "w_big": w_big, "w_small": w_small,
         "ab_w_out": ab_w_out[0], "ab_conv_w": ab_conv_w[0], "ab_A_log": ab_A_log[0], "ab_dt_bias": ab_dt_bias[0],
         "ab_norm_g": ab_norm_g[0], "c_w_in": c_w_in[0], "c_w_out": c_w_out[0], "c_lambda": c_lambda[0],
         "c_norm_g": c_norm_g[0]}
    n_seq = bp + bs
    mc = -(-n_seq // 8) * 8
    c_all = jnp.pad(jnp.concatenate([c_prompt, c_sample], axis=0), ((0, mc - n_seq), (0, 0)))
    mod = _ada(c_all, ada_w, ada_b)
    mods_p = [tuple(mod[i, 0:bp, n * d:(n + 1) * d] for n in range(6)) for i in range(2)]
    mods_s = [tuple(mod[i, bp:n_seq, n * d:(n + 1) * d] for n in range(6)) for i in range(2)]

    y_p, p0, p1 = _prompt_trunk(x_prompt[0], mods_p, P)
    cache = {"ab_k": cache_ab_k, "ab_v": cache_ab_v, "ab_kidx": cache_ab_kidx, "ab_delta": state_ab_delta,
             "ab_conv": state_ab_conv, "c_k": cache_c_k, "c_v": cache_c_v, "page_table": page_table}
    y_s, s0, s1 = _sample_trunk(x_sample, mods_s, P, cache, past_len)

    return (y_p[None], y_s,
            p0[0].reshape(1, 1, t, B_KV, B_HD), p0[1].reshape(1, 1, t, B_KV, B_HD), p0[2].reshape(1, 1, t, IDX_HD),
            p0[3][None], p0[4].reshape(1, 1, A_CONV - 1, -1),
            p1[0].reshape(1, 1, t, C_HEADS, 2, C_HD), p1[1].reshape(1, 1, t, C_HEADS, 2 * C_HD),
            s0[0].reshape(1, bs, ts_, B_KV, B_HD), s0[1].reshape(1, bs, ts_, B_KV, B_HD),
            s0[2].reshape(1, bs, ts_, IDX_HD), s0[3][None], s0[4][None],
            s1[0].reshape(1, bs, ts_, C_HEADS, 2, C_HD), s1[1].reshape(1, bs, ts_, C_HEADS, 2 * C_HD))
```

```python
import functools
import math

import jax
import jax.numpy as jnp
from jax import lax
from jax.experimental import pallas as pl
from jax.experimental.pallas import tpu as pltpu

F32 = jnp.float32
BF16 = jnp.bfloat16
I32 = jnp.int32

EPS = 1e-6
NEG_BIG = -1e30
ROPE_THETA = 10000.0
PAGE = 128
A_HEADS = 8
A_DK = 128
A_CONV = 4
B_HEADS = 8
B_KV = 2
B_HD = 128
IDX_HEADS = 8
IDX_HD = 64
TOPK = 256
C_HEADS = 16
C_HD = 64
LANES = 128

NN = (((1,), (0,)), ((), ()))
NT = (((1,), (1,)), ((), ()))
TN = (((0,), (0,)), ((), ()))
BNN = (((2,), (1,)), ((0,), (0,)))
BNT = (((2,), (2,)), ((0,), (0,)))
INT_MIN = -2147483648
KEY_NEG_INF = -2139095041


def _cp(dims, vmem_mb=None):
    kw = dict(dimension_semantics=dims)
    if vmem_mb is not None:
        kw["vmem_limit_bytes"] = vmem_mb << 20
    return pltpu.CompilerParams(**kw)


def _dot(a, b, dims=NN):
    return lax.dot_general(a, b, dims, preferred_element_type=F32)


def _split2(a):
    hi = a.astype(BF16)
    return hi, (a - hi.astype(F32)).astype(BF16)


def _split3(a):
    hi = a.astype(BF16)
    r = a - hi.astype(F32)
    mid = r.astype(BF16)
    return hi, mid, (r - mid.astype(F32)).astype(BF16)


def _dot1(a, b, dims=NN):
    return _dot(a.astype(BF16), b.astype(BF16), dims)


def _dot3(a, b, dims=NN):
    ah, al = _split2(a)
    bh, bl = _split2(b)
    return _dot(ah, bh, dims) + (_dot(ah, bl, dims) + _dot(al, bh, dims))


def _dot_exact_left(ones_bf16, b, dims=NN):
    b0, b1, b2 = _split3(b)
    return _dot(ones_bf16, b0, dims) + (_dot(ones_bf16, b1, dims) + _dot(ones_bf16, b2, dims))


def _sigmoid(x):
    return 1.0 / (1.0 + jnp.exp(-x))


def _silu(x):
    return x * _sigmoid(x)


def _ada_kernel(c_ref, w_ref, b_ref, o_ref):
    o_ref[0] = _dot3(_silu(c_ref[...]), w_ref[0]) + b_ref[0]


def _ada(c_all, ada_w, ada_b, tn=512):
    nl, d, n = ada_w.shape
    mc = c_all.shape[0]
    return pl.pallas_call(
        _ada_kernel,
        grid=(nl, n // tn),
        in_specs=[pl.BlockSpec((mc, d), lambda l, j: (0, 0)),
                  pl.BlockSpec((1, d, tn), lambda l, j: (l, 0, j)),
                  pl.BlockSpec((1, 1, tn), lambda l, j: (l, 0, j))],
        out_specs=pl.BlockSpec((1, mc, tn), lambda l, j: (l, 0, j)),
        out_shape=jax.ShapeDtypeStruct((nl, mc, n), F32),
        compiler_params=_cp(("arbitrary", "arbitrary"), 40),
        name="ada_mod",
    )(c_all, ada_w, ada_b.reshape(nl, 1, n))


def _norm_mm_kernel(x_ref, g_ref, sc_ref, sh_ref, w_ref, o_ref, *scratch, hi, relu2):
    hh_ref = scratch[0]

    @pl.when(pl.program_id(1) == 0)
    def _():
        x = x_ref[...]
        y = x * lax.rsqrt(jnp.mean(x * x, axis=-1, keepdims=True) + EPS)
        h = (y * g_ref[...]) * (1.0 + sc_ref[...]) + sh_ref[...]
        hh = h.astype(BF16)
        hh_ref[...] = hh
        if hi:
            scratch[1][...] = (h - hh.astype(F32)).astype(BF16)

    w = w_ref[0]
    wh = w.astype(BF16)
    acc = _dot(hh_ref[...], wh)
    if hi:
        wl = (w - wh.astype(F32)).astype(BF16)
        acc = acc + (_dot(hh_ref[...], wl) + _dot(scratch[1][...], wh))
    if relu2:
        acc = jnp.square(jnp.maximum(acc, 0.0))
    o_ref[...] = acc.astype(o_ref.dtype)


def _layer_weight(w):
    return w if isinstance(w, tuple) else (w[None], 0)


def _norm_matmul(x, g, sc, sh, w, *, tm, tn, hi=False, relu2=False, out_dtype=F32, vmem_mb=48):
    m, d = x.shape
    tm = min(tm, m)
    w, layer = _layer_weight(w)
    n = w.shape[2]
    per_row = sc.shape[0] != 1
    mod_spec = pl.BlockSpec((tm, d), lambda i, j: (i, 0)) if per_row else pl.BlockSpec((1, d), lambda i, j: (0, 0))
    scratch = [pltpu.VMEM((tm, d), BF16)] + ([pltpu.VMEM((tm, d), BF16)] if hi else [])
    return pl.pallas_call(
        functools.partial(_norm_mm_kernel, hi=hi, relu2=relu2),
        grid=(m // tm, n // tn),
        in_specs=[pl.BlockSpec((tm, d), lambda i, j: (i, 0), pipeline_mode=pl.Buffered(1)),
                  pl.BlockSpec((1, d), lambda i, j: (0, 0)),
                  mod_spec, mod_spec,
                  pl.BlockSpec((1, d, tn), lambda i, j: (layer, 0, j))],
        out_specs=pl.BlockSpec((tm, tn), lambda i, j: (i, j)),
        out_shape=jax.ShapeDtypeStruct((m, n), out_dtype),
        scratch_shapes=scratch,
        compiler_params=_cp(("arbitrary", "arbitrary"), vmem_mb),
        name="norm_matmul",
    )(x, g, sc, sh, w)


def _mm_norm_res_kernel(a_ref, w_ref, ng_ref, gate_ref, res_ref, o_ref, *, hi):
    k = pl.program_id(1)

    @pl.when(k == 0)
    def _():
        o_ref[...] = jnp.zeros(o_ref.shape, F32)

    if hi:
        o_ref[...] += _dot3(a_ref[...].astype(F32), w_ref[0])
    else:
        o_ref[...] += _dot1(a_ref[...], w_ref[0])

    @pl.when(k == pl.num_programs(1) - 1)
    def _():
        m = o_ref[...]
        y = m * lax.rsqrt(jnp.mean(m * m, axis=-1, keepdims=True) + EPS)
        o_ref[...] = res_ref[...] + gate_ref[...] * (y * ng_ref[...])


def _matmul_norm_res(a, w, ng, gate, res, *, tm, tk, hi=False, vmem_mb=48):
    m, kdim = a.shape
    w, layer = _layer_weight(w)
    n = w.shape[2]
    per_row = gate.shape[0] != 1
    gate_spec = pl.BlockSpec((tm, n), lambda i, k: (i, 0)) if per_row else pl.BlockSpec((1, n), lambda i, k: (0, 0))
    return pl.pallas_call(
        functools.partial(_mm_norm_res_kernel, hi=hi),
        grid=(m // tm, kdim // tk),
        in_specs=[pl.BlockSpec((tm, tk), lambda i, k: (i, k)),
                  pl.BlockSpec((1, tk, n), lambda i, k: (layer, k, 0)),
                  pl.BlockSpec((1, n), lambda i, k: (0, 0)),
                  gate_spec,
                  pl.BlockSpec((tm, n), lambda i, k: (i, 0), pipeline_mode=pl.Buffered(1))],
        out_specs=pl.BlockSpec((tm, n), lambda i, k: (i, 0)),
        out_shape=jax.ShapeDtypeStruct((m, n), F32),
        compiler_params=_cp(("arbitrary", "arbitrary"), vmem_mb),
        name="matmul_norm_res",
    )(a, w, ng, gate, res)


def _rope_kernel(x_ref, tab_ref, o_ref, *, r1, r2, gw):
    for k in range(gw):
        x = x_ref[:, k * LANES:(k + 1) * LANES]
        out = x * tab_ref[:, 0:LANES] + pltpu.roll(x, r1, 1) * tab_ref[:, LANES:2 * LANES]
        if r2 is not None:
            out = out + pltpu.roll(x, r2, 1) * tab_ref[:, 2 * LANES:3 * LANES]
        o_ref[:, k * LANES:(k + 1) * LANES] = out


def _rope(x, tab, *, col0, ngroups, n_kind0, r1, r2, tm, gw=1):
    t = x.shape[0]
    assert col0 % gw == 0 and ngroups % gw == 0 and n_kind0 % gw == 0
    return pl.pallas_call(
        functools.partial(_rope_kernel, r1=r1, r2=r2, gw=gw),
        grid=(t // tm, ngroups // gw),
        in_specs=[pl.BlockSpec((tm, gw * LANES), lambda i, j: (i, col0 // gw + j)),
                  pl.BlockSpec((tm, 3 * LANES), lambda i, j: (i, jnp.where(j * gw >= n_kind0, 1, 0)))],
        out_specs=pl.BlockSpec((tm, gw * LANES), lambda i, j: (i, j)),
        out_shape=jax.ShapeDtypeStruct((t, ngroups * LANES), F32),
        compiler_params=_cp(("arbitrary", "arbitrary")),
        name="rope",
    )(x, tab)


def _rope_tables(pos):
    p = pos.astype(F32)[:, None]

    def cs(half):
        inv = ROPE_THETA ** (-jnp.arange(half, dtype=F32) / half)
        ang = p * inv[None, :]
        return jnp.cos(ang), jnp.sin(ang)

    c, s = cs(64)
    tab128 = jnp.concatenate([c, c, -s, s, jnp.zeros_like(c), jnp.zeros_like(c)], axis=1)
    c, s = cs(32)
    z = jnp.zeros_like(c)
    one64 = jnp.ones((pos.shape[0], 64), F32)
    z64 = jnp.zeros((pos.shape[0], 64), F32)
    kind0 = jnp.concatenate([c, c, c, c, z, s, z, s, -s, z, -s, z], axis=1)
    kind1 = jnp.concatenate([c, c, one64, z, s, z64, -s, z, z64], axis=1)
    return tab128, jnp.concatenate([kind0, kind1], axis=1)


def _gdn_kernel(qkv_ref, z_ref, gate_ref, buf_ref, cw_ref, alog_ref, dtb_ref, ng_ref, s0_ref,
                o_ref, sout_ref, xbuf_ref, s_ref, *, c, t_valid, gate_off):
    ci = pl.program_id(1)
    nh, dk = A_HEADS, A_DK
    hw = nh * dk

    @pl.when(ci == 0)
    def _():
        s_ref[...] = s0_ref[0]
        xbuf_ref[0:8, :] = jnp.zeros((8, 3 * hw), F32)
        xbuf_ref[8 - (A_CONV - 1):8, :] = buf_ref[0]

    xbuf_ref[8:8 + c, :] = qkv_ref[0]
    y = xbuf_ref[5:5 + c, :] * cw_ref[0:1, :]
    for j in range(1, A_CONV):
        y = y + xbuf_ref[5 + j:5 + j + c, :] * cw_ref[j:j + 1, :]
    tail = xbuf_ref[8 + c - 3:8 + c, :]
    xbuf_ref[5:8, :] = tail
    y = _silu(y)

    gt = gate_ref[0]
    ba = gt[:, gate_off:gate_off + nh]
    aa = gt[:, gate_off + nh:gate_off + 2 * nh]
    beta = _sigmoid(ba)
    xs = aa + dtb_ref[...]
    softplus = jnp.maximum(xs, 0.0) + jnp.log1p(jnp.exp(-jnp.abs(xs)))
    g = -jnp.exp(alog_ref[...]) * softplus
    row = lax.broadcasted_iota(I32, (c, c), 0)
    col = lax.broadcasted_iota(I32, (c, c), 1)
    if t_valid < c:
        valid = lax.broadcasted_iota(I32, (c, nh), 0) < t_valid
        beta = jnp.where(valid, beta, 0.0)
        g = jnp.where(valid, g, 0.0)
    incl = (row >= col)[None]
    strict = (row > col)[None]
    tri = jnp.where(row >= col, 1.0, 0.0).astype(BF16)
    gc = _dot_exact_left(tri, g)
    eye_h = jnp.where(lax.broadcasted_iota(I32, (nh, nh), 0) == lax.broadcasted_iota(I32, (nh, nh), 1),
                      1.0, 0.0).astype(BF16)
    gct = _dot_exact_left(eye_h, gc, NT)
    eye_c = jnp.where(row == col, 1.0, 0.0)[None]

    def heads(a, off):
        return jnp.stack([a[:, off + h * dk:off + (h + 1) * dk] for h in range(nh)], axis=0)

    def cols(a):
        return jnp.stack([a[:, h:h + 1] for h in range(nh)], axis=0)

    q3 = heads(y, 0)
    k3 = heads(y, hw)
    v3 = heads(y, 2 * hw)
    q3 = q3 * lax.rsqrt(jnp.sum(q3 * q3, axis=-1, keepdims=True) + 1e-6)
    k3 = k3 * lax.rsqrt(jnp.sum(k3 * k3, axis=-1, keepdims=True) + 1e-6)
    b3 = cols(beta)
    gch = cols(gc)
    gct3 = jnp.stack([gct[h:h + 1, :] for h in range(nh)], axis=0)
    decay = jnp.where(incl, jnp.exp(jnp.minimum(gch - gct3, 0.0)), 0.0)
    kb = k3 * b3
    vb = v3 * b3
    a_mat = jnp.where(strict, _dot3(kb, k3, BNT) * decay, 0.0)
    def same_block(s):
        sh = int(round(math.log2(s)))
        return (jnp.right_shift(row, sh) == jnp.right_shift(col, sh))[None]

    s_blk = min(8, c)
    pw = jnp.where(same_block(s_blk), -a_mat, 0.0)
    t_inv = eye_c + pw
    for _ in range(int(round(math.log2(s_blk))) - 1):
        pw = _dot3(pw, pw, BNN)
        t_inv = t_inv + _dot3(t_inv, pw, BNN)
    while s_blk < c:
        off = jnp.where(jnp.logical_and(same_block(2 * s_blk), jnp.logical_not(same_block(s_blk))), a_mat, 0.0)
        t_inv = t_inv - _dot3(t_inv, _dot3(off, t_inv, BNN), BNN)
        s_blk *= 2
    egc = jnp.exp(gch)
    u = _dot3(t_inv, vb, BNN)
    w = _dot3(t_inv, kb * egc, BNN)
    qs = q3 * dk ** -0.5
    qk = _dot3(qs, k3, BNT) * decay
    g_last = gch[:, c - 1:c, :]
    k_dec = k3 * jnp.exp(jnp.minimum(g_last - gch, 0.0))
    s_old = s_ref[...]
    v_new = u - _dot3(w, s_old, BNN)
    o = _dot3(qs * egc, s_old, BNN) + _dot3(qk, v_new, BNN)
    eg_last = jnp.exp(g_last)
    on = o * lax.rsqrt(jnp.mean(o * o, axis=-1, keepdims=True) + EPS) * ng_ref[...]
    for h in range(nh):
        s_ref[h] = s_old[h] * eg_last[h] + _dot3(k_dec[h], v_new[h], TN)
        o_ref[0, :, h * dk:(h + 1) * dk] = on[h] * _silu(z_ref[0, :, h * dk:(h + 1) * dk])

    @pl.when(ci == pl.num_programs(1) - 1)
    def _():
        sout_ref[0] = s_ref[...]


def _gdn(proj, small, conv_buf, conv_w, a_log, dt_bias, norm_g, s0, *, c, t_valid, gate_group, gate_off):
    b, t = proj.shape[:2]
    nh, dk = A_HEADS, A_DK
    hw = nh * dk
    return pl.pallas_call(
        functools.partial(_gdn_kernel, c=c, t_valid=t_valid, gate_off=gate_off),
        grid=(b, t // c),
        in_specs=[pl.BlockSpec((1, c, 3 * hw), lambda bi, ci: (bi, ci, 0)),
                  pl.BlockSpec((1, c, hw), lambda bi, ci: (bi, ci, 3)),
                  pl.BlockSpec((1, c, LANES), lambda bi, ci: (bi, ci, gate_group)),
                  pl.BlockSpec((1, A_CONV - 1, 3 * hw), lambda bi, ci: (bi, 0, 0)),
                  pl.BlockSpec((A_CONV, 3 * hw), lambda bi, ci: (0, 0)),
                  pl.BlockSpec((1, nh), lambda bi, ci: (0, 0)),
                  pl.BlockSpec((1, nh), lambda bi, ci: (0, 0)),
                  pl.BlockSpec((1, dk), lambda bi, ci: (0, 0)),
                  pl.BlockSpec((1, nh, dk, dk), lambda bi, ci: (bi, 0, 0, 0))],
        out_specs=[pl.BlockSpec((1, c, hw), lambda bi, ci: (bi, ci, 0)),
                   pl.BlockSpec((1, nh, dk, dk), lambda bi, ci: (bi, 0, 0, 0))],
        out_shape=[jax.ShapeDtypeStruct((b, t, hw), F32), jax.ShapeDtypeStruct((b, nh, dk, dk), F32)],
        scratch_shapes=[pltpu.VMEM((8 + c, 3 * hw), F32), pltpu.VMEM((nh, dk, dk), F32)],
        compiler_params=_cp(("arbitrary", "arbitrary"), 40),
        name="gdn",
    )(proj, proj, small, conv_buf, conv_w, a_log.reshape(1, nh), dt_bias.reshape(1, nh), norm_g.reshape(1, dk), s0)


def _idx_prep_kernel(x_ref, q_ref, k_ref):
    tm = x_ref.shape[0]
    lo_half = lax.broadcasted_iota(I32, (tm, LANES), 1) < 64

    def split(xg):
        hi = xg.astype(BF16).astype(F32)
        return hi, xg - hi

    for gq in range(IDX_HEADS // 2):
        hi, lo = split(x_ref[:, gq * LANES:(gq + 1) * LANES])
        hi_r = pltpu.roll(hi, 64, 1)
        lo_r = pltpu.roll(lo, 64, 1)
        base = 2 * gq * 256
        q_ref[:, base:base + 128] = jnp.where(lo_half, hi, hi_r).astype(BF16)
        q_ref[:, base + 128:base + 256] = jnp.where(lo_half, lo, lo_r).astype(BF16)
        q_ref[:, base + 256:base + 384] = jnp.where(lo_half, hi_r, hi).astype(BF16)
        q_ref[:, base + 384:base + 512] = jnp.where(lo_half, lo_r, lo).astype(BF16)
    hi, lo = split(x_ref[:, 4 * LANES:5 * LANES])
    kk = jnp.where(lo_half, hi, pltpu.roll(lo, 64, 1)).astype(BF16)
    k_ref[:, 0:128] = kk
    k_ref[:, 128:256] = kk


def _idx_prep(small_r, tm=512):
    t = small_r.shape[0]
    return pl.pallas_call(
        _idx_prep_kernel,
        grid=(t // tm,),
        in_specs=[pl.BlockSpec((tm, 5 * LANES), lambda i: (i, 0))],
        out_specs=[pl.BlockSpec((tm, IDX_HEADS * 256), lambda i: (i, 0)),
                   pl.BlockSpec((tm, 256), lambda i: (i, 0))],
        out_shape=[jax.ShapeDtypeStruct((t, IDX_HEADS * 256), BF16), jax.ShapeDtypeStruct((t, 256), BF16)],
        compiler_params=_cp(("arbitrary",)),
        name="idx_prep",
    )(small_r)


IDX_SCALE = IDX_HEADS ** -0.5 * IDX_HD ** -0.5
W_OFF = 64


def _idx_scores_kernel(q_ref, k_ref, w_ref, o_ref, *, tq, ts):
    i = pl.program_id(0)
    j = pl.program_id(1)
    live = j * ts <= i * tq + tq - 1

    @pl.when(live)
    def _():
        k = k_ref[...]
        wg = w_ref[...] * IDX_SCALE
        acc = jnp.zeros((tq, ts), F32)
        for h in range(IDX_HEADS):
            s = _dot(q_ref[:, h * 256:(h + 1) * 256], k, NT)
            acc = acc + wg[:, W_OFF + h:W_OFF + h + 1] * jnp.maximum(s, 0.0)
        row = i * tq + lax.broadcasted_iota(I32, (tq, ts), 0)
        col = j * ts + lax.broadcasted_iota(I32, (tq, ts), 1)
        o_ref[...] = jnp.where(col <= row, acc, -jnp.inf)

    @pl.when(jnp.logical_not(live))
    def _():
        o_ref[...] = jnp.full((tq, ts), -jnp.inf, F32)


def _idx_scores(qcat, kcat, small_r, *, tq=256, ts=512):
    t = qcat.shape[0]
    return pl.pallas_call(
        functools.partial(_idx_scores_kernel, tq=tq, ts=ts),
        grid=(t // tq, t // ts),
        in_specs=[pl.BlockSpec((tq, IDX_HEADS * 256), lambda i, j: (i, 0)),
                  pl.BlockSpec((ts, 256), lambda i, j: (j, 0)),
                  pl.BlockSpec((tq, LANES), lambda i, j: (i, 4))],
        out_specs=pl.BlockSpec((tq, ts), lambda i, j: (i, j)),
        out_shape=jax.ShapeDtypeStruct((t, t), F32),
        compiler_params=_cp(("arbitrary", "arbitrary")),
        name="idx_scores",
    )(qcat, kcat, small_r)


def _thr_kernel(s_ref, thr_ref, need_ref, tie_ref, key_ref, *, k, cw, causal):
    tr, s_cols = s_ref.shape
    nchunks = ((pl.program_id(0) + 1) * tr + cw - 1) // cw if causal else s_cols // cw
    kf = float(k)

    def fill(ci, carry):
        off = pl.multiple_of(ci * cw, cw)
        bits = pltpu.bitcast(s_ref[:, pl.ds(off, cw)] + 0.0, I32)
        key_ref[:, pl.ds(off, cw)] = jnp.where(bits < 0, bits ^ 0x7FFFFFFF, bits)
        return carry

    lax.fori_loop(0, nchunks, fill, 0)

    def count(cand, strict):
        def cbody(ci, acc):
            off = pl.multiple_of(ci * cw, cw)
            blk = key_ref[:, pl.ds(off, cw)]
            for t in range(cw // LANES):
                kk = blk[:, t * LANES:(t + 1) * LANES]
                acc = acc + jnp.where(kk > cand if strict else kk >= cand, 1.0, 0.0)
            return acc

        acc = lax.fori_loop(0, nchunks, cbody, jnp.zeros((tr, LANES), F32))
        return jnp.sum(acc, axis=-1, keepdims=True)

    cnt0 = count(0, False)
    p0 = jnp.where(cnt0 >= kf, 0, INT_MIN).astype(I32)
    visited = (nchunks * cw).astype(F32) if causal else float(s_cols)
    cnt_p0 = jnp.where(cnt0 >= kf, cnt0, visited)

    def cond(state):
        b, _, cnt_p = state
        return jnp.logical_and(b < 31, jnp.max(cnt_p) > kf)

    def body(state):
        b, p, cnt_p = state
        cand = p | jnp.left_shift(jnp.int32(1), 30 - b)
        cnt = count(cand, False)
        take = cnt >= kf
        return b + 1, jnp.where(take, cand, p), jnp.where(take, cnt, cnt_p)

    _, p, _ = lax.while_loop(cond, body, (jnp.int32(0), p0, cnt_p0))
    p = jnp.maximum(p, KEY_NEG_INF)
    thr_ref[...] = pltpu.bitcast(jnp.where(p < 0, p ^ 0x7FFFFFFF, p), F32)
    need_ref[...] = kf - count(p, True)
    tie_ref[...] = jnp.where(jnp.logical_and(count(p, False) > kf, p != KEY_NEG_INF), 1, 0).astype(I32)


def _topk_thr(scores, *, k, cw, causal, tr):
    r, s = scores.shape
    tr = min(tr, r)
    assert cw >= k and s % cw == 0
    return pl.pallas_call(
        functools.partial(_thr_kernel, k=k, cw=cw, causal=causal),
        grid=(r // tr,),
        in_specs=[pl.BlockSpec((tr, s), lambda i: (i, 0))],
        out_specs=[pl.BlockSpec((tr, 1), lambda i: (i, 0))] * 3,
        out_shape=[jax.ShapeDtypeStruct((r, 1), F32), jax.ShapeDtypeStruct((r, 1), F32),
                   jax.ShapeDtypeStruct((r, 1), I32)],
        scratch_shapes=[pltpu.VMEM((tr, s), I32)],
        compiler_params=_cp(("arbitrary",), 32),
        name="topk_thr",
    )(scores)


def _select_mask(sc, thr, need, causal, eq_before, tie):
    if not tie:
        return jnp.logical_and(sc >= thr, sc > -jnp.inf if causal is None else causal), None
    n = sc.shape[1]
    cs = min(n, 512)
    upper = jnp.where(lax.broadcasted_iota(I32, (cs, cs), 0) < lax.broadcasted_iota(I32, (cs, cs), 1),
                      1.0, 0.0).astype(BF16)
    masks = []
    for c0 in range(0, n, cs):
        scc = sc[:, c0:c0 + cs]
        eq = scc == thr
        eqf = jnp.where(eq, 1.0, 0.0)
        rank = eq_before + _dot(eqf.astype(BF16), upper)
        sel = jnp.logical_or(scc > thr, jnp.logical_and(eq, rank < need))
        masks.append(jnp.logical_and(sel, scc > -jnp.inf if causal is None else causal[:, c0:c0 + cs]))
        eq_before = eq_before + jnp.sum(eqf, axis=-1, keepdims=True)
    if len(masks) == 1:
        return masks[0], eq_before
    return jnp.concatenate([jnp.where(m, 1.0, 0.0) for m in masks], axis=1) > 0.5, eq_before


def _flash_update(s, m_ref, acc_ref, r0, rows, vblk):
    ts = s.shape[1]
    m_old = m_ref[r0:r0 + rows, :]
    m_new = jnp.maximum(m_old, jnp.max(s, axis=-1, keepdims=True))
    p = jnp.exp(s - jnp.tile(m_new, (1, ts // LANES)))
    alpha = jnp.exp(m_old - m_new)
    acc_ref[r0:r0 + rows, :] = (jnp.tile(alpha, (1, acc_ref.shape[1] // LANES)) * acc_ref[r0:r0 + rows, :]
                                + _dot(p.astype(BF16), vblk))
    m_ref[r0:r0 + rows, :] = m_new


def _dsa_attn_kernel(tie_ref, q_ref, k_ref, v_ref, sc_ref, thr_ref, need_ref, o_ref,
                     kb_ref, vb_ref, qs_ref, m_ref, acc_ref, eq_ref, *, tq, ts):
    i = pl.program_id(1)
    hg = B_HEADS // B_KV
    d = B_HD

    @pl.when(i == 0)
    def _():
        kb_ref[...] = k_ref[...].astype(BF16)
        vb_ref[:, 0:d] = v_ref[...].astype(BF16)
        vb_ref[:, d:2 * d] = jnp.ones((vb_ref.shape[0], d), BF16)

    for h in range(hg):
        qs_ref[h * tq:(h + 1) * tq, :] = (q_ref[:, h * d:(h + 1) * d] * d ** -0.5).astype(BF16)
    m_ref[...] = jnp.full(m_ref.shape, NEG_BIG, F32)
    acc_ref[...] = jnp.zeros(acc_ref.shape, F32)
    eq_ref[...] = jnp.zeros(eq_ref.shape, F32)
    thr = thr_ref[...]
    need = need_ref[...]
    nblk = (i * tq + tq + ts - 1) // ts

    def run(tie):
        def body(jb, carry):
            off = pl.multiple_of(jb * ts, ts)
            sc = sc_ref[:, pl.ds(off, ts)]
            row = i * tq + lax.broadcasted_iota(I32, (tq, ts), 0)
            col = off + lax.broadcasted_iota(I32, (tq, ts), 1)
            mask, eq_new = _select_mask(sc, thr, need, col <= row, eq_ref[...], tie)
            if tie:
                eq_ref[...] = eq_new
            kblk = kb_ref[pl.ds(off, ts), :]
            vblk = vb_ref[pl.ds(off, ts), :]
            for h in range(hg):
                s = jnp.where(mask, _dot(qs_ref[h * tq:(h + 1) * tq, :], kblk, NT), NEG_BIG)
                _flash_update(s, m_ref, acc_ref, h * tq, tq, vblk)
            return carry

        lax.fori_loop(0, nblk, body, 0)

    has_tie = tie_ref[i] > 0

    @pl.when(has_tie)
    def _():
        run(True)

    @pl.when(jnp.logical_not(has_tie))
    def _():
        run(False)

    for h in range(hg):
        o_ref[:, h * d:(h + 1) * d] = acc_ref[h * tq:(h + 1) * tq, 0:d] / acc_ref[h * tq:(h + 1) * tq, d:2 * d]


def _dsa_attn(tie_blk, q_r, k_r, v_src, v_col0, scores, thr, need, *, tq, ts=512):
    t = q_r.shape[0]
    hg = B_HEADS // B_KV
    once = dict(pipeline_mode=pl.Buffered(1))
    grid_spec = pltpu.PrefetchScalarGridSpec(
        num_scalar_prefetch=1,
        grid=(B_KV, t // tq),
        in_specs=[pl.BlockSpec((tq, hg * B_HD), lambda g, i, tie: (i, g)),
                  pl.BlockSpec((t, B_HD), lambda g, i, tie: (0, g), **once),
                  pl.BlockSpec((t, B_HD), lambda g, i, tie: (0, v_col0 + g), **once),
                  pl.BlockSpec((tq, t), lambda g, i, tie: (i, 0)),
                  pl.BlockSpec((tq, 1), lambda g, i, tie: (i, 0)),
                  pl.BlockSpec((tq, 1), lambda g, i, tie: (i, 0))],
        out_specs=pl.BlockSpec((tq, hg * B_HD), lambda g, i, tie: (i, g)),
        scratch_shapes=[pltpu.VMEM((t, B_HD), BF16), pltpu.VMEM((t, 2 * B_HD), BF16),
                        pltpu.VMEM((hg * tq, B_HD), BF16), pltpu.VMEM((hg * tq, LANES), F32),
                        pltpu.VMEM((hg * tq, 2 * B_HD), F32), pltpu.VMEM((tq, 1), F32)])
    return pl.pallas_call(
        functools.partial(_dsa_attn_kernel, tq=tq, ts=ts),
        grid_spec=grid_spec,
        out_shape=jax.ShapeDtypeStruct((t, B_HEADS * B_HD), F32),
        compiler_params=_cp(("arbitrary", "arbitrary"), 56),
        name="dsa_attn",
    )(tie_blk, q_r, k_r, v_src, scores, thr, need)


def _lambda_value(lam_ref, lam_init):
    lp = lam_ref[...]
    a = jnp.sum(lp[0:1, :] * lp[1:2, :], axis=-1, keepdims=True)
    b = jnp.sum(lp[2:3, :] * lp[3:4, :], axis=-1, keepdims=True)
    return jnp.exp(a) - jnp.exp(b) + lam_init


def _diff_attn_kernel(q_ref, k_ref, v_ref, lam_ref, ng_ref, o_ref, kb_ref, vb_ref, qs_ref, m_ref, acc_ref,
                      *, tq, ts, rsub, lam_init):
    i = pl.program_id(1)
    d2 = 2 * C_HD
    rows = 2 * tq

    @pl.when(i == 0)
    def _():
        kb_ref[...] = k_ref[...].astype(BF16)
        vb_ref[:, 0:d2] = v_ref[...].astype(BF16)
        vb_ref[:, d2:2 * d2] = jnp.ones((vb_ref.shape[0], d2), BF16)

    q = q_ref[...] * C_HD ** -0.5
    first = lax.broadcasted_iota(I32, (tq, d2), 1) < C_HD
    qs_ref[0:tq, :] = jnp.where(first, q, 0.0).astype(BF16)
    qs_ref[tq:rows, :] = jnp.where(first, 0.0, q).astype(BF16)
    m_ref[...] = jnp.full(m_ref.shape, NEG_BIG, F32)
    acc_ref[...] = jnp.zeros(acc_ref.shape, F32)

    def step(jb, diag_col0):
        off = pl.multiple_of(jb * ts, ts)
        kblk = kb_ref[pl.ds(off, ts), :]
        vblk = vb_ref[pl.ds(off, ts), :]
        for r0 in range(0, rows, rsub):
            q0 = r0 % tq
            if diag_col0 is not None and q0 + rsub - 1 < diag_col0:
                continue
            s = _dot(qs_ref[r0:r0 + rsub, :], kblk, NT)
            if diag_col0 is not None and q0 < diag_col0 + ts - 1:
                qrow = q0 + lax.broadcasted_iota(I32, (rsub, ts), 0)
                s = jnp.where(diag_col0 + lax.broadcasted_iota(I32, (rsub, ts), 1) <= qrow, s, NEG_BIG)
            _flash_update(s, m_ref, acc_ref, r0, rsub, vblk)

    def body(jb, carry):
        step(jb, None)
        return carry

    nfull = (i * tq) // ts
    lax.fori_loop(0, nfull, body, 0)
    for dj in range(tq // ts):
        step(nfull + dj, dj * ts)

    lam = _lambda_value(lam_ref, lam_init)
    o = (acc_ref[0:tq, 0:d2] / acc_ref[0:tq, d2:2 * d2]
         - lam * (acc_ref[tq:rows, 0:d2] / acc_ref[tq:rows, d2:2 * d2]))
    on = o * lax.rsqrt(jnp.mean(o * o, axis=-1, keepdims=True) + EPS) * ng_ref[...]
    o_ref[...] = on * (1.0 - lam_init)


def _diff_attn(qk_r, v_src, v_col0, c_lambda, c_norm_g, *, lam_init, tq=2048, ts=1024, rsub=512):
    t = qk_r.shape[0]
    tq = min(tq, t)
    d2 = 2 * C_HD
    once = dict(pipeline_mode=pl.Buffered(1))
    return pl.pallas_call(
        functools.partial(_diff_attn_kernel, tq=tq, ts=ts, rsub=rsub, lam_init=lam_init),
        grid=(C_HEADS, t // tq),
        in_specs=[pl.BlockSpec((tq, d2), lambda h, i: (i, h)),
                  pl.BlockSpec((t, d2), lambda h, i: (0, C_HEADS + h), **once),
                  pl.BlockSpec((t, d2), lambda h, i: (0, v_col0 + h), **once),
                  pl.BlockSpec((4, C_HD), lambda h, i: (0, 0)),
                  pl.BlockSpec((1, d2), lambda h, i: (0, 0))],
        out_specs=pl.BlockSpec((tq, d2), lambda h, i: (i, h)),
        out_shape=jax.ShapeDtypeStruct((t, C_HEADS * d2), F32),
        scratch_shapes=[pltpu.VMEM((t, d2), BF16), pltpu.VMEM((t, 2 * d2), BF16), pltpu.VMEM((2 * tq, d2), BF16),
                        pltpu.VMEM((2 * tq, LANES), F32), pltpu.VMEM((2 * tq, 2 * d2), F32)],
        compiler_params=_cp(("arbitrary", "arbitrary"), 48),
        name="diff_attn",
    )(qk_r, qk_r, v_src, c_lambda, c_norm_g.reshape(1, d2))


TS_PAD = 8


def _page_specs(n_pg, blk, layer, n_prefetch):
    def spec(k):
        if n_prefetch == 1:
            return pl.BlockSpec((1, 1) + blk, lambda bi, p, pt: (layer, pt[bi, p * n_pg + k], 0, 0))
        return pl.BlockSpec((1, 1) + blk, lambda bi, p, pt, tie: (layer, pt[bi, p * n_pg + k], 0, 0))

    return [spec(k) for k in range(n_pg)]


def _idx_scores_s_kernel(pt_ref, q_ref, w_ref, *refs, n_pg, t_new):
    pools = refs[:n_pg]
    new_ref, o_ref, onew_ref = refs[n_pg:]
    q = q_ref[0]
    w = w_ref[0] * IDX_SCALE

    def weigh(s):
        acc = jnp.zeros((TS_PAD, s.shape[1]), F32)
        for h in range(IDX_HEADS):
            acc = acc + w[h * TS_PAD:(h + 1) * TS_PAD, :] * jnp.maximum(s[h * TS_PAD:(h + 1) * TS_PAD, :], 0.0)
        return acc

    for k in range(n_pg):
        o_ref[0, :, k * PAGE:(k + 1) * PAGE] = weigh(_dot3(q, pools[k][0, 0]))

    @pl.when(pl.program_id(1) == pl.num_programs(1) - 1)
    def _():
        tq = lax.broadcasted_iota(I32, (TS_PAD, PAGE), 0)
        tk = lax.broadcasted_iota(I32, (TS_PAD, PAGE), 1)
        ok = jnp.logical_and(tk <= tq, tk < t_new)
        onew_ref[0] = jnp.where(ok, weigh(_dot3(q, new_ref[0], NT)), -jnp.inf)


def _idx_scores_s(page_table, q_s, w_s, pool_ki_t, ki_new, *, layer, t_new, n_pg=8):
    b, n_pages = page_table.shape
    grid_spec = pltpu.PrefetchScalarGridSpec(
        num_scalar_prefetch=1,
        grid=(b, n_pages // n_pg),
        in_specs=[pl.BlockSpec((1, IDX_HEADS * TS_PAD, IDX_HD), lambda bi, p, pt: (bi, 0, 0)),
                  pl.BlockSpec((1, IDX_HEADS * TS_PAD, 1), lambda bi, p, pt: (bi, 0, 0))]
        + _page_specs(n_pg, (IDX_HD, PAGE), layer, 1)
        + [pl.BlockSpec((1, PAGE, IDX_HD), lambda bi, p, pt: (bi, 0, 0))],
        out_specs=[pl.BlockSpec((1, TS_PAD, n_pg * PAGE), lambda bi, p, pt: (bi, 0, p)),
                   pl.BlockSpec((1, TS_PAD, PAGE), lambda bi, p, pt: (bi, 0, 0))])
    return pl.pallas_call(
        functools.partial(_idx_scores_s_kernel, n_pg=n_pg, t_new=t_new),
        grid_spec=grid_spec,
        out_shape=[jax.ShapeDtypeStruct((b, TS_PAD, n_pages * PAGE), F32),
                   jax.ShapeDtypeStruct((b, TS_PAD, PAGE), F32)],
        compiler_params=_cp(("arbitrary", "arbitrary")),
        name="idx_scores_sample",
    )(page_table, q_s, w_s, *([pool_ki_t] * n_pg), ki_new)


def _dsa_attn_s_kernel(pt_ref, tie_ref, q_ref, *refs, n_pg):
    pk = refs[:n_pg]
    pv = refs[n_pg:2 * n_pg]
    nk_ref, nv_ref, sc_ref, scn_ref, thr_ref, need_ref, o_ref, m_ref, l_ref, acc_ref, eq_ref = refs[2 * n_pg:]
    bi = pl.program_id(0)
    p = pl.program_id(1)
    hg = B_HEADS // B_KV

    @pl.when(p == 0)
    def _():
        m_ref[...] = jnp.full(m_ref.shape, NEG_BIG, F32)
        l_ref[...] = jnp.zeros(l_ref.shape, F32)
        acc_ref[...] = jnp.zeros(acc_ref.shape, F32)
        eq_ref[...] = jnp.zeros(eq_ref.shape, F32)

    thr = thr_ref[0]
    need = need_ref[0]
    scale = B_HD ** -0.5

    def attend(sc, keys_of, vals_of, tie):
        mask8, eq_new = _select_mask(sc, thr, need, None, eq_ref[...], tie)
        if tie:
            eq_ref[...] = eq_new
        mask = jnp.concatenate([mask8] * hg, axis=0)
        for g in range(B_KV):
            s = jnp.where(mask, _dot3(q_ref[0, g] * scale, keys_of(g), NT), NEG_BIG)
            m_old = m_ref[g]
            m_new = jnp.maximum(m_old, jnp.max(s, axis=-1, keepdims=True))
            pr = jnp.where(mask, jnp.exp(s - m_new), 0.0)
            alpha = jnp.exp(m_old - m_new)
            l_ref[g] = alpha * l_ref[g] + jnp.sum(pr, axis=-1, keepdims=True)
            acc_ref[g] = alpha * acc_ref[g] + _dot3(pr, vals_of(g))
            m_ref[g] = m_new

    def paged(refs_):
        return lambda g: jnp.concatenate([r[0, 0, pl.ds(g, PAGE, stride=B_KV), :] for r in refs_], axis=0)

    def run(tie):
        attend(sc_ref[0], paged(pk), paged(pv), tie)

        @pl.when(p == pl.num_programs(1) - 1)
        def _():
            attend(scn_ref[0], lambda g: nk_ref[0, :, g * B_HD:(g + 1) * B_HD],
                   lambda g: nv_ref[0, :, g * B_HD:(g + 1) * B_HD], tie)

    has_tie = tie_ref[bi] > 0

    @pl.when(has_tie)
    def _():
        run(True)

    @pl.when(jnp.logical_not(has_tie))
    def _():
        run(False)

    @pl.when(p == pl.num_programs(1) - 1)
    def _():
        for g in range(B_KV):
            o_ref[0, g] = acc_ref[g] / l_ref[g]


def _dsa_attn_s(page_table, tie_b, q_s, pool_k2, pool_v2, k_new, v_new, scores, scores_new, thr, need,
                *, layer, n_pg=4):
    b, n_pages = page_table.shape
    hg = B_HEADS // B_KV
    rows = hg * TS_PAD
    kvw = B_KV * B_HD
    grid_spec = pltpu.PrefetchScalarGridSpec(
        num_scalar_prefetch=2,
        grid=(b, n_pages // n_pg),
        in_specs=[pl.BlockSpec((1, B_KV, rows, B_HD), lambda bi, p, pt, tie: (bi, 0, 0, 0))]
        + _page_specs(n_pg, (PAGE * B_KV, B_HD), layer, 2) + _page_specs(n_pg, (PAGE * B_KV, B_HD), layer, 2)
        + [pl.BlockSpec((1, PAGE, kvw), lambda bi, p, pt, tie: (bi, 0, 0)),
           pl.BlockSpec((1, PAGE, kvw), lambda bi, p, pt, tie: (bi, 0, 0)),
           pl.BlockSpec((1, TS_PAD, n_pg * PAGE), lambda bi, p, pt, tie: (bi, 0, p)),
           pl.BlockSpec((1, TS_PAD, PAGE), lambda bi, p, pt, tie: (bi, 0, 0)),
           pl.BlockSpec((1, TS_PAD, 1), lambda bi, p, pt, tie: (bi, 0, 0)),
           pl.BlockSpec((1, TS_PAD, 1), lambda bi, p, pt, tie: (bi, 0, 0))],
        out_specs=pl.BlockSpec((1, B_KV, rows, B_HD), lambda bi, p, pt, tie: (bi, 0, 0, 0)),
        scratch_shapes=[pltpu.VMEM((B_KV, rows, 1), F32), pltpu.VMEM((B_KV, rows, 1), F32),
                        pltpu.VMEM((B_KV, rows, B_HD), F32), pltpu.VMEM((TS_PAD, 1), F32)])
    return pl.pallas_call(
        functools.partial(_dsa_attn_s_kernel, n_pg=n_pg),
        grid_spec=grid_spec,
        out_shape=jax.ShapeDtypeStruct((b, B_KV, rows, B_HD), F32),
        compiler_params=_cp(("arbitrary", "arbitrary")),
        name="dsa_attn_sample",
    )(page_table, tie_b, q_s, *([pool_k2] * n_pg), *([pool_v2] * n_pg), k_new, v_new, scores, scores_new, thr, need)


def _diff_attn_s_kernel(pt_ref, wt_ref, *refs, n_pg, t_new, lam_init):
    pk = refs[:n_pg]
    pv = refs[n_pg:2 * n_pg]
    nk_ref, nv_ref, ex_ref, hm_ref, lam_ref, ng_ref, o_ref, m_ref, l_ref, acc_ref = refs[2 * n_pg:]
    p = pl.program_id(1)
    nrow = wt_ref.shape[1]
    rph = 2 * t_new
    dv = 2 * C_HD

    @pl.when(p == 0)
    def _():
        m_ref[...] = jnp.full(m_ref.shape, NEG_BIG, F32)
        l_ref[...] = jnp.zeros(l_ref.shape, F32)
        acc_ref[...] = jnp.zeros(acc_ref.shape, F32)

    wt = wt_ref[0]

    def attend(s, pv_of):
        m_old = m_ref[...]
        m_new = jnp.maximum(m_old, jnp.max(s, axis=-1, keepdims=True))
        pr = jnp.exp(s - m_new)
        alpha = jnp.exp(m_old - m_new)
        l_ref[...] = alpha * l_ref[...] + jnp.sum(pr, axis=-1, keepdims=True)
        acc_ref[...] = alpha * acc_ref[...] + pv_of(pr.astype(BF16))
        m_ref[...] = m_new

    def pv_paged(prb):
        out = jnp.zeros((nrow, dv), F32)
        for k in range(n_pg):
            spread = _dot(prb[:, k * PAGE:(k + 1) * PAGE], ex_ref[...]) * hm_ref[...]
            out = out + _dot(spread.astype(BF16), pv[k][0, 0].astype(BF16))
        return out

    s_past = _dot(wt, jnp.concatenate([r[0, 0] for r in pk], axis=1).astype(BF16))
    attend(s_past, pv_paged)

    @pl.when(p == pl.num_programs(1) - 1)
    def _():
        tq = lax.broadcasted_iota(I32, (nrow, TS_PAD), 0) & (t_new - 1)
        tk = lax.broadcasted_iota(I32, (nrow, TS_PAD), 1)
        s_new = jnp.where(tk <= tq, _dot(wt, nk_ref[0].astype(BF16), NT), NEG_BIG)
        attend(s_new, lambda prb: jnp.concatenate(
            [_dot(prb[h * rph:(h + 1) * rph, :], nv_ref[0, :, h * dv:(h + 1) * dv].astype(BF16))
             for h in range(C_HEADS)], axis=0))
        lam = _lambda_value(lam_ref, lam_init)
        a = acc_ref[...] / l_ref[...]
        for h in range(C_HEADS):
            r0 = h * rph
            o = a[r0:r0 + t_new, :] - lam * a[r0 + t_new:r0 + rph, :]
            on = o * lax.rsqrt(jnp.mean(o * o, axis=-1, keepdims=True) + EPS) * ng_ref[...]
            o_ref[0, :, h * dv:(h + 1) * dv] = on * (1.0 - lam_init)


def _diff_attn_s(page_table, wt, pool_kt, pool_v2, k_new, v_new, c_lambda, c_norm_g, *, layer, t_new, lam_init,
                 n_pg=4):
    b, n_pages = page_table.shape
    nrow, width = wt.shape[1:]
    dv = 2 * C_HD
    vrows = PAGE * C_HEADS
    vrow = jnp.arange(vrows, dtype=I32)
    expand = (vrow[None, :] // C_HEADS == jnp.arange(PAGE, dtype=I32)[:, None]).astype(BF16)
    own_head = (vrow[None, :] % C_HEADS == jnp.arange(nrow, dtype=I32)[:, None] // (2 * t_new)).astype(F32)
    grid_spec = pltpu.PrefetchScalarGridSpec(
        num_scalar_prefetch=1,
        grid=(b, n_pages // n_pg),
        in_specs=[pl.BlockSpec((1, nrow, width), lambda bi, p, pt: (bi, 0, 0))]
        + _page_specs(n_pg, (width, PAGE), layer, 1) + _page_specs(n_pg, (vrows, dv), layer, 1)
        + [pl.BlockSpec((1, TS_PAD, width), lambda bi, p, pt: (bi, 0, 0)),
           pl.BlockSpec((1, TS_PAD, width), lambda bi, p, pt: (bi, 0, 0)),
           pl.BlockSpec((PAGE, vrows), lambda bi, p, pt: (0, 0)),
           pl.BlockSpec((nrow, vrows), lambda bi, p, pt: (0, 0)),
           pl.BlockSpec((4, C_HD), lambda bi, p, pt: (0, 0)),
           pl.BlockSpec((1, dv), lambda bi, p, pt: (0, 0))],
        out_specs=pl.BlockSpec((1, t_new, width), lambda bi, p, pt: (bi, 0, 0)),
        scratch_shapes=[pltpu.VMEM((nrow, 1), F32), pltpu.VMEM((nrow, 1), F32), pltpu.VMEM((nrow, dv), F32)])
    return pl.pallas_call(
        functools.partial(_diff_attn_s_kernel, n_pg=n_pg, t_new=t_new, lam_init=lam_init),
        grid_spec=grid_spec,
        out_shape=jax.ShapeDtypeStruct((b, t_new, width), F32),
        compiler_params=_cp(("arbitrary", "arbitrary"), 56),
        name="diff_attn_sample",
    )(page_table, wt, *([pool_kt] * n_pg), *([pool_v2] * n_pg), k_new, v_new, expand, own_head, c_lambda,
      c_norm_g.reshape(1, dv))


def _split_ab_weights(w_in):
    n_a = 4 * A_HEADS * A_DK
    n_gate = 2 * A_HEADS
    n_b = (B_HEADS + 2 * B_KV) * B_HD
    n_idx = IDX_HEADS * IDX_HD + IDX_HD + IDX_HEADS
    big = jnp.concatenate([w_in[:, :n_a], w_in[:, n_a + n_gate:n_a + n_gate + n_b]], axis=1)
    pad = 5 * LANES - n_idx - n_gate
    small = jnp.concatenate([w_in[:, n_a + n_gate + n_b:], w_in[:, n_a:n_a + n_gate],
                             jnp.zeros((w_in.shape[0], pad), w_in.dtype)], axis=1)
    return big, small


PROMPT_TM = 2048
GATE_OFF = IDX_HD + IDX_HEADS
QB_COL0 = 4 * A_HEADS * A_DK // LANES
KB_COL0 = QB_COL0 + B_HEADS
VB_COL0 = KB_COL0 + B_KV


def _mlp(x, mods, norm_g, w1, w2, *, tm, hi, tn=512, tk=2048):
    sh2, sc2, g2 = mods
    hid = _norm_matmul(x, norm_g[2:3], sc2, sh2, w1, tm=tm, tn=tn, hi=hi, relu2=True,
                       out_dtype=F32 if hi else BF16, vmem_mb=56)
    if hi:
        return _matmul_norm_res(hid, w2, norm_g[3:4], g2, x, tm=tm, tk=512, hi=True)
    return _matmul_norm_res(hid, w2, norm_g[3:4], g2, x, tm=512, tk=tk, vmem_mb=56)


def _prompt_trunk(x, mods, P):
    t = x.shape[0]
    pos = jnp.arange(t, dtype=I32)
    tab128, tab64 = _rope_tables(pos)
    sh1, sc1, g1, sh2, sc2, g2 = mods[0]
    ng = P["norm_g"][0]
    big = _norm_matmul(x, ng[0:1], sc1, sh1, P["w_big"], tm=PROMPT_TM, tn=256, vmem_mb=56)
    small = _norm_matmul(x, ng[0:1], sc1, sh1, P["w_small"], tm=PROMPT_TM // 2, tn=5 * LANES, hi=True)
    small_r = _rope(small, tab64, col0=0, ngroups=5, n_kind0=4, r1=32, r2=96, tm=1024)
    qb_r = _rope(big, tab128, col0=QB_COL0, ngroups=B_HEADS, n_kind0=B_HEADS, r1=64, r2=None, tm=1024, gw=4)
    kb_r = _rope(big, tab128, col0=KB_COL0, ngroups=B_KV, n_kind0=B_KV, r1=64, r2=None, tm=1024)
    qcat, kcat = _idx_prep(small_r)
    scores = _idx_scores(qcat, kcat, small_r)
    thr, need, tie = _topk_thr(scores, k=min(TOPK, t // 4), cw=1024, causal=True, tr=128)
    tq = 256
    tie_blk = jnp.max(tie.reshape(t // tq, tq), axis=1)
    ob = _dsa_attn(tie_blk, qb_r, kb_r, big, VB_COL0, scores, thr, need, tq=tq)
    oa, s_new = _gdn(big[None], small_r[None], jnp.zeros((1, A_CONV - 1, 3 * A_HEADS * A_DK), F32), P["ab_conv_w"],
                     P["ab_A_log"], P["ab_dt_bias"], P["ab_norm_g"], jnp.zeros((1, A_HEADS, A_DK, A_DK), F32),
                     c=128, t_valid=128, gate_group=4, gate_off=GATE_OFF)
    mix = jnp.concatenate([oa[0], ob], axis=1)
    x = _matmul_norm_res(mix, P["ab_w_out"], ng[1:2], g1, x, tm=512, tk=2048, vmem_mb=56)
    x = _mlp(x, (sh2, sc2, g2), ng, (P["mlp_w1"], 0), (P["mlp_w2"], 0), tm=PROMPT_TM, hi=False)
    outs0 = (kb_r, big[:, VB_COL0 * LANES:], small_r[:, 4 * LANES:4 * LANES + IDX_HD], s_new,
             big[t - (A_CONV - 1):, :3 * A_HEADS * A_DK])
    sh1, sc1, g1, sh2, sc2, g2 = mods[1]
    ng = P["norm_g"][1]
    lam_init = 0.8 - 0.6 * math.exp(-0.3 * 1)
    proj = _norm_matmul(x, ng[0:1], sc1, sh1, P["c_w_in"], tm=PROMPT_TM, tn=256, vmem_mb=56)
    qk_r = _rope(proj, tab64, col0=0, ngroups=2 * C_HEADS, n_kind0=2 * C_HEADS, r1=32, r2=96, tm=1024, gw=4)
    o = _diff_attn(qk_r, proj, 2 * C_HEADS, P["c_lambda"], P["c_norm_g"], lam_init=lam_init)
    x = _matmul_norm_res(o, P["c_w_out"], ng[1:2], g1, x, tm=512, tk=2048, vmem_mb=56)
    x = _mlp(x, (sh2, sc2, g2), ng, (P["mlp_w1"], 1), (P["mlp_w2"], 1), tm=PROMPT_TM, hi=False)
    outs1 = (qk_r[:, 2 * C_HEADS * C_HD:], proj[:, 4 * C_HEADS * C_HD:])
    return x, outs0, outs1


def _sample_trunk(x, mods, P, cache, past_len):
    b, ts_, d = x.shape
    m = b * ts_
    x = x.reshape(m, d)
    pos = past_len + jnp.arange(ts_, dtype=I32)
    tab128, tab64 = _rope_tables(pos)
    tab128 = jnp.tile(tab128, (b, 1))
    tab64 = jnp.tile(tab64, (b, 1))
    page_table = cache["page_table"]
    n_pages = page_table.shape[1]
    hw = A_HEADS * A_DK

    def rows(a):
        return jnp.repeat(a, ts_, axis=0)

    def pad_t(a, n):
        return jnp.pad(a, ((0, 0), (0, n - ts_), (0, 0)))

    sh1, sc1, g1, sh2, sc2, g2 = (rows(a) for a in mods[0])
    ng = P["norm_g"][0]
    big = _norm_matmul(x, ng[0:1], sc1, sh1, P["w_big"], tm=m, tn=512, hi=True)
    small = _norm_matmul(x, ng[0:1], sc1, sh1, P["w_small"], tm=m, tn=5 * LANES, hi=True)
    small_r = _rope(small, tab64, col0=0, ngroups=5, n_kind0=4, r1=32, r2=96, tm=m)
    qb_r = _rope(big, tab128, col0=QB_COL0, ngroups=B_HEADS, n_kind0=B_HEADS, r1=64, r2=None, tm=m)
    kb_r = _rope(big, tab128, col0=KB_COL0, ngroups=B_KV, n_kind0=B_KV, r1=64, r2=None, tm=m)
    vb = big[:, VB_COL0 * LANES:]
    sm3 = small_r.reshape(b, ts_, 5 * LANES)
    qi = sm3[:, :, :IDX_HEADS * IDX_HD].reshape(b, ts_, IDX_HEADS, IDX_HD)
    qi = jnp.pad(jnp.swapaxes(qi, 1, 2), ((0, 0), (0, 0), (0, TS_PAD - ts_), (0, 0)))
    qi = qi.reshape(b, IDX_HEADS * TS_PAD, IDX_HD)
    wi = sm3[:, :, 4 * LANES + W_OFF:4 * LANES + W_OFF + IDX_HEADS]
    wi = jnp.pad(jnp.swapaxes(wi, 1, 2), ((0, 0), (0, 0), (0, TS_PAD - ts_))).reshape(b, IDX_HEADS * TS_PAD, 1)
    ki_r = sm3[:, :, 4 * LANES:4 * LANES + IDX_HD]
    pool_ki_t = jnp.swapaxes(cache["ab_kidx"], 2, 3)
    sc_past, sc_new = _idx_scores_s(page_table, qi, wi, pool_ki_t, pad_t(ki_r, PAGE), layer=0, t_new=ts_,
                                    n_pg=min(16, n_pages))
    scores = jnp.concatenate([sc_past, sc_new], axis=2)
    ncols = scores.shape[2]
    k_sel = min(TOPK, (past_len + ts_) // 4)
    chunk = min(d for d in range(1, ncols // LANES + 1) if (ncols // LANES) % d == 0 and d * LANES >= k_sel) * LANES
    thr, need, tie = _topk_thr(scores.reshape(b * TS_PAD, ncols), k=k_sel, cw=chunk, causal=False, tr=64)
    tie_b = jnp.max(tie.reshape(b, TS_PAD)[:, :ts_], axis=1)
    hg = B_HEADS // B_KV
    q4 = qb_r.reshape(b, ts_, B_KV, hg, B_HD)
    q4 = jnp.pad(jnp.transpose(q4, (0, 2, 3, 1, 4)), ((0, 0), (0, 0), (0, 0), (0, TS_PAD - ts_), (0, 0)))
    q4 = q4.reshape(b, B_KV, hg * TS_PAD, B_HD)
    kvw = B_KV * B_HD
    pool_k2 = cache["ab_k"].reshape(cache["ab_k"].shape[:2] + (PAGE * B_KV, B_HD))
    pool_v2 = cache["ab_v"].reshape(cache["ab_v"].shape[:2] + (PAGE * B_KV, B_HD))
    ob = _dsa_attn_s(page_table, tie_b, q4, pool_k2, pool_v2, pad_t(kb_r.reshape(b, ts_, kvw), PAGE),
                     pad_t(vb.reshape(b, ts_, kvw), PAGE), sc_past, sc_new, thr.reshape(b, TS_PAD, 1),
                     need.reshape(b, TS_PAD, 1), layer=0, n_pg=min(16, n_pages))
    ob = ob.reshape(b, B_KV, hg, TS_PAD, B_HD)[:, :, :, :ts_]
    ob = jnp.transpose(ob, (0, 3, 1, 2, 4)).reshape(m, B_HEADS * B_HD)
    oa, s_new = _gdn(pad_t(big.reshape(b, ts_, -1), TS_PAD), pad_t(sm3, TS_PAD), cache["ab_conv"][0], P["ab_conv_w"],
                     P["ab_A_log"], P["ab_dt_bias"], P["ab_norm_g"], cache["ab_delta"][0],
                     c=TS_PAD, t_valid=ts_, gate_group=4, gate_off=GATE_OFF)
    mix = jnp.concatenate([oa[:, :ts_].reshape(m, hw), ob], axis=1)
    x = _matmul_norm_res(mix, P["ab_w_out"], ng[1:2], g1, x, tm=m, tk=512, hi=True)
    x = _mlp(x, (sh2, sc2, g2), ng, (P["mlp_w1"], 0), (P["mlp_w2"], 0), tm=m, hi=True)
    conv_in = jnp.concatenate([cache["ab_conv"][0], big.reshape(b, ts_, -1)[:, :, :3 * hw]], axis=1)
    outs0 = (kb_r, vb, ki_r, s_new, conv_in[:, ts_:])
    sh1, sc1, g1, sh2, sc2, g2 = (rows(a) for a in mods[1])
    ng = P["norm_g"][1]
    lam_init = 0.8 - 0.6 * math.exp(-0.3 * 1)
    proj = _norm_matmul(x, ng[0:1], sc1, sh1, P["c_w_in"], tm=m, tn=512, hi=True)
    qk_r = _rope(proj, tab64, col0=0, ngroups=2 * C_HEADS, n_kind0=2 * C_HEADS, r1=32, r2=96, tm=m)
    cw = 2 * C_HEADS * C_HD
    q3 = qk_r[:, :cw].reshape(b, ts_, cw) * C_HD ** -0.5
    k3 = qk_r[:, cw:].reshape(b, ts_, cw)
    v3 = proj[:, 2 * cw:].reshape(b, ts_, cw)
    lane_hp = jnp.arange(cw, dtype=I32) // C_HD
    want = (2 * jnp.arange(C_HEADS, dtype=I32)[:, None] + jnp.arange(2, dtype=I32)[None, :])
    sel = (lane_hp[None, None, :] == want[:, :, None]).astype(F32)
    wt = (q3[:, None, None, :, :] * sel[None, :, :, None, :]).reshape(b, 2 * C_HEADS * ts_, cw).astype(BF16)
    ck = cache["c_k"]
    pool_ckt = jnp.transpose(ck, (0, 1, 3, 4, 5, 2)).reshape(ck.shape[:2] + (cw, PAGE))
    pool_cv2 = cache["c_v"].reshape(cache["c_v"].shape[:2] + (PAGE * C_HEADS, 2 * C_HD))
    o = _diff_attn_s(page_table, wt, pool_ckt, pool_cv2, pad_t(k3, TS_PAD), pad_t(v3, TS_PAD), P["c_lambda"],
                     P["c_norm_g"], layer=0, t_new=ts_, lam_init=lam_init, n_pg=min(8, n_pages))
    x = _matmul_norm_res(o.reshape(m, cw), P["c_w_out"], ng[1:2], g1, x, tm=m, tk=512, hi=True)
    x = _mlp(x, (sh2, sc2, g2), ng, (P["mlp_w1"], 1), (P["mlp_w2"], 1), tm=m, hi=True)
    outs1 = (qk_r[:, cw:], proj[:, 2 * cw:])
    return x.reshape(b, ts_, d), outs0, outs1


def kernel(x_prompt, x_sample, cache_ab_k, cache_ab_v, cache_ab_kidx, state_ab_delta, state_ab_conv, cache_c_k,
           cache_c_v, page_table, c_prompt, c_sample, ada_w, ada_b, norm_g, mlp_w1, mlp_w2, ab_w_in, ab_w_out,
           ab_conv_w, ab_A_log, ab_dt_bias, ab_norm_g, c_w_in, c_w_out, c_lambda, c_norm_g):
    bp, t, d = x_prompt.shape
    bs, ts_, _ = x_sample.shape
    assert bp == 1
    past_len = page_table.shape[1] * PAGE
    w_big, w_small = _split_ab_weights(ab_w_in[0])
    P = {"norm_g": norm_g, "mlp_w1": mlp_w1, "mlp_w2": mlp_w2, "w_big": w_big, "w_small": w_small,
         "ab_w_out": ab_w_out[0], "ab_conv_w": ab_conv_w[0], "ab_A_log": ab_A_log[0], "ab_dt_bias": ab_dt_bias[0],
         "ab_norm_g": ab_norm_g[0], "c_w_in": c_w_in[0], "c_w_out": c_w_out[0], "c_lambda": c_lambda[0],
         "c_norm_g": c_norm_g[0]}
    n_seq = bp + bs
    mc = -(-n_seq // 8) * 8
    c_all = jnp.pad(jnp.concatenate([c_prompt, c_sample], axis=0), ((0, mc - n_seq), (0, 0)))
    mod = _ada(c_all, ada_w, ada_b)
    mods_p = [tuple(mod[i, 0:bp, n * d:(n + 1) * d] for n in range(6)) for i in range(2)]
    mods_s = [tuple(mod[i, bp:n_seq, n * d:(n + 1) * d] for n in range(6)) for i in range(2)]

    y_p, p0, p1 = _prompt_trunk(x_prompt[0], mods_p, P)
    cache = {"ab_k": cache_ab_k, "ab_v": cache_ab_v, "ab_kidx": cache_ab_kidx, "ab_delta": state_ab_delta,
             "ab_conv": state_ab_conv, "c_k": cache_c_k, "c_v": cache_c_v, "page_table": page_table}
    y_s, s0, s1 = _sample_trunk(x_sample, mods_s, P, cache, past_len)

    return (y_p[None], y_s,
            p0[0].reshape(1, 1, t, B_KV, B_HD), p0[1].reshape(1, 1, t, B_KV, B_HD), p0[2].reshape(1, 1, t, IDX_HD),
            p0[3][None], p0[4].reshape(1, 1, A_CONV - 1, -1),
            p1[0].reshape(1, 1, t, C_HEADS, 2, C_HD), p1[1].reshape(1, 1, t, C_HEADS, 2 * C_HD),
            s0[0].reshape(1, bs, ts_, B_KV, B_HD), s0[1].reshape(1, bs, ts_, B_KV, B_HD),
            s0[2].reshape(1, bs, ts_, IDX_HD), s0[3][None], s0[4][None],
            s1[0].reshape(1, bs, ts_, C_HEADS, 2, C_HD), s1[1].reshape(1, bs, ts_, C_HEADS, 2 * C_HD))
```

```python
import functools
import math

import jax
import jax.numpy as jnp
from jax import lax
from jax.experimental import pallas as pl
from jax.experimental.pallas import tpu as pltpu

F32 = jnp.float32
BF16 = jnp.bfloat16
I32 = jnp.int32
I16 = jnp.int16

EPS = 1e-6
NEG_BIG = -1e30
ROPE_THETA = 10000.0
PAGE = 128
A_HEADS = 8
A_DK = 128
A_CONV = 4
B_HEADS = 8
B_KV = 2
B_HD = 128
IDX_HEADS = 8
IDX_HD = 64
TOPK = 256
C_HEADS = 16
C_HD = 64
LANES = 128

NN = (((1,), (0,)), ((), ()))
NT = (((1,), (1,)), ((), ()))
TN = (((0,), (0,)), ((), ()))
BNN = (((2,), (1,)), ((0,), (0,)))
BNT = (((2,), (2,)), ((0,), (0,)))
KEY_NEG_INF = -2139095041


def _cp(dims, vmem_mb=None):
    kw = dict(dimension_semantics=dims)
    if vmem_mb is not None:
        kw["vmem_limit_bytes"] = vmem_mb << 20
    return pltpu.CompilerParams(**kw)


def _dot(a, b, dims=NN):
    return lax.dot_general(a, b, dims, preferred_element_type=F32)


def _split2(a):
    hi = a.astype(BF16)
    return hi, (a - hi.astype(F32)).astype(BF16)


def _split3(a):
    hi = a.astype(BF16)
    r = a - hi.astype(F32)
    mid = r.astype(BF16)
    return hi, mid, (r - mid.astype(F32)).astype(BF16)


def _dot1(a, b, dims=NN):
    return _dot(a.astype(BF16), b.astype(BF16), dims)


def _dot3(a, b, dims=NN):
    ah, al = _split2(a)
    bh, bl = _split2(b)
    return _dot(ah, bh, dims) + (_dot(ah, bl, dims) + _dot(al, bh, dims))


def _dot_exact_left(ones_bf16, b, dims=NN):
    b0, b1, b2 = _split3(b)
    return _dot(ones_bf16, b0, dims) + (_dot(ones_bf16, b1, dims) + _dot(ones_bf16, b2, dims))


def _sigmoid(x):
    return 1.0 / (1.0 + jnp.exp(-x))


def _silu(x):
    return x * _sigmoid(x)


def _ada_kernel(c_ref, w_ref, b_ref, o_ref):
    o_ref[0] = _dot3(_silu(c_ref[...]), w_ref[0]) + b_ref[0]


def _ada(c_all, ada_w, ada_b, tn=512):
    nl, d, n = ada_w.shape
    mc = c_all.shape[0]
    return pl.pallas_call(
        _ada_kernel,
        grid=(nl, n // tn),
        in_specs=[pl.BlockSpec((mc, d), lambda l, j: (0, 0)),
                  pl.BlockSpec((1, d, tn), lambda l, j: (l, 0, j)),
                  pl.BlockSpec((1, 1, tn), lambda l, j: (l, 0, j))],
        out_specs=pl.BlockSpec((1, mc, tn), lambda l, j: (l, 0, j)),
        out_shape=jax.ShapeDtypeStruct((nl, mc, n), F32),
        compiler_params=_cp(("arbitrary", "arbitrary"), 40),
        name="ada_mod",
    )(c_all, ada_w, ada_b.reshape(nl, 1, n))


def _norm_mm_kernel(x_ref, g_ref, sc_ref, sh_ref, w_ref, o_ref, *scratch, hi, relu2):
    hh_ref = scratch[0]

    @pl.when(pl.program_id(1) == 0)
    def _():
        x = x_ref[...]
        y = x * lax.rsqrt(jnp.mean(x * x, axis=-1, keepdims=True) + EPS)
        h = (y * g_ref[...]) * (1.0 + sc_ref[...]) + sh_ref[...]
        hh = h.astype(BF16)
        hh_ref[...] = hh
        if hi:
            scratch[1][...] = (h - hh.astype(F32)).astype(BF16)

    w = w_ref[0]
    wh = w.astype(BF16)
    acc = _dot(hh_ref[...], wh)
    if hi:
        wl = (w - wh.astype(F32)).astype(BF16)
        acc = acc + (_dot(hh_ref[...], wl) + _dot(scratch[1][...], wh))
    if relu2:
        acc = jnp.square(jnp.maximum(acc, 0.0))
    o_ref[...] = acc.astype(o_ref.dtype)


def _layer_weight(w):
    return w if isinstance(w, tuple) else (w[None], 0)


def _norm_matmul(x, g, sc, sh, w, *, tm, tn, hi=False, relu2=False, out_dtype=F32, vmem_mb=48):
    m, d = x.shape
    tm = min(tm, m)
    w, layer = _layer_weight(w)
    n = w.shape[2]
    per_row = sc.shape[0] != 1
    mod_spec = pl.BlockSpec((tm, d), lambda i, j: (i, 0)) if per_row else pl.BlockSpec((1, d), lambda i, j: (0, 0))
    scratch = [pltpu.VMEM((tm, d), BF16)] + ([pltpu.VMEM((tm, d), BF16)] if hi else [])
    return pl.pallas_call(
        functools.partial(_norm_mm_kernel, hi=hi, relu2=relu2),
        grid=(m // tm, n // tn),
        in_specs=[pl.BlockSpec((tm, d), lambda i, j: (i, 0), pipeline_mode=pl.Buffered(1)),
                  pl.BlockSpec((1, d), lambda i, j: (0, 0)),
                  mod_spec, mod_spec,
                  pl.BlockSpec((1, d, tn), lambda i, j: (layer, 0, j))],
        out_specs=pl.BlockSpec((tm, tn), lambda i, j: (i, j)),
        out_shape=jax.ShapeDtypeStruct((m, n), out_dtype),
        scratch_shapes=scratch,
        compiler_params=_cp(("arbitrary", "arbitrary"), vmem_mb),
        name="norm_matmul",
    )(x, g, sc, sh, w)


def _mm_norm_res_kernel(a_ref, w_ref, ng_ref, gate_ref, res_ref, o_ref, *, hi):
    k = pl.program_id(1)

    @pl.when(k == 0)
    def _():
        o_ref[...] = jnp.zeros(o_ref.shape, F32)

    if hi:
        o_ref[...] += _dot3(a_ref[...].astype(F32), w_ref[0])
    else:
        o_ref[...] += _dot1(a_ref[...], w_ref[0])

    @pl.when(k == pl.num_programs(1) - 1)
    def _():
        m = o_ref[...]
        y = m * lax.rsqrt(jnp.mean(m * m, axis=-1, keepdims=True) + EPS)
        o_ref[...] = res_ref[...] + gate_ref[...] * (y * ng_ref[...])


def _matmul_norm_res(a, w, ng, gate, res, *, tm, tk, hi=False, vmem_mb=48):
    m, kdim = a.shape
    w, layer = _layer_weight(w)
    n = w.shape[2]
    per_row = gate.shape[0] != 1
    gate_spec = pl.BlockSpec((tm, n), lambda i, k: (i, 0)) if per_row else pl.BlockSpec((1, n), lambda i, k: (0, 0))
    return pl.pallas_call(
        functools.partial(_mm_norm_res_kernel, hi=hi),
        grid=(m // tm, kdim // tk),
        in_specs=[pl.BlockSpec((tm, tk), lambda i, k: (i, k)),
                  pl.BlockSpec((1, tk, n), lambda i, k: (layer, k, 0)),
                  pl.BlockSpec((1, n), lambda i, k: (0, 0)),
                  gate_spec,
                  pl.BlockSpec((tm, n), lambda i, k: (i, 0), pipeline_mode=pl.Buffered(1))],
        out_specs=pl.BlockSpec((tm, n), lambda i, k: (i, 0)),
        out_shape=jax.ShapeDtypeStruct((m, n), F32),
        compiler_params=_cp(("arbitrary", "arbitrary"), vmem_mb),
        name="matmul_norm_res",
    )(a, w, ng, gate, res)


def _rope_kernel(x_ref, tab_ref, o_ref, *, r1, r2, gw):
    for k in range(gw):
        x = x_ref[:, k * LANES:(k + 1) * LANES]
        out = x * tab_ref[:, 0:LANES] + pltpu.roll(x, r1, 1) * tab_ref[:, LANES:2 * LANES]
        if r2 is not None:
            out = out + pltpu.roll(x, r2, 1) * tab_ref[:, 2 * LANES:3 * LANES]
        o_ref[:, k * LANES:(k + 1) * LANES] = out


def _rope(x, tab, *, col0, ngroups, n_kind0, r1, r2, tm, gw=1):
    t = x.shape[0]
    assert col0 % gw == 0 and ngroups % gw == 0 and n_kind0 % gw == 0
    return pl.pallas_call(
        functools.partial(_rope_kernel, r1=r1, r2=r2, gw=gw),
        grid=(t // tm, ngroups // gw),
        in_specs=[pl.BlockSpec((tm, gw * LANES), lambda i, j: (i, col0 // gw + j)),
                  pl.BlockSpec((tm, 3 * LANES), lambda i, j: (i, jnp.where(j * gw >= n_kind0, 1, 0)))],
        out_specs=pl.BlockSpec((tm, gw * LANES), lambda i, j: (i, j)),
        out_shape=jax.ShapeDtypeStruct((t, ngroups * LANES), F32),
        compiler_params=_cp(("arbitrary", "arbitrary")),
        name="rope",
    )(x, tab)


def _rope_tables(pos):
    p = pos.astype(F32)[:, None]

    def cs(half):
        inv = ROPE_THETA ** (-jnp.arange(half, dtype=F32) / half)
        ang = p * inv[None, :]
        return jnp.cos(ang), jnp.sin(ang)

    c, s = cs(64)
    tab128 = jnp.concatenate([c, c, -s, s, jnp.zeros_like(c), jnp.zeros_like(c)], axis=1)
    c, s = cs(32)
    z = jnp.zeros_like(c)
    one64 = jnp.ones((pos.shape[0], 64), F32)
    z64 = jnp.zeros((pos.shape[0], 64), F32)
    kind0 = jnp.concatenate([c, c, c, c, z, s, z, s, -s, z, -s, z], axis=1)
    kind1 = jnp.concatenate([c, c, one64, z, s, z64, -s, z, z64], axis=1)
    return tab128, jnp.concatenate([kind0, kind1], axis=1)


def _gdn_kernel(qkv_ref, z_ref, gate_ref, buf_ref, cw_ref, alog_ref, dtb_ref, ng_ref, s0_ref,
                o_ref, sout_ref, xbuf_ref, s_ref, *, c, t_valid, gate_off):
    ci = pl.program_id(1)
    nh, dk = A_HEADS, A_DK
    hw = nh * dk

    @pl.when(ci == 0)
    def _():
        s_ref[...] = s0_ref[0]
        xbuf_ref[0:8, :] = jnp.zeros((8, 3 * hw), F32)
        xbuf_ref[8 - (A_CONV - 1):8, :] = buf_ref[0]

    xbuf_ref[8:8 + c, :] = qkv_ref[0]
    y = xbuf_ref[5:5 + c, :] * cw_ref[0:1, :]
    for j in range(1, A_CONV):
        y = y + xbuf_ref[5 + j:5 + j + c, :] * cw_ref[j:j + 1, :]
    tail = xbuf_ref[8 + c - 3:8 + c, :]
    xbuf_ref[5:8, :] = tail
    y = _silu(y)

    gt = gate_ref[0]
    ba = gt[:, gate_off:gate_off + nh]
    aa = gt[:, gate_off + nh:gate_off + 2 * nh]
    beta = _sigmoid(ba)
    xs = aa + dtb_ref[...]
    softplus = jnp.maximum(xs, 0.0) + jnp.log1p(jnp.exp(-jnp.abs(xs)))
    g = -jnp.exp(alog_ref[...]) * softplus
    row = lax.broadcasted_iota(I32, (c, c), 0)
    col = lax.broadcasted_iota(I32, (c, c), 1)
    if t_valid < c:
        valid = lax.broadcasted_iota(I32, (c, nh), 0) < t_valid
        beta = jnp.where(valid, beta, 0.0)
        g = jnp.where(valid, g, 0.0)
    incl = (row >= col)[None]
    strict = (row > col)[None]
    tri = jnp.where(row >= col, 1.0, 0.0).astype(BF16)
    gc = _dot_exact_left(tri, g)
    eye_h = jnp.where(lax.broadcasted_iota(I32, (nh, nh), 0) == lax.broadcasted_iota(I32, (nh, nh), 1),
                      1.0, 0.0).astype(BF16)
    gct = _dot_exact_left(eye_h, gc, NT)
    eye_c = jnp.where(row == col, 1.0, 0.0)[None]

    def heads(a, off):
        return jnp.stack([a[:, off + h * dk:off + (h + 1) * dk] for h in range(nh)], axis=0)

    def cols(a):
        return jnp.stack([a[:, h:h + 1] for h in range(nh)], axis=0)

    q3 = heads(y, 0)
    k3 = heads(y, hw)
    v3 = heads(y, 2 * hw)
    q3 = q3 * lax.rsqrt(jnp.sum(q3 * q3, axis=-1, keepdims=True) + 1e-6)
    k3 = k3 * lax.rsqrt(jnp.sum(k3 * k3, axis=-1, keepdims=True) + 1e-6)
    b3 = cols(beta)
    gch = cols(gc)
    gct3 = jnp.stack([gct[h:h + 1, :] for h in range(nh)], axis=0)
    decay = jnp.where(incl, jnp.exp(jnp.minimum(gch - gct3, 0.0)), 0.0)
    kb = k3 * b3
    vb = v3 * b3
    a_mat = jnp.where(strict, _dot3(kb, k3, BNT) * decay, 0.0)
    def same_block(s):
        sh = int(round(math.log2(s)))
        return (jnp.right_shift(row, sh) == jnp.right_shift(col, sh))[None]

    s_blk = min(8, c)
    pw = jnp.where(same_block(s_blk), -a_mat, 0.0)
    t_inv = eye_c + pw
    for _ in range(int(round(math.log2(s_blk))) - 1):
        pw = _dot3(pw, pw, BNN)
        t_inv = t_inv + _dot3(t_inv, pw, BNN)
    while s_blk < c:
        off = jnp.where(jnp.logical_and(same_block(2 * s_blk), jnp.logical_not(same_block(s_blk))), a_mat, 0.0)
        t_inv = t_inv - _dot3(t_inv, _dot3(off, t_inv, BNN), BNN)
        s_blk *= 2
    egc = jnp.exp(gch)
    u = _dot3(t_inv, vb, BNN)
    w = _dot3(t_inv, kb * egc, BNN)
    qs = q3 * dk ** -0.5
    qk = _dot3(qs, k3, BNT) * decay
    g_last = gch[:, c - 1:c, :]
    k_dec = k3 * jnp.exp(jnp.minimum(g_last - gch, 0.0))
    s_old = s_ref[...]
    v_new = u - _dot3(w, s_old, BNN)
    o = _dot3(qs * egc, s_old, BNN) + _dot3(qk, v_new, BNN)
    eg_last = jnp.exp(g_last)
    on = o * lax.rsqrt(jnp.mean(o * o, axis=-1, keepdims=True) + EPS) * ng_ref[...]
    for h in range(nh):
        s_ref[h] = s_old[h] * eg_last[h] + _dot3(k_dec[h], v_new[h], TN)
        o_ref[0, :, h * dk:(h + 1) * dk] = on[h] * _silu(z_ref[0, :, h * dk:(h + 1) * dk])

    @pl.when(ci == pl.num_programs(1) - 1)
    def _():
        sout_ref[0] = s_ref[...]


def _gdn(proj, small, conv_buf, conv_w, a_log, dt_bias, norm_g, s0, *, c, t_valid, gate_group, gate_off):
    b, t = proj.shape[:2]
    nh, dk = A_HEADS, A_DK
    hw = nh * dk
    return pl.pallas_call(
        functools.partial(_gdn_kernel, c=c, t_valid=t_valid, gate_off=gate_off),
        grid=(b, t // c),
        in_specs=[pl.BlockSpec((1, c, 3 * hw), lambda bi, ci: (bi, ci, 0)),
                  pl.BlockSpec((1, c, hw), lambda bi, ci: (bi, ci, 3)),
                  pl.BlockSpec((1, c, LANES), lambda bi, ci: (bi, ci, gate_group)),
                  pl.BlockSpec((1, A_CONV - 1, 3 * hw), lambda bi, ci: (bi, 0, 0)),
                  pl.BlockSpec((A_CONV, 3 * hw), lambda bi, ci: (0, 0)),
                  pl.BlockSpec((1, nh), lambda bi, ci: (0, 0)),
                  pl.BlockSpec((1, nh), lambda bi, ci: (0, 0)),
                  pl.BlockSpec((1, dk), lambda bi, ci: (0, 0)),
                  pl.BlockSpec((1, nh, dk, dk), lambda bi, ci: (bi, 0, 0, 0))],
        out_specs=[pl.BlockSpec((1, c, hw), lambda bi, ci: (bi, ci, 0)),
                   pl.BlockSpec((1, nh, dk, dk), lambda bi, ci: (bi, 0, 0, 0))],
        out_shape=[jax.ShapeDtypeStruct((b, t, hw), F32), jax.ShapeDtypeStruct((b, nh, dk, dk), F32)],
        scratch_shapes=[pltpu.VMEM((8 + c, 3 * hw), F32), pltpu.VMEM((nh, dk, dk), F32)],
        compiler_params=_cp(("arbitrary", "arbitrary"), 40),
        name="gdn",
    )(proj, proj, small, conv_buf, conv_w, a_log.reshape(1, nh), dt_bias.reshape(1, nh), norm_g.reshape(1, dk), s0)


def _idx_prep_kernel(x_ref, q_ref, k_ref):
    tm = x_ref.shape[0]
    lo_half = lax.broadcasted_iota(I32, (tm, LANES), 1) < 64

    def split(xg):
        hi = xg.astype(BF16).astype(F32)
        return hi, xg - hi

    for gq in range(IDX_HEADS // 2):
        hi, lo = split(x_ref[:, gq * LANES:(gq + 1) * LANES])
        hi_r = pltpu.roll(hi, 64, 1)
        lo_r = pltpu.roll(lo, 64, 1)
        base = 2 * gq * 256
        q_ref[:, base:base + 128] = jnp.where(lo_half, hi, hi_r).astype(BF16)
        q_ref[:, base + 128:base + 256] = jnp.where(lo_half, lo, lo_r).astype(BF16)
        q_ref[:, base + 256:base + 384] = jnp.where(lo_half, hi_r, hi).astype(BF16)
        q_ref[:, base + 384:base + 512] = jnp.where(lo_half, lo_r, lo).astype(BF16)
    hi, lo = split(x_ref[:, 4 * LANES:5 * LANES])
    kk = jnp.where(lo_half, hi, pltpu.roll(lo, 64, 1)).astype(BF16)
    k_ref[:, 0:128] = kk
    k_ref[:, 128:256] = kk


def _idx_prep(small_r, tm=512):
    t = small_r.shape[0]
    return pl.pallas_call(
        _idx_prep_kernel,
        grid=(t // tm,),
        in_specs=[pl.BlockSpec((tm, 5 * LANES), lambda i: (i, 0))],
        out_specs=[pl.BlockSpec((tm, IDX_HEADS * 256), lambda i: (i, 0)),
                   pl.BlockSpec((tm, 256), lambda i: (i, 0))],
        out_shape=[jax.ShapeDtypeStruct((t, IDX_HEADS * 256), BF16), jax.ShapeDtypeStruct((t, 256), BF16)],
        compiler_params=_cp(("arbitrary",)),
        name="idx_prep",
    )(small_r)


IDX_SCALE = IDX_HEADS ** -0.5 * IDX_HD ** -0.5
W_OFF = 64


def _idx_scores_kernel(q_ref, k_ref, w_ref, o_ref, *, tq, ts):
    i = pl.program_id(0)
    j = pl.program_id(1)
    live = j * ts <= i * tq + tq - 1

    @pl.when(live)
    def _():
        k = k_ref[...]
        wg = w_ref[...] * IDX_SCALE
        acc = jnp.zeros((tq, ts), F32)
        for h in range(IDX_HEADS):
            s = _dot(q_ref[:, h * 256:(h + 1) * 256], k, NT)
            acc = acc + wg[:, W_OFF + h:W_OFF + h + 1] * jnp.maximum(s, 0.0)
        row = i * tq + lax.broadcasted_iota(I32, (tq, ts), 0)
        col = j * ts + lax.broadcasted_iota(I32, (tq, ts), 1)
        o_ref[...] = jnp.where(col <= row, acc, -jnp.inf)

    @pl.when(jnp.logical_not(live))
    def _():
        o_ref[...] = jnp.full((tq, ts), -jnp.inf, F32)


def _idx_scores(qcat, kcat, small_r, *, tq=256, ts=512):
    t = qcat.shape[0]
    return pl.pallas_call(
        functools.partial(_idx_scores_kernel, tq=tq, ts=ts),
        grid=(t // tq, t // ts),
        in_specs=[pl.BlockSpec((tq, IDX_HEADS * 256), lambda i, j: (i, 0)),
                  pl.BlockSpec((ts, 256), lambda i, j: (j, 0)),
                  pl.BlockSpec((tq, LANES), lambda i, j: (i, 4))],
        out_specs=pl.BlockSpec((tq, ts), lambda i, j: (i, j)),
        out_shape=jax.ShapeDtypeStruct((t, t), F32),
        compiler_params=_cp(("arbitrary", "arbitrary")),
        name="idx_scores",
    )(qcat, kcat, small_r)


def _thr_kernel(s_ref, thr_ref, need_ref, tie_ref, hi_ref, lo_ref, *, k, cw, causal):
    tr, s_cols = s_ref.shape
    nchunks = ((pl.program_id(0) + 1) * tr + cw - 1) // cw if causal else s_cols // cw
    kf = float(k)

    def fill(ci, carry):
        off = pl.multiple_of(ci * cw, cw)
        bits = pltpu.bitcast(s_ref[:, pl.ds(off, cw)] + 0.0, I32)
        key = jnp.where(bits < 0, bits ^ 0x7FFFFFFF, bits)
        hi_ref[:, pl.ds(off, cw)] = jnp.right_shift(key, 16).astype(I16)
        lo_ref[:, pl.ds(off, cw)] = (key ^ 0x8000).astype(I16)
        return carry

    lax.fori_loop(0, nchunks, fill, 0)

    def count(ref, cand, strict):
        def cbody(ci, acc):
            off = pl.multiple_of(ci * cw, cw)
            blk = ref[:, pl.ds(off, cw)]
            for t in range(cw // LANES):
                kk = blk[:, t * LANES:(t + 1) * LANES]
                acc = acc + jnp.where(kk > cand if strict else kk >= cand, jnp.int16(1), jnp.int16(0))
            return acc

        acc = lax.fori_loop(0, nchunks, cbody, jnp.zeros((tr, LANES), I16))
        return jnp.sum(acc.astype(F32), axis=-1, keepdims=True)

    p_hi = jnp.where(count(hi_ref, jnp.zeros((tr, 1), I16), False) >= kf, 0, -32768).astype(I32)

    def hi_body(b, p):
        cand = p | jnp.left_shift(jnp.int32(1), 14 - b)
        return jnp.where(count(hi_ref, cand.astype(I16), False) >= kf, cand, p)

    p_hi = lax.fori_loop(0, 15, hi_body, p_hi)
    p_hi16 = p_hi.astype(I16)
    cnt_gt_hi = count(hi_ref, p_hi16, True)
    n_bucket = count(hi_ref, p_hi16, False) - cnt_gt_hi
    k_in = kf - cnt_gt_hi

    def restrict(ci, carry):
        off = pl.multiple_of(ci * cw, cw)
        lo_ref[:, pl.ds(off, cw)] = jnp.where(hi_ref[:, pl.ds(off, cw)] == p_hi16, lo_ref[:, pl.ds(off, cw)],
                                              jnp.int16(-32768))
        return carry

    lax.fori_loop(0, nchunks, restrict, 0)

    def lo_body(b, p):
        cand = p | jnp.left_shift(jnp.int32(1), 15 - b)
        return jnp.where(count(lo_ref, (cand - 32768).astype(I16), False) >= k_in, cand, p)

    p_lo = lax.fori_loop(0, 16, lo_body, jnp.zeros((tr, 1), I32))
    p_lo16 = (p_lo - 32768).astype(I16)
    cnt_ge_lo = jnp.where(p_lo == 0, n_bucket, count(lo_ref, p_lo16, False))
    p = jnp.left_shift(p_hi, 16) | p_lo
    thr_ref[...] = pltpu.bitcast(jnp.where(p < 0, p ^ 0x7FFFFFFF, p), F32)
    need_ref[...] = k_in - count(lo_ref, p_lo16, True)
    tie_ref[...] = jnp.where(jnp.logical_and(cnt_gt_hi + cnt_ge_lo > kf, p != KEY_NEG_INF), 1, 0).astype(I32)


def _topk_thr(scores, *, k, cw, causal, tr):
    r, s = scores.shape
    tr = min(tr, r)
    assert cw >= k and s % cw == 0
    return pl.pallas_call(
        functools.partial(_thr_kernel, k=k, cw=cw, causal=causal),
        grid=(r // tr,),
        in_specs=[pl.BlockSpec((tr, s), lambda i: (i, 0))],
        out_specs=[pl.BlockSpec((tr, 1), lambda i: (i, 0))] * 3,
        out_shape=[jax.ShapeDtypeStruct((r, 1), F32), jax.ShapeDtypeStruct((r, 1), F32),
                   jax.ShapeDtypeStruct((r, 1), I32)],
        scratch_shapes=[pltpu.VMEM((tr, s), I16), pltpu.VMEM((tr, s), I16)],
        compiler_params=_cp(("arbitrary",), 40),
        name="topk_thr",
    )(scores)


def _select_mask(sc, thr, need, causal, eq_before, tie):
    def visible(m, scc, c0):
        if isinstance(causal, str):
            return m
        return jnp.logical_and(m, scc > -jnp.inf if causal is None else causal[:, c0:c0 + scc.shape[1]])

    if not tie:
        return visible(sc >= thr, sc, 0), None
    n = sc.shape[1]
    cs = min(n, 512)
    upper = jnp.where(lax.broadcasted_iota(I32, (cs, cs), 0) < lax.broadcasted_iota(I32, (cs, cs), 1),
                      1.0, 0.0).astype(BF16)
    masks = []
    for c0 in range(0, n, cs):
        scc = sc[:, c0:c0 + cs]
        eq = scc == thr
        eqf = jnp.where(eq, 1.0, 0.0)
        rank = eq_before + _dot(eqf.astype(BF16), upper)
        sel = jnp.logical_or(scc > thr, jnp.logical_and(eq, rank < need))
        masks.append(visible(sel, scc, c0))
        eq_before = eq_before + jnp.sum(eqf, axis=-1, keepdims=True)
    if len(masks) == 1:
        return masks[0], eq_before
    return jnp.concatenate([jnp.where(m, 1.0, 0.0) for m in masks], axis=1) > 0.5, eq_before


def _flash_update(s, m_ref, acc_ref, r0, rows, vblk):
    ts = s.shape[1]
    m_old = m_ref[r0:r0 + rows, :]
    m_new = jnp.maximum(m_old, jnp.max(s, axis=-1, keepdims=True))
    p = jnp.exp(s - jnp.tile(m_new, (1, ts // LANES)))
    alpha = jnp.exp(m_old - m_new)
    acc_ref[r0:r0 + rows, :] = (jnp.tile(alpha, (1, acc_ref.shape[1] // LANES)) * acc_ref[r0:r0 + rows, :]
                                + _dot(p.astype(BF16), vblk))
    m_ref[r0:r0 + rows, :] = m_new


def _dsa_attn_kernel(tie_ref, q_ref, k_ref, v_ref, sc_ref, thr_ref, need_ref, o_ref,
                     kb_ref, vb_ref, qs_ref, m_ref, acc_ref, eq_ref, *, tq, ts):
    i = pl.program_id(1)
    hg = B_HEADS // B_KV
    d = B_HD

    @pl.when(i == 0)
    def _():
        kb_ref[...] = k_ref[...].astype(BF16)
        vb_ref[:, 0:d] = v_ref[...].astype(BF16)
        vb_ref[:, d:2 * d] = jnp.ones((vb_ref.shape[0], d), BF16)

    for h in range(hg):
        qs_ref[h * tq:(h + 1) * tq, :] = (q_ref[:, h * d:(h + 1) * d] * d ** -0.5).astype(BF16)
    m_ref[...] = jnp.full(m_ref.shape, NEG_BIG, F32)
    acc_ref[...] = jnp.zeros(acc_ref.shape, F32)
    eq_ref[...] = jnp.zeros(eq_ref.shape, F32)
    thr = thr_ref[...]
    need = need_ref[...]
    nblk = (i * tq + tq + ts - 1) // ts

    nfull = (i * tq) // ts

    def run(tie):
        def block(jb, diagonal):
            off = pl.multiple_of(jb * ts, ts)
            sc = sc_ref[:, pl.ds(off, ts)]
            if diagonal:
                row = i * tq + lax.broadcasted_iota(I32, (tq, ts), 0)
                col = off + lax.broadcasted_iota(I32, (tq, ts), 1)
                causal = col <= row
            else:
                causal = "all"
            mask, eq_new = _select_mask(sc, thr, need, causal, eq_ref[...], tie)
            if tie:
                eq_ref[...] = eq_new
            kblk = kb_ref[pl.ds(off, ts), :]
            vblk = vb_ref[pl.ds(off, ts), :]
            for h in range(hg):
                s = jnp.where(mask, _dot(qs_ref[h * tq:(h + 1) * tq, :], kblk, NT), NEG_BIG)
                _flash_update(s, m_ref, acc_ref, h * tq, tq, vblk)

        def full_body(jb, carry):
            block(jb, False)
            return carry

        def diag_body(jb, carry):
            block(jb, True)
            return carry

        lax.fori_loop(0, nfull, full_body, 0)
        lax.fori_loop(nfull, nblk, diag_body, 0)

    has_tie = tie_ref[i] > 0

    @pl.when(has_tie)
    def _():
        run(True)

    @pl.when(jnp.logical_not(has_tie))
    def _():
        run(False)

    for h in range(hg):
        o_ref[:, h * d:(h + 1) * d] = acc_ref[h * tq:(h + 1) * tq, 0:d] / acc_ref[h * tq:(h + 1) * tq, d:2 * d]


def _dsa_attn(tie_blk, q_r, k_r, v_src, v_col0, scores, thr, need, *, tq, ts=512):
    t = q_r.shape[0]
    hg = B_HEADS // B_KV
    once = dict(pipeline_mode=pl.Buffered(1))
    grid_spec = pltpu.PrefetchScalarGridSpec(
        num_scalar_prefetch=1,
        grid=(B_KV, t // tq),
        in_specs=[pl.BlockSpec((tq, hg * B_HD), lambda g, i, tie: (i, g)),
                  pl.BlockSpec((t, B_HD), lambda g, i, tie: (0, g), **once),
                  pl.BlockSpec((t, B_HD), lambda g, i, tie: (0, v_col0 + g), **once),
                  pl.BlockSpec((tq, t), lambda g, i, tie: (i, 0)),
                  pl.BlockSpec((tq, 1), lambda g, i, tie: (i, 0)),
                  pl.BlockSpec((tq, 1), lambda g, i, tie: (i, 0))],
        out_specs=pl.BlockSpec((tq, hg * B_HD), lambda g, i, tie: (i, g)),
        scratch_shapes=[pltpu.VMEM((t, B_HD), BF16), pltpu.VMEM((t, 2 * B_HD), BF16),
                        pltpu.VMEM((hg * tq, B_HD), BF16), pltpu.VMEM((hg * tq, LANES), F32),
                        pltpu.VMEM((hg * tq, 2 * B_HD), F32), pltpu.VMEM((tq, 1), F32)])
    return pl.pallas_call(
        functools.partial(_dsa_attn_kernel, tq=tq, ts=ts),
        grid_spec=grid_spec,
        out_shape=jax.ShapeDtypeStruct((t, B_HEADS * B_HD), F32),
        compiler_params=_cp(("arbitrary", "arbitrary"), 56),
        name="dsa_attn",
    )(tie_blk, q_r, k_r, v_src, scores, thr, need)


def _lambda_value(lam_ref, lam_init):
    lp = lam_ref[...]
    a = jnp.sum(lp[0:1, :] * lp[1:2, :], axis=-1, keepdims=True)
    b = jnp.sum(lp[2:3, :] * lp[3:4, :], axis=-1, keepdims=True)
    return jnp.exp(a) - jnp.exp(b) + lam_init


def _diff_attn_kernel(q_ref, k_ref, v_ref, lam_ref, ng_ref, o_ref, kb_ref, vb_ref, qs_ref, m_ref, acc_ref,
                      *, tq, ts, rsub, lam_init):
    i = pl.program_id(1)
    d2 = 2 * C_HD
    rows = 2 * tq

    @pl.when(i == 0)
    def _():
        kb_ref[...] = k_ref[...].astype(BF16)
        vb_ref[:, 0:d2] = v_ref[...].astype(BF16)
        vb_ref[:, d2:2 * d2] = jnp.ones((vb_ref.shape[0], d2), BF16)

    q = q_ref[...] * C_HD ** -0.5
    first = lax.broadcasted_iota(I32, (tq, d2), 1) < C_HD
    qs_ref[0:tq, :] = jnp.where(first, q, 0.0).astype(BF16)
    qs_ref[tq:rows, :] = jnp.where(first, 0.0, q).astype(BF16)
    m_ref[...] = jnp.full(m_ref.shape, NEG_BIG, F32)
    acc_ref[...] = jnp.zeros(acc_ref.shape, F32)

    def step(jb, diag_col0):
        off = pl.multiple_of(jb * ts, ts)
        kblk = kb_ref[pl.ds(off, ts), :]
        vblk = vb_ref[pl.ds(off, ts), :]
        for r0 in range(0, rows, rsub):
            q0 = r0 % tq
            if diag_col0 is not None and q0 + rsub - 1 < diag_col0:
                continue
            s = _dot(qs_ref[r0:r0 + rsub, :], kblk, NT)
            if diag_col0 is not None and q0 < diag_col0 + ts - 1:
                qrow = q0 + lax.broadcasted_iota(I32, (rsub, ts), 0)
                s = jnp.where(diag_col0 + lax.broadcasted_iota(I32, (rsub, ts), 1) <= qrow, s, NEG_BIG)
            _flash_update(s, m_ref, acc_ref, r0, rsub, vblk)

    def body(jb, carry):
        step(jb, None)
        return carry

    nfull = (i * tq) // ts
    lax.fori_loop(0, nfull, body, 0)
    for dj in range(tq // ts):
        step(nfull + dj, dj * ts)

    lam = _lambda_value(lam_ref, lam_init)
    o = (acc_ref[0:tq, 0:d2] / acc_ref[0:tq, d2:2 * d2]
         - lam * (acc_ref[tq:rows, 0:d2] / acc_ref[tq:rows, d2:2 * d2]))
    on = o * lax.rsqrt(jnp.mean(o * o, axis=-1, keepdims=True) + EPS) * ng_ref[...]
    o_ref[...] = on * (1.0 - lam_init)


def _diff_attn(q_r, k_r, v_src, v_col0, c_lambda, c_norm_g, *, lam_init, tq=2048, ts=1024, rsub=512):
    t = q_r.shape[0]
    tq = min(tq, t)
    d2 = 2 * C_HD
    once = dict(pipeline_mode=pl.Buffered(1))
    return pl.pallas_call(
        functools.partial(_diff_attn_kernel, tq=tq, ts=ts, rsub=rsub, lam_init=lam_init),
        grid=(C_HEADS, t // tq),
        in_specs=[pl.BlockSpec((tq, d2), lambda h, i: (i, h)),
                  pl.BlockSpec((t, d2), lambda h, i: (0, h), **once),
                  pl.BlockSpec((t, d2), lambda h, i: (0, v_col0 + h), **once),
                  pl.BlockSpec((4, C_HD), lambda h, i: (0, 0)),
                  pl.BlockSpec((1, d2), lambda h, i: (0, 0))],
        out_specs=pl.BlockSpec((tq, d2), lambda h, i: (i, h)),
        out_shape=jax.ShapeDtypeStruct((t, C_HEADS * d2), F32),
        scratch_shapes=[pltpu.VMEM((t, d2), BF16), pltpu.VMEM((t, 2 * d2), BF16), pltpu.VMEM((2 * tq, d2), BF16),
                        pltpu.VMEM((2 * tq, LANES), F32), pltpu.VMEM((2 * tq, 2 * d2), F32)],
        compiler_params=_cp(("arbitrary", "arbitrary"), 48),
        name="diff_attn",
    )(q_r, k_r, v_src, c_lambda, c_norm_g.reshape(1, d2))


TS_PAD = 8


def _page_specs(n_pg, blk, layer, n_prefetch):
    def spec(k):
        if n_prefetch == 1:
            return pl.BlockSpec((1, 1) + blk, lambda bi, p, pt: (layer, pt[bi, p * n_pg + k], 0, 0))
        return pl.BlockSpec((1, 1) + blk, lambda bi, p, pt, tie: (layer, pt[bi, p * n_pg + k], 0, 0))

    return [spec(k) for k in range(n_pg)]


def _idx_scores_s_kernel(pt_ref, q_ref, w_ref, *refs, n_pg, t_new):
    pools = refs[:n_pg]
    new_ref, o_ref, onew_ref = refs[n_pg:]
    q = q_ref[0]
    w = w_ref[0] * IDX_SCALE

    def weigh(s):
        acc = jnp.zeros((TS_PAD, s.shape[1]), F32)
        for h in range(IDX_HEADS):
            acc = acc + w[h * TS_PAD:(h + 1) * TS_PAD, :] * jnp.maximum(s[h * TS_PAD:(h + 1) * TS_PAD, :], 0.0)
        return acc

    for k in range(n_pg):
        o_ref[0, :, k * PAGE:(k + 1) * PAGE] = weigh(_dot3(q, pools[k][0, 0]))

    @pl.when(pl.program_id(1) == pl.num_programs(1) - 1)
    def _():
        tq = lax.broadcasted_iota(I32, (TS_PAD, PAGE), 0)
        tk = lax.broadcasted_iota(I32, (TS_PAD, PAGE), 1)
        ok = jnp.logical_and(tk <= tq, tk < t_new)
        onew_ref[0] = jnp.where(ok, weigh(_dot3(q, new_ref[0], NT)), -jnp.inf)


def _idx_scores_s(page_table, q_s, w_s, pool_ki_t, ki_new, *, layer, t_new, n_pg=8):
    b, n_pages = page_table.shape
    grid_spec = pltpu.PrefetchScalarGridSpec(
        num_scalar_prefetch=1,
        grid=(b, n_pages // n_pg),
        in_specs=[pl.BlockSpec((1, IDX_HEADS * TS_PAD, IDX_HD), lambda bi, p, pt: (bi, 0, 0)),
                  pl.BlockSpec((1, IDX_HEADS * TS_PAD, 1), lambda bi, p, pt: (bi, 0, 0))]
        + _page_specs(n_pg, (IDX_HD, PAGE), layer, 1)
        + [pl.BlockSpec((1, PAGE, IDX_HD), lambda bi, p, pt: (bi, 0, 0))],
        out_specs=[pl.BlockSpec((1, TS_PAD, n_pg * PAGE), lambda bi, p, pt: (bi, 0, p)),
                   pl.BlockSpec((1, TS_PAD, PAGE), lambda bi, p, pt: (bi, 0, 0))])
    return pl.pallas_call(
        functools.partial(_idx_scores_s_kernel, n_pg=n_pg, t_new=t_new),
        grid_spec=grid_spec,
        out_shape=[jax.ShapeDtypeStruct((b, TS_PAD, n_pages * PAGE), F32),
                   jax.ShapeDtypeStruct((b, TS_PAD, PAGE), F32)],
        compiler_params=_cp(("arbitrary", "arbitrary")),
        name="idx_scores_sample",
    )(page_table, q_s, w_s, *([pool_ki_t] * n_pg), ki_new)


def _dsa_attn_s_kernel(pt_ref, tie_ref, q_ref, *refs, n_pg):
    pk = refs[:n_pg]
    pv = refs[n_pg:2 * n_pg]
    nk_ref, nv_ref, sc_ref, scn_ref, thr_ref, need_ref, o_ref, m_ref, l_ref, acc_ref, eq_ref = refs[2 * n_pg:]
    bi = pl.program_id(0)
    p = pl.program_id(1)
    hg = B_HEADS // B_KV

    @pl.when(p == 0)
    def _():
        m_ref[...] = jnp.full(m_ref.shape, NEG_BIG, F32)
        l_ref[...] = jnp.zeros(l_ref.shape, F32)
        acc_ref[...] = jnp.zeros(acc_ref.shape, F32)
        eq_ref[...] = jnp.zeros(eq_ref.shape, F32)

    thr = thr_ref[0]
    need = need_ref[0]
    scale = B_HD ** -0.5

    def attend(sc, keys_of, vals_of, tie):
        mask8, eq_new = _select_mask(sc, thr, need, None, eq_ref[...], tie)
        if tie:
            eq_ref[...] = eq_new
        mask = jnp.concatenate([mask8] * hg, axis=0)
        for g in range(B_KV):
            s = jnp.where(mask, _dot3(q_ref[0, g] * scale, keys_of(g), NT), NEG_BIG)
            m_old = m_ref[g]
            m_new = jnp.maximum(m_old, jnp.max(s, axis=-1, keepdims=True))
            pr = jnp.where(mask, jnp.exp(s - m_new), 0.0)
            alpha = jnp.exp(m_old - m_new)
            l_ref[g] = alpha * l_ref[g] + jnp.sum(pr, axis=-1, keepdims=True)
            acc_ref[g] = alpha * acc_ref[g] + _dot3(pr, vals_of(g))
            m_ref[g] = m_new

    def paged(refs_):
        return lambda g: jnp.concatenate([r[0, 0, pl.ds(g, PAGE, stride=B_KV), :] for r in refs_], axis=0)

    def run(tie):
        attend(sc_ref[0], paged(pk), paged(pv), tie)

        @pl.when(p == pl.num_programs(1) - 1)
        def _():
            attend(scn_ref[0], lambda g: nk_ref[0, :, g * B_HD:(g + 1) * B_HD],
                   lambda g: nv_ref[0, :, g * B_HD:(g + 1) * B_HD], tie)

    has_tie = tie_ref[bi] > 0

    @pl.when(has_tie)
    def _():
        run(True)

    @pl.when(jnp.logical_not(has_tie))
    def _():
        run(False)

    @pl.when(p == pl.num_programs(1) - 1)
    def _():
        for g in range(B_KV):
            o_ref[0, g] = acc_ref[g] / l_ref[g]


def _dsa_attn_s(page_table, tie_b, q_s, pool_k2, pool_v2, k_new, v_new, scores, scores_new, thr, need,
                *, layer, n_pg=4):
    b, n_pages = page_table.shape
    hg = B_HEADS // B_KV
    rows = hg * TS_PAD
    kvw = B_KV * B_HD
    grid_spec = pltpu.PrefetchScalarGridSpec(
        num_scalar_prefetch=2,
        grid=(b, n_pages // n_pg),
        in_specs=[pl.BlockSpec((1, B_KV, rows, B_HD), lambda bi, p, pt, tie: (bi, 0, 0, 0))]
        + _page_specs(n_pg, (PAGE * B_KV, B_HD), layer, 2) + _page_specs(n_pg, (PAGE * B_KV, B_HD), layer, 2)
        + [pl.BlockSpec((1, PAGE, kvw), lambda bi, p, pt, tie: (bi, 0, 0)),
           pl.BlockSpec((1, PAGE, kvw), lambda bi, p, pt, tie: (bi, 0, 0)),
           pl.BlockSpec((1, TS_PAD, n_pg * PAGE), lambda bi, p, pt, tie: (bi, 0, p)),
           pl.BlockSpec((1, TS_PAD, PAGE), lambda bi, p, pt, tie: (bi, 0, 0)),
           pl.BlockSpec((1, TS_PAD, 1), lambda bi, p, pt, tie: (bi, 0, 0)),
           pl.BlockSpec((1, TS_PAD, 1), lambda bi, p, pt, tie: (bi, 0, 0))],
        out_specs=pl.BlockSpec((1, B_KV, rows, B_HD), lambda bi, p, pt, tie: (bi, 0, 0, 0)),
        scratch_shapes=[pltpu.VMEM((B_KV, rows, 1), F32), pltpu.VMEM((B_KV, rows, 1), F32),
                        pltpu.VMEM((B_KV, rows, B_HD), F32), pltpu.VMEM((TS_PAD, 1), F32)])
    return pl.pallas_call(
        functools.partial(_dsa_attn_s_kernel, n_pg=n_pg),
        grid_spec=grid_spec,
        out_shape=jax.ShapeDtypeStruct((b, B_KV, rows, B_HD), F32),
        compiler_params=_cp(("arbitrary", "arbitrary")),
        name="dsa_attn_sample",
    )(page_table, tie_b, q_s, *([pool_k2] * n_pg), *([pool_v2] * n_pg), k_new, v_new, scores, scores_new, thr, need)


def _diff_attn_s_kernel(pt_ref, wt_ref, *refs, n_pg, t_new, lam_init):
    pk = refs[:n_pg]
    pv = refs[n_pg:2 * n_pg]
    nk_ref, nv_ref, ex_ref, hm_ref, lam_ref, ng_ref, o_ref, m_ref, l_ref, acc_ref = refs[2 * n_pg:]
    p = pl.program_id(1)
    nrow = wt_ref.shape[1]
    rph = 2 * t_new
    dv = 2 * C_HD

    @pl.when(p == 0)
    def _():
        m_ref[...] = jnp.full(m_ref.shape, NEG_BIG, F32)
        l_ref[...] = jnp.zeros(l_ref.shape, F32)
        acc_ref[...] = jnp.zeros(acc_ref.shape, F32)

    wt = wt_ref[0]

    def attend(s, pv_of):
        m_old = m_ref[...]
        m_new = jnp.maximum(m_old, jnp.max(s, axis=-1, keepdims=True))
        pr = jnp.exp(s - m_new)
        alpha = jnp.exp(m_old - m_new)
        l_ref[...] = alpha * l_ref[...] + jnp.sum(pr, axis=-1, keepdims=True)
        acc_ref[...] = alpha * acc_ref[...] + pv_of(pr.astype(BF16))
        m_ref[...] = m_new

    def pv_paged(prb):
        out = jnp.zeros((nrow, dv), F32)
        for k in range(n_pg):
            spread = _dot(prb[:, k * PAGE:(k + 1) * PAGE], ex_ref[...]) * hm_ref[...]
            out = out + _dot(spread.astype(BF16), pv[k][0, 0].astype(BF16))
        return out

    s_past = _dot(wt, jnp.concatenate([r[0, 0] for r in pk], axis=1).astype(BF16))
    attend(s_past, pv_paged)

    @pl.when(p == pl.num_programs(1) - 1)
    def _():
        tq = lax.broadcasted_iota(I32, (nrow, TS_PAD), 0) & (t_new - 1)
        tk = lax.broadcasted_iota(I32, (nrow, TS_PAD), 1)
        s_new = jnp.where(tk <= tq, _dot(wt, nk_ref[0].astype(BF16), NT), NEG_BIG)
        attend(s_new, lambda prb: jnp.concatenate(
            [_dot(prb[h * rph:(h + 1) * rph, :], nv_ref[0, :, h * dv:(h + 1) * dv].astype(BF16))
             for h in range(C_HEADS)], axis=0))
        lam = _lambda_value(lam_ref, lam_init)
        a = acc_ref[...] / l_ref[...]
        for h in range(C_HEADS):
            r0 = h * rph
            o = a[r0:r0 + t_new, :] - lam * a[r0 + t_new:r0 + rph, :]
            on = o * lax.rsqrt(jnp.mean(o * o, axis=-1, keepdims=True) + EPS) * ng_ref[...]
            o_ref[0, :, h * dv:(h + 1) * dv] = on * (1.0 - lam_init)


def _diff_attn_s(page_table, wt, pool_kt, pool_v2, k_new, v_new, c_lambda, c_norm_g, *, layer, t_new, lam_init,
                 n_pg=4):
    b, n_pages = page_table.shape
    nrow, width = wt.shape[1:]
    dv = 2 * C_HD
    vrows = PAGE * C_HEADS
    vrow = jnp.arange(vrows, dtype=I32)
    expand = (vrow[None, :] // C_HEADS == jnp.arange(PAGE, dtype=I32)[:, None]).astype(BF16)
    own_head = (vrow[None, :] % C_HEADS == jnp.arange(nrow, dtype=I32)[:, None] // (2 * t_new)).astype(F32)
    grid_spec = pltpu.PrefetchScalarGridSpec(
        num_scalar_prefetch=1,
        grid=(b, n_pages // n_pg),
        in_specs=[pl.BlockSpec((1, nrow, width), lambda bi, p, pt: (bi, 0, 0))]
        + _page_specs(n_pg, (width, PAGE), layer, 1) + _page_specs(n_pg, (vrows, dv), layer, 1)
        + [pl.BlockSpec((1, TS_PAD, width), lambda bi, p, pt: (bi, 0, 0)),
           pl.BlockSpec((1, TS_PAD, width), lambda bi, p, pt: (bi, 0, 0)),
           pl.BlockSpec((PAGE, vrows), lambda bi, p, pt: (0, 0)),
           pl.BlockSpec((nrow, vrows), lambda bi, p, pt: (0, 0)),
           pl.BlockSpec((4, C_HD), lambda bi, p, pt: (0, 0)),
           pl.BlockSpec((1, dv), lambda bi, p, pt: (0, 0))],
        out_specs=pl.BlockSpec((1, t_new, width), lambda bi, p, pt: (bi, 0, 0)),
        scratch_shapes=[pltpu.VMEM((nrow, 1), F32), pltpu.VMEM((nrow, 1), F32), pltpu.VMEM((nrow, dv), F32)])
    return pl.pallas_call(
        functools.partial(_diff_attn_s_kernel, n_pg=n_pg, t_new=t_new, lam_init=lam_init),
        grid_spec=grid_spec,
        out_shape=jax.ShapeDtypeStruct((b, t_new, width), F32),
        compiler_params=_cp(("arbitrary", "arbitrary"), 56),
        name="diff_attn_sample",
    )(page_table, wt, *([pool_kt] * n_pg), *([pool_v2] * n_pg), k_new, v_new, expand, own_head, c_lambda,
      c_norm_g.reshape(1, dv))


def _split_ab_weights(w_in):
    n_a = 4 * A_HEADS * A_DK
    n_gate = 2 * A_HEADS
    n_b = (B_HEADS + 2 * B_KV) * B_HD
    n_idx = IDX_HEADS * IDX_HD + IDX_HD + IDX_HEADS
    big = jnp.concatenate([w_in[:, :n_a], w_in[:, n_a + n_gate:n_a + n_gate + n_b]], axis=1)
    pad = 5 * LANES - n_idx - n_gate
    small = jnp.concatenate([w_in[:, n_a + n_gate + n_b:], w_in[:, n_a:n_a + n_gate],
                             jnp.zeros((w_in.shape[0], pad), w_in.dtype)], axis=1)
    return big, small


PROMPT_TM = 2048
GATE_OFF = IDX_HD + IDX_HEADS
QB_COL0 = 4 * A_HEADS * A_DK // LANES
KB_COL0 = QB_COL0 + B_HEADS
VB_COL0 = KB_COL0 + B_KV


def _mlp(x, mods, norm_g, w1, w2, *, tm, hi, tn=512, tk=2048):
    sh2, sc2, g2 = mods
    hid = _norm_matmul(x, norm_g[2:3], sc2, sh2, w1, tm=tm, tn=tn, hi=hi, relu2=True,
                       out_dtype=F32 if hi else BF16, vmem_mb=56)
    if hi:
        return _matmul_norm_res(hid, w2, norm_g[3:4], g2, x, tm=tm, tk=512, hi=True)
    return _matmul_norm_res(hid, w2, norm_g[3:4], g2, x, tm=512, tk=tk, vmem_mb=56)


def _prompt_trunk(x, mods, P):
    t = x.shape[0]
    pos = jnp.arange(t, dtype=I32)
    tab128, tab64 = _rope_tables(pos)
    sh1, sc1, g1, sh2, sc2, g2 = mods[0]
    ng = P["norm_g"][0]
    big = _norm_matmul(x, ng[0:1], sc1, sh1, P["w_big"], tm=PROMPT_TM, tn=256, vmem_mb=56)
    small = _norm_matmul(x, ng[0:1], sc1, sh1, P["w_small"], tm=PROMPT_TM // 2, tn=5 * LANES, hi=True)
    small_r = _rope(small, tab64, col0=0, ngroups=5, n_kind0=4, r1=32, r2=96, tm=1024)
    qb_r = _rope(big, tab128, col0=QB_COL0, ngroups=B_HEADS, n_kind0=B_HEADS, r1=64, r2=None, tm=1024, gw=4)
    kb_r = _rope(big, tab128, col0=KB_COL0, ngroups=B_KV, n_kind0=B_KV, r1=64, r2=None, tm=1024)
    qcat, kcat = _idx_prep(small_r)
    scores = _idx_scores(qcat, kcat, small_r)
    thr, need, tie = _topk_thr(scores, k=min(TOPK, t // 4), cw=1024, causal=True, tr=256)
    tq = 256
    tie_blk = jnp.max(tie.reshape(t // tq, tq), axis=1)
    ob = _dsa_attn(tie_blk, qb_r, kb_r, big, VB_COL0, scores, thr, need, tq=tq)
    oa, s_new = _gdn(big[None], small_r[None], jnp.zeros((1, A_CONV - 1, 3 * A_HEADS * A_DK), F32), P["ab_conv_w"],
                     P["ab_A_log"], P["ab_dt_bias"], P["ab_norm_g"], jnp.zeros((1, A_HEADS, A_DK, A_DK), F32),
                     c=128, t_valid=128, gate_group=4, gate_off=GATE_OFF)
    mix = jnp.concatenate([oa[0], ob], axis=1)
    x = _matmul_norm_res(mix, P["ab_w_out"], ng[1:2], g1, x, tm=512, tk=2048, vmem_mb=56)
    x = _mlp(x, (sh2, sc2, g2), ng, (P["mlp_w1"], 0), (P["mlp_w2"], 0), tm=PROMPT_TM, hi=False)
    outs0 = (kb_r, big[:, VB_COL0 * LANES:], small_r[:, 4 * LANES:4 * LANES + IDX_HD], s_new,
             big[t - (A_CONV - 1):, :3 * A_HEADS * A_DK])
    sh1, sc1, g1, sh2, sc2, g2 = mods[1]
    ng = P["norm_g"][1]
    lam_init = 0.8 - 0.6 * math.exp(-0.3 * 1)
    proj = _norm_matmul(x, ng[0:1], sc1, sh1, P["c_w_in"], tm=PROMPT_TM, tn=256, vmem_mb=56)
    q_r = _rope(proj, tab64, col0=0, ngroups=C_HEADS, n_kind0=C_HEADS, r1=32, r2=96, tm=1024, gw=4)
    k_r = _rope(proj, tab64, col0=C_HEADS, ngroups=C_HEADS, n_kind0=C_HEADS, r1=32, r2=96, tm=1024, gw=4)
    o = _diff_attn(q_r, k_r, proj, 2 * C_HEADS, P["c_lambda"], P["c_norm_g"], lam_init=lam_init)
    x = _matmul_norm_res(o, P["c_w_out"], ng[1:2], g1, x, tm=512, tk=2048, vmem_mb=56)
    x = _mlp(x, (sh2, sc2, g2), ng, (P["mlp_w1"], 1), (P["mlp_w2"], 1), tm=PROMPT_TM, hi=False)
    outs1 = (k_r, proj[:, 4 * C_HEADS * C_HD:])
    return x, outs0, outs1


def _sample_trunk(x, mods, P, cache, past_len):
    b, ts_, d = x.shape
    m = b * ts_
    x = x.reshape(m, d)
    pos = past_len + jnp.arange(ts_, dtype=I32)
    tab128, tab64 = _rope_tables(pos)
    tab128 = jnp.tile(tab128, (b, 1))
    tab64 = jnp.tile(tab64, (b, 1))
    page_table = cache["page_table"]
    n_pages = page_table.shape[1]
    hw = A_HEADS * A_DK

    def rows(a):
        return jnp.repeat(a, ts_, axis=0)

    def pad_t(a, n):
        return jnp.pad(a, ((0, 0), (0, n - ts_), (0, 0)))

    sh1, sc1, g1, sh2, sc2, g2 = (rows(a) for a in mods[0])
    ng = P["norm_g"][0]
    big = _norm_matmul(x, ng[0:1], sc1, sh1, P["w_big"], tm=m, tn=512, hi=True)
    small = _norm_matmul(x, ng[0:1], sc1, sh1, P["w_small"], tm=m, tn=5 * LANES, hi=True)
    small_r = _rope(small, tab64, col0=0, ngroups=5, n_kind0=4, r1=32, r2=96, tm=m)
    qb_r = _rope(big, tab128, col0=QB_COL0, ngroups=B_HEADS, n_kind0=B_HEADS, r1=64, r2=None, tm=m)
    kb_r = _rope(big, tab128, col0=KB_COL0, ngroups=B_KV, n_kind0=B_KV, r1=64, r2=None, tm=m)
    vb = big[:, VB_COL0 * LANES:]
    sm3 = small_r.reshape(b, ts_, 5 * LANES)
    qi = sm3[:, :, :IDX_HEADS * IDX_HD].reshape(b, ts_, IDX_HEADS, IDX_HD)
    qi = jnp.pad(jnp.swapaxes(qi, 1, 2), ((0, 0), (0, 0), (0, TS_PAD - ts_), (0, 0)))
    qi = qi.reshape(b, IDX_HEADS * TS_PAD, IDX_HD)
    wi = sm3[:, :, 4 * LANES + W_OFF:4 * LANES + W_OFF + IDX_HEADS]
    wi = jnp.pad(jnp.swapaxes(wi, 1, 2), ((0, 0), (0, 0), (0, TS_PAD - ts_))).reshape(b, IDX_HEADS * TS_PAD, 1)
    ki_r = sm3[:, :, 4 * LANES:4 * LANES + IDX_HD]
    pool_ki_t = jnp.swapaxes(cache["ab_kidx"], 2, 3)
    sc_past, sc_new = _idx_scores_s(page_table, qi, wi, pool_ki_t, pad_t(ki_r, PAGE), layer=0, t_new=ts_,
                                    n_pg=min(16, n_pages))
    scores = jnp.concatenate([sc_past, sc_new], axis=2)
    ncols = scores.shape[2]
    k_sel = min(TOPK, (past_len + ts_) // 4)
    chunk = min(d for d in range(1, ncols // LANES + 1) if (ncols // LANES) % d == 0 and d * LANES >= k_sel) * LANES
    thr, need, tie = _topk_thr(scores.reshape(b * TS_PAD, ncols), k=k_sel, cw=chunk, causal=False, tr=64)
    tie_b = jnp.max(tie.reshape(b, TS_PAD)[:, :ts_], axis=1)
    hg = B_HEADS // B_KV
    q4 = qb_r.reshape(b, ts_, B_KV, hg, B_HD)
    q4 = jnp.pad(jnp.transpose(q4, (0, 2, 3, 1, 4)), ((0, 0), (0, 0), (0, 0), (0, TS_PAD - ts_), (0, 0)))
    q4 = q4.reshape(b, B_KV, hg * TS_PAD, B_HD)
    kvw = B_KV * B_HD
    pool_k2 = cache["ab_k"].reshape(cache["ab_k"].shape[:2] + (PAGE * B_KV, B_HD))
    pool_v2 = cache["ab_v"].reshape(cache["ab_v"].shape[:2] + (PAGE * B_KV, B_HD))
    ob = _dsa_attn_s(page_table, tie_b, q4, pool_k2, pool_v2, pad_t(kb_r.reshape(b, ts_, kvw), PAGE),
                     pad_t(vb.reshape(b, ts_, kvw), PAGE), sc_past, sc_new, thr.reshape(b, TS_PAD, 1),
                     need.reshape(b, TS_PAD, 1), layer=0, n_pg=min(16, n_pages))
    ob = ob.reshape(b, B_KV, hg, TS_PAD, B_HD)[:, :, :, :ts_]
    ob = jnp.transpose(ob, (0, 3, 1, 2, 4)).reshape(m, B_HEADS * B_HD)
    oa, s_new = _gdn(pad_t(big.reshape(b, ts_, -1), TS_PAD), pad_t(sm3, TS_PAD), cache["ab_conv"][0], P["ab_conv_w"],
                     P["ab_A_log"], P["ab_dt_bias"], P["ab_norm_g"], cache["ab_delta"][0],
                     c=TS_PAD, t_valid=ts_, gate_group=4, gate_off=GATE_OFF)
    mix = jnp.concatenate([oa[:, :ts_].reshape(m, hw), ob], axis=1)
    x = _matmul_norm_res(mix, P["ab_w_out"], ng[1:2], g1, x, tm=m, tk=512, hi=True)
    x = _mlp(x, (sh2, sc2, g2), ng, (P["mlp_w1"], 0), (P["mlp_w2"], 0), tm=m, hi=True)
    conv_in = jnp.concatenate([cache["ab_conv"][0], big.reshape(b, ts_, -1)[:, :, :3 * hw]], axis=1)
    outs0 = (kb_r, vb, ki_r, s_new, conv_in[:, ts_:])
    sh1, sc1, g1, sh2, sc2, g2 = (rows(a) for a in mods[1])
    ng = P["norm_g"][1]
    lam_init = 0.8 - 0.6 * math.exp(-0.3 * 1)
    proj = _norm_matmul(x, ng[0:1], sc1, sh1, P["c_w_in"], tm=m, tn=512, hi=True)
    qk_r = _rope(proj, tab64, col0=0, ngroups=2 * C_HEADS, n_kind0=2 * C_HEADS, r1=32, r2=96, tm=m)
    cw = 2 * C_HEADS * C_HD
    q3 = qk_r[:, :cw].reshape(b, ts_, cw) * C_HD ** -0.5
    k3 = qk_r[:, cw:].reshape(b, ts_, cw)
    v3 = proj[:, 2 * cw:].reshape(b, ts_, cw)
    lane_hp = jnp.arange(cw, dtype=I32) // C_HD
    want = (2 * jnp.arange(C_HEADS, dtype=I32)[:, None] + jnp.arange(2, dtype=I32)[None, :])
    sel = (lane_hp[None, None, :] == want[:, :, None]).astype(F32)
    wt = (q3[:, None, None, :, :] * sel[None, :, :, None, :]).reshape(b, 2 * C_HEADS * ts_, cw).astype(BF16)
    ck = cache["c_k"]
    pool_ckt = jnp.transpose(ck, (0, 1, 3, 4, 5, 2)).reshape(ck.shape[:2] + (cw, PAGE))
    pool_cv2 = cache["c_v"].reshape(cache["c_v"].shape[:2] + (PAGE * C_HEADS, 2 * C_HD))
    o = _diff_attn_s(page_table, wt, pool_ckt, pool_cv2, pad_t(k3, TS_PAD), pad_t(v3, TS_PAD), P["c_lambda"],
                     P["c_norm_g"], layer=0, t_new=ts_, lam_init=lam_init, n_pg=min(8, n_pages))
    x = _matmul_norm_res(o.reshape(m, cw), P["c_w_out"], ng[1:2], g1, x, tm=m, tk=512, hi=True)
    x = _mlp(x, (sh2, sc2, g2), ng, (P["mlp_w1"], 1), (P["mlp_w2"], 1), tm=m, hi=True)
    outs1 = (qk_r[:, cw:], proj[:, 2 * cw:])
    return x.reshape(b, ts_, d), outs0, outs1


def kernel(x_prompt, x_sample, cache_ab_k, cache_ab_v, cache_ab_kidx, state_ab_delta, state_ab_conv, cache_c_k,
           cache_c_v, page_table, c_prompt, c_sample, ada_w, ada_b, norm_g, mlp_w1, mlp_w2, ab_w_in, ab_w_out,
           ab_conv_w, ab_A_log, ab_dt_bias, ab_norm_g, c_w_in, c_w_out, c_lambda, c_norm_g):
    bp, t, d = x_prompt.shape
    bs, ts_, _ = x_sample.shape
    assert bp == 1
    past_len = page_table.shape[1] * PAGE
    w_big, w_small = _split_ab_weights(ab_w_in[0])
    P = {"norm_g": norm_g, "mlp_w1": mlp_w1, "mlp_w2": mlp_w2, "w_big": w_big, "w_small": w_small,
         "ab_w_out": ab_w_out[0], "ab_conv_w": ab_conv_w[0], "ab_A_log": ab_A_log[0], "ab_dt_bias": ab_dt_bias[0],
         "ab_norm_g": ab_norm_g[0], "c_w_in": c_w_in[0], "c_w_out": c_w_out[0], "c_lambda": c_lambda[0],
         "c_norm_g": c_norm_g[0]}
    n_seq = bp + bs
    mc = -(-n_seq // 8) * 8
    c_all = jnp.pad(jnp.concatenate([c_prompt, c_sample], axis=0), ((0, mc - n_seq), (0, 0)))
    mod = _ada(c_all, ada_w, ada_b)
    mods_p = [tuple(mod[i, 0:bp, n * d:(n + 1) * d] for n in range(6)) for i in range(2)]
    mods_s = [tuple(mod[i, bp:n_seq, n * d:(n + 1) * d] for n in range(6)) for i in range(2)]

    y_p, p0, p1 = _prompt_trunk(x_prompt[0], mods_p, P)
    cache = {"ab_k": cache_ab_k, "ab_v": cache_ab_v, "ab_kidx": cache_ab_kidx, "ab_delta": state_ab_delta,
             "ab_conv": state_ab_conv, "c_k": cache_c_k, "c_v": cache_c_v, "page_table": page_table}
    y_s, s0, s1 = _sample_trunk(x_sample, mods_s, P, cache, past_len)

    return (y_p[None], y_s,
            p0[0].reshape(1, 1, t, B_KV, B_HD), p0[1].reshape(1, 1, t, B_KV, B_HD), p0[2].reshape(1, 1, t, IDX_HD),
            p0[3][None], p0[4].reshape(1, 1, A_CONV - 1, -1),
            p1[0].reshape(1, 1, t, C_HEADS, 2, C_HD), p1[1].reshape(1, 1, t, C_HEADS, 2 * C_HD),
            s0[0].reshape(1, bs, ts_, B_KV, B_HD), s0[1].reshape(1, bs, ts_, B_KV, B_HD),
            s0[2].reshape(1, bs, ts_, IDX_HD), s0[3][None], s0[4][None],
            s1[0].reshape(1, bs, ts_, C_HEADS, 2, C_HD), s1[1].reshape(1, bs, ts_, C_HEADS, 2 * C_HD))
```

```python
import functools
import math

import jax
import jax.numpy as jnp
from jax import lax
from jax.experimental import pallas as pl
from jax.experimental.pallas import tpu as pltpu

F32 = jnp.float32
BF16 = jnp.bfloat16
I32 = jnp.int32

EPS = 1e-6
NEG_BIG = -1e30
ROPE_THETA = 10000.0
PAGE = 128
A_HEADS = 8
A_DK = 128
A_CONV = 4
B_HEADS = 8
B_KV = 2
B_HD = 128
IDX_HEADS = 8
IDX_HD = 64
TOPK = 256
C_HEADS = 16
C_HD = 64
LANES = 128

NN = (((1,), (0,)), ((), ()))
NT = (((1,), (1,)), ((), ()))
TN = (((0,), (0,)), ((), ()))
BNN = (((2,), (1,)), ((0,), (0,)))
BNT = (((2,), (2,)), ((0,), (0,)))
INT_MIN = -2147483648
KEY_NEG_INF = -2139095041


def _cp(dims, vmem_mb=None):
    kw = dict(dimension_semantics=dims)
    if vmem_mb is not None:
        kw["vmem_limit_bytes"] = vmem_mb << 20
    return pltpu.CompilerParams(**kw)


def _dot(a, b, dims=NN):
    return lax.dot_general(a, b, dims, preferred_element_type=F32)


def _split2(a):
    hi = a.astype(BF16)
    return hi, (a - hi.astype(F32)).astype(BF16)


def _split3(a):
    hi = a.astype(BF16)
    r = a - hi.astype(F32)
    mid = r.astype(BF16)
    return hi, mid, (r - mid.astype(F32)).astype(BF16)


def _dot1(a, b, dims=NN):
    return _dot(a.astype(BF16), b.astype(BF16), dims)


def _dot3(a, b, dims=NN):
    ah, al = _split2(a)
    bh, bl = _split2(b)
    return _dot(ah, bh, dims) + (_dot(ah, bl, dims) + _dot(al, bh, dims))


def _dot_exact_left(ones_bf16, b, dims=NN):
    b0, b1, b2 = _split3(b)
    return _dot(ones_bf16, b0, dims) + (_dot(ones_bf16, b1, dims) + _dot(ones_bf16, b2, dims))


def _sigmoid(x):
    return 1.0 / (1.0 + jnp.exp(-x))


def _silu(x):
    return x * _sigmoid(x)


def _ada_kernel(c_ref, w_ref, b_ref, o_ref):
    o_ref[0] = _dot3(_silu(c_ref[...]), w_ref[0]) + b_ref[0]


def _ada(c_all, ada_w, ada_b, tn=512):
    nl, d, n = ada_w.shape
    mc = c_all.shape[0]
    return pl.pallas_call(
        _ada_kernel,
        grid=(nl, n // tn),
        in_specs=[pl.BlockSpec((mc, d), lambda l, j: (0, 0)),
                  pl.BlockSpec((1, d, tn), lambda l, j: (l, 0, j)),
                  pl.BlockSpec((1, 1, tn), lambda l, j: (l, 0, j))],
        out_specs=pl.BlockSpec((1, mc, tn), lambda l, j: (l, 0, j)),
        out_shape=jax.ShapeDtypeStruct((nl, mc, n), F32),
        compiler_params=_cp(("arbitrary", "arbitrary"), 40),
        name="ada_mod",
    )(c_all, ada_w, ada_b.reshape(nl, 1, n))


def _norm_mm_kernel(x_ref, g_ref, sc_ref, sh_ref, w_ref, o_ref, *scratch, hi, relu2):
    hh_ref = scratch[0]

    @pl.when(pl.program_id(1) == 0)
    def _():
        x = x_ref[...]
        y = x * lax.rsqrt(jnp.mean(x * x, axis=-1, keepdims=True) + EPS)
        h = (y * g_ref[...]) * (1.0 + sc_ref[...]) + sh_ref[...]
        hh = h.astype(BF16)
        hh_ref[...] = hh
        if hi:
            scratch[1][...] = (h - hh.astype(F32)).astype(BF16)

    w = w_ref[0]
    wh = w.astype(BF16)
    acc = _dot(hh_ref[...], wh)
    if hi:
        wl = (w - wh.astype(F32)).astype(BF16)
        acc = acc + (_dot(hh_ref[...], wl) + _dot(scratch[1][...], wh))
    if relu2:
        acc = jnp.square(jnp.maximum(acc, 0.0))
    o_ref[...] = acc.astype(o_ref.dtype)


def _layer_weight(w):
    return w if isinstance(w, tuple) else (w[None], 0)


def _norm_matmul(x, g, sc, sh, w, *, tm, tn, hi=False, relu2=False, out_dtype=F32, vmem_mb=48):
    m, d = x.shape
    tm = min(tm, m)
    w, layer = _layer_weight(w)
    n = w.shape[2]
    per_row = sc.shape[0] != 1
    mod_spec = pl.BlockSpec((tm, d), lambda i, j: (i, 0)) if per_row else pl.BlockSpec((1, d), lambda i, j: (0, 0))
    scratch = [pltpu.VMEM((tm, d), BF16)] + ([pltpu.VMEM((tm, d), BF16)] if hi else [])
    return pl.pallas_call(
        functools.partial(_norm_mm_kernel, hi=hi, relu2=relu2),
        grid=(m // tm, n // tn),
        in_specs=[pl.BlockSpec((tm, d), lambda i, j: (i, 0), pipeline_mode=pl.Buffered(1)),
                  pl.BlockSpec((1, d), lambda i, j: (0, 0)),
                  mod_spec, mod_spec,
                  pl.BlockSpec((1, d, tn), lambda i, j: (layer, 0, j))],
        out_specs=pl.BlockSpec((tm, tn), lambda i, j: (i, j)),
        out_shape=jax.ShapeDtypeStruct((m, n), out_dtype),
        scratch_shapes=scratch,
        compiler_params=_cp(("arbitrary", "arbitrary"), vmem_mb),
        name="norm_matmul",
    )(x, g, sc, sh, w)


def _mm_norm_res_kernel(a_ref, w_ref, ng_ref, gate_ref, res_ref, o_ref, *, hi):
    k = pl.program_id(1)

    @pl.when(k == 0)
    def _():
        o_ref[...] = jnp.zeros(o_ref.shape, F32)

    if hi:
        o_ref[...] += _dot3(a_ref[...].astype(F32), w_ref[0])
    else:
        o_ref[...] += _dot1(a_ref[...], w_ref[0])

    @pl.when(k == pl.num_programs(1) - 1)
    def _():
        m = o_ref[...]
        y = m * lax.rsqrt(jnp.mean(m * m, axis=-1, keepdims=True) + EPS)
        o_ref[...] = res_ref[...] + gate_ref[...] * (y * ng_ref[...])


def _matmul_norm_res(a, w, ng, gate, res, *, tm, tk, hi=False, vmem_mb=48):
    m, kdim = a.shape
    w, layer = _layer_weight(w)
    n = w.shape[2]
    per_row = gate.shape[0] != 1
    gate_spec = pl.BlockSpec((tm, n), lambda i, k: (i, 0)) if per_row else pl.BlockSpec((1, n), lambda i, k: (0, 0))
    return pl.pallas_call(
        functools.partial(_mm_norm_res_kernel, hi=hi),
        grid=(m // tm, kdim // tk),
        in_specs=[pl.BlockSpec((tm, tk), lambda i, k: (i, k)),
                  pl.BlockSpec((1, tk, n), lambda i, k: (layer, k, 0)),
                  pl.BlockSpec((1, n), lambda i, k: (0, 0)),
                  gate_spec,
                  pl.BlockSpec((tm, n), lambda i, k: (i, 0), pipeline_mode=pl.Buffered(1))],
        out_specs=pl.BlockSpec((tm, n), lambda i, k: (i, 0)),
        out_shape=jax.ShapeDtypeStruct((m, n), F32),
        compiler_params=_cp(("arbitrary", "arbitrary"), vmem_mb),
        name="matmul_norm_res",
    )(a, w, ng, gate, res)


def _rope_kernel(x_ref, tab_ref, o_ref, *, r1, r2, gw):
    for k in range(gw):
        x = x_ref[:, k * LANES:(k + 1) * LANES]
        out = x * tab_ref[:, 0:LANES] + pltpu.roll(x, r1, 1) * tab_ref[:, LANES:2 * LANES]
        if r2 is not None:
            out = out + pltpu.roll(x, r2, 1) * tab_ref[:, 2 * LANES:3 * LANES]
        o_ref[:, k * LANES:(k + 1) * LANES] = out


def _rope(x, tab, *, col0, ngroups, n_kind0, r1, r2, tm, gw=1):
    t = x.shape[0]
    assert col0 % gw == 0 and ngroups % gw == 0 and n_kind0 % gw == 0
    return pl.pallas_call(
        functools.partial(_rope_kernel, r1=r1, r2=r2, gw=gw),
        grid=(t // tm, ngroups // gw),
        in_specs=[pl.BlockSpec((tm, gw * LANES), lambda i, j: (i, col0 // gw + j)),
                  pl.BlockSpec((tm, 3 * LANES), lambda i, j: (i, jnp.where(j * gw >= n_kind0, 1, 0)))],
        out_specs=pl.BlockSpec((tm, gw * LANES), lambda i, j: (i, j)),
        out_shape=jax.ShapeDtypeStruct((t, ngroups * LANES), F32),
        compiler_params=_cp(("arbitrary", "arbitrary")),
        name="rope",
    )(x, tab)


def _rope_tables(pos):
    p = pos.astype(F32)[:, None]

    def cs(half):
        inv = ROPE_THETA ** (-jnp.arange(half, dtype=F32) / half)
        ang = p * inv[None, :]
        return jnp.cos(ang), jnp.sin(ang)

    c, s = cs(64)
    tab128 = jnp.concatenate([c, c, -s, s, jnp.zeros_like(c), jnp.zeros_like(c)], axis=1)
    c, s = cs(32)
    z = jnp.zeros_like(c)
    one64 = jnp.ones((pos.shape[0], 64), F32)
    z64 = jnp.zeros((pos.shape[0], 64), F32)
    kind0 = jnp.concatenate([c, c, c, c, z, s, z, s, -s, z, -s, z], axis=1)
    kind1 = jnp.concatenate([c, c, one64, z, s, z64, -s, z, z64], axis=1)
    return tab128, jnp.concatenate([kind0, kind1], axis=1)


def _gdn_kernel(qkv_ref, z_ref, gate_ref, buf_ref, cw_ref, alog_ref, dtb_ref, ng_ref, s0_ref,
                o_ref, sout_ref, xbuf_ref, s_ref, *, c, t_valid, gate_off):
    ci = pl.program_id(1)
    nh, dk = A_HEADS, A_DK
    hw = nh * dk

    @pl.when(ci == 0)
    def _():
        s_ref[...] = s0_ref[0]
        xbuf_ref[0:8, :] = jnp.zeros((8, 3 * hw), F32)
        xbuf_ref[8 - (A_CONV - 1):8, :] = buf_ref[0]

    xbuf_ref[8:8 + c, :] = qkv_ref[0]
    y = xbuf_ref[5:5 + c, :] * cw_ref[0:1, :]
    for j in range(1, A_CONV):
        y = y + xbuf_ref[5 + j:5 + j + c, :] * cw_ref[j:j + 1, :]
    tail = xbuf_ref[8 + c - 3:8 + c, :]
    xbuf_ref[5:8, :] = tail
    y = _silu(y)

    gt = gate_ref[0]
    ba = gt[:, gate_off:gate_off + nh]
    aa = gt[:, gate_off + nh:gate_off + 2 * nh]
    beta = _sigmoid(ba)
    xs = aa + dtb_ref[...]
    softplus = jnp.maximum(xs, 0.0) + jnp.log1p(jnp.exp(-jnp.abs(xs)))
    g = -jnp.exp(alog_ref[...]) * softplus
    row = lax.broadcasted_iota(I32, (c, c), 0)
    col = lax.broadcasted_iota(I32, (c, c), 1)
    if t_valid < c:
        valid = lax.broadcasted_iota(I32, (c, nh), 0) < t_valid
        beta = jnp.where(valid, beta, 0.0)
        g = jnp.where(valid, g, 0.0)
    incl = (row >= col)[None]
    strict = (row > col)[None]
    tri = jnp.where(row >= col, 1.0, 0.0).astype(BF16)
    gc = _dot_exact_left(tri, g)
    eye_h = jnp.where(lax.broadcasted_iota(I32, (nh, nh), 0) == lax.broadcasted_iota(I32, (nh, nh), 1),
                      1.0, 0.0).astype(BF16)
    gct = _dot_exact_left(eye_h, gc, NT)
    eye_c = jnp.where(row == col, 1.0, 0.0)[None]

    def heads(a, off):
        return jnp.stack([a[:, off + h * dk:off + (h + 1) * dk] for h in range(nh)], axis=0)

    def cols(a):
        return jnp.stack([a[:, h:h + 1] for h in range(nh)], axis=0)

    q3 = heads(y, 0)
    k3 = heads(y, hw)
    v3 = heads(y, 2 * hw)
    q3 = q3 * lax.rsqrt(jnp.sum(q3 * q3, axis=-1, keepdims=True) + 1e-6)
    k3 = k3 * lax.rsqrt(jnp.sum(k3 * k3, axis=-1, keepdims=True) + 1e-6)
    b3 = cols(beta)
    gch = cols(gc)
    gct3 = jnp.stack([gct[h:h + 1, :] for h in range(nh)], axis=0)
    decay = jnp.where(incl, jnp.exp(jnp.minimum(gch - gct3, 0.0)), 0.0)
    kb = k3 * b3
    vb = v3 * b3
    a_mat = jnp.where(strict, _dot3(kb, k3, BNT) * decay, 0.0)
    def same_block(s):
        sh = int(round(math.log2(s)))
        return (jnp.right_shift(row, sh) == jnp.right_shift(col, sh))[None]

    s_blk = min(8, c)
    pw = jnp.where(same_block(s_blk), -a_mat, 0.0)
    t_inv = eye_c + pw
    for _ in range(int(round(math.log2(s_blk))) - 1):
        pw = _dot3(pw, pw, BNN)
        t_inv = t_inv + _dot3(t_inv, pw, BNN)
    while s_blk < c:
        off = jnp.where(jnp.logical_and(same_block(2 * s_blk), jnp.logical_not(same_block(s_blk))), a_mat, 0.0)
        t_inv = t_inv - _dot3(t_inv, _dot3(off, t_inv, BNN), BNN)
        s_blk *= 2
    egc = jnp.exp(gch)
    u = _dot3(t_inv, vb, BNN)
    w = _dot3(t_inv, kb * egc, BNN)
    qs = q3 * dk ** -0.5
    qk = _dot3(qs, k3, BNT) * decay
    g_last = gch[:, c - 1:c, :]
    k_dec = k3 * jnp.exp(jnp.minimum(g_last - gch, 0.0))
    s_old = s_ref[...]
    v_new = u - _dot3(w, s_old, BNN)
    o = _dot3(qs * egc, s_old, BNN) + _dot3(qk, v_new, BNN)
    eg_last = jnp.exp(g_last)
    on = o * lax.rsqrt(jnp.mean(o * o, axis=-1, keepdims=True) + EPS) * ng_ref[...]
    for h in range(nh):
        s_ref[h] = s_old[h] * eg_last[h] + _dot3(k_dec[h], v_new[h], TN)
        o_ref[0, :, h * dk:(h + 1) * dk] = on[h] * _silu(z_ref[0, :, h * dk:(h + 1) * dk])

    @pl.when(ci == pl.num_programs(1) - 1)
    def _():
        sout_ref[0] = s_ref[...]


def _gdn(proj, small, conv_buf, conv_w, a_log, dt_bias, norm_g, s0, *, c, t_valid, gate_group, gate_off):
    b, t = proj.shape[:2]
    nh, dk = A_HEADS, A_DK
    hw = nh * dk
    return pl.pallas_call(
        functools.partial(_gdn_kernel, c=c, t_valid=t_valid, gate_off=gate_off),
        grid=(b, t // c),
        in_specs=[pl.BlockSpec((1, c, 3 * hw), lambda bi, ci: (bi, ci, 0)),
                  pl.BlockSpec((1, c, hw), lambda bi, ci: (bi, ci, 3)),
                  pl.BlockSpec((1, c, LANES), lambda bi, ci: (bi, ci, gate_group)),
                  pl.BlockSpec((1, A_CONV - 1, 3 * hw), lambda bi, ci: (bi, 0, 0)),
                  pl.BlockSpec((A_CONV, 3 * hw), lambda bi, ci: (0, 0)),
                  pl.BlockSpec((1, nh), lambda bi, ci: (0, 0)),
                  pl.BlockSpec((1, nh), lambda bi, ci: (0, 0)),
                  pl.BlockSpec((1, dk), lambda bi, ci: (0, 0)),
                  pl.BlockSpec((1, nh, dk, dk), lambda bi, ci: (bi, 0, 0, 0))],
        out_specs=[pl.BlockSpec((1, c, hw), lambda bi, ci: (bi, ci, 0)),
                   pl.BlockSpec((1, nh, dk, dk), lambda bi, ci: (bi, 0, 0, 0))],
        out_shape=[jax.ShapeDtypeStruct((b, t, hw), F32), jax.ShapeDtypeStruct((b, nh, dk, dk), F32)],
        scratch_shapes=[pltpu.VMEM((8 + c, 3 * hw), F32), pltpu.VMEM((nh, dk, dk), F32)],
        compiler_params=_cp(("arbitrary", "arbitrary"), 40),
        name="gdn",
    )(proj, proj, small, conv_buf, conv_w, a_log.reshape(1, nh), dt_bias.reshape(1, nh), norm_g.reshape(1, dk), s0)


def _idx_prep_kernel(x_ref, q_ref, k_ref):
    tm = x_ref.shape[0]
    lo_half = lax.broadcasted_iota(I32, (tm, LANES), 1) < 64

    def split(xg):
        hi = xg.astype(BF16).astype(F32)
        return hi, xg - hi

    for gq in range(IDX_HEADS // 2):
        hi, lo = split(x_ref[:, gq * LANES:(gq + 1) * LANES])
        hi_r = pltpu.roll(hi, 64, 1)
        lo_r = pltpu.roll(lo, 64, 1)
        base = 2 * gq * 256
        q_ref[:, base:base + 128] = jnp.where(lo_half, hi, hi_r).astype(BF16)
        q_ref[:, base + 128:base + 256] = jnp.where(lo_half, lo, lo_r).astype(BF16)
        q_ref[:, base + 256:base + 384] = jnp.where(lo_half, hi_r, hi).astype(BF16)
        q_ref[:, base + 384:base + 512] = jnp.where(lo_half, lo_r, lo).astype(BF16)
    hi, lo = split(x_ref[:, 4 * LANES:5 * LANES])
    kk = jnp.where(lo_half, hi, pltpu.roll(lo, 64, 1)).astype(BF16)
    k_ref[:, 0:128] = kk
    k_ref[:, 128:256] = kk


def _idx_prep(small_r, tm=512):
    t = small_r.shape[0]
    return pl.pallas_call(
        _idx_prep_kernel,
        grid=(t // tm,),
        in_specs=[pl.BlockSpec((tm, 5 * LANES), lambda i: (i, 0))],
        out_specs=[pl.BlockSpec((tm, IDX_HEADS * 256), lambda i: (i, 0)),
                   pl.BlockSpec((tm, 256), lambda i: (i, 0))],
        out_shape=[jax.ShapeDtypeStruct((t, IDX_HEADS * 256), BF16), jax.ShapeDtypeStruct((t, 256), BF16)],
        compiler_params=_cp(("arbitrary",)),
        name="idx_prep",
    )(small_r)


IDX_SCALE = IDX_HEADS ** -0.5 * IDX_HD ** -0.5
W_OFF = 64


def _idx_scores_kernel(q_ref, k_ref, w_ref, o_ref, *, tq, ts):
    i = pl.program_id(0)
    j = pl.program_id(1)
    live = j * ts <= i * tq + tq - 1

    @pl.when(live)
    def _():
        k = k_ref[...]
        wg = w_ref[...] * IDX_SCALE
        acc = jnp.zeros((tq, ts), F32)
        for h in range(IDX_HEADS):
            s = _dot(q_ref[:, h * 256:(h + 1) * 256], k, NT)
            acc = acc + wg[:, W_OFF + h:W_OFF + h + 1] * jnp.maximum(s, 0.0)
        row = i * tq + lax.broadcasted_iota(I32, (tq, ts), 0)
        col = j * ts + lax.broadcasted_iota(I32, (tq, ts), 1)
        o_ref[...] = jnp.where(col <= row, acc, -jnp.inf)

    @pl.when(jnp.logical_not(live))
    def _():
        o_ref[...] = jnp.full((tq, ts), -jnp.inf, F32)


def _idx_scores(qcat, kcat, small_r, *, tq=256, ts=512):
    t = qcat.shape[0]
    return pl.pallas_call(
        functools.partial(_idx_scores_kernel, tq=tq, ts=ts),
        grid=(t // tq, t // ts),
        in_specs=[pl.BlockSpec((tq, IDX_HEADS * 256), lambda i, j: (i, 0)),
                  pl.BlockSpec((ts, 256), lambda i, j: (j, 0)),
                  pl.BlockSpec((tq, LANES), lambda i, j: (i, 4))],
        out_specs=pl.BlockSpec((tq, ts), lambda i, j: (i, j)),
        out_shape=jax.ShapeDtypeStruct((t, t), F32),
        compiler_params=_cp(("arbitrary", "arbitrary")),
        name="idx_scores",
    )(qcat, kcat, small_r)


def _thr_kernel(s_ref, thr_ref, need_ref, tie_ref, key_ref, *, k, cw, causal):
    tr, s_cols = s_ref.shape
    nchunks = ((pl.program_id(0) + 1) * tr + cw - 1) // cw if causal else s_cols // cw
    kf = float(k)

    def fill(ci, carry):
        off = pl.multiple_of(ci * cw, cw)
        bits = pltpu.bitcast(s_ref[:, pl.ds(off, cw)] + 0.0, I32)
        key_ref[:, pl.ds(off, cw)] = jnp.where(bits < 0, bits ^ 0x7FFFFFFF, bits)
        return carry

    lax.fori_loop(0, nchunks, fill, 0)

    def count(cand, strict):
        def cbody(ci, acc):
            off = pl.multiple_of(ci * cw, cw)
            blk = key_ref[:, pl.ds(off, cw)]
            for t in range(cw // LANES):
                kk = blk[:, t * LANES:(t + 1) * LANES]
                acc = acc + jnp.where(kk > cand if strict else kk >= cand, 1.0, 0.0)
            return acc

        acc = lax.fori_loop(0, nchunks, cbody, jnp.zeros((tr, LANES), F32))
        return jnp.sum(acc, axis=-1, keepdims=True)

    cnt0 = count(0, False)
    p0 = jnp.where(cnt0 >= kf, 0, INT_MIN).astype(I32)
    visited = (nchunks * cw).astype(F32) if causal else float(s_cols)
    cnt_p0 = jnp.where(cnt0 >= kf, cnt0, visited)

    def cond(state):
        b, _, cnt_p = state
        return jnp.logical_and(b < 31, jnp.max(cnt_p) > kf)

    def body(state):
        b, p, cnt_p = state
        cand = p | jnp.left_shift(jnp.int32(1), 30 - b)
        cnt = count(cand, False)
        take = cnt >= kf
        return b + 1, jnp.where(take, cand, p), jnp.where(take, cnt, cnt_p)

    _, p, _ = lax.while_loop(cond, body, (jnp.int32(0), p0, cnt_p0))
    p = jnp.maximum(p, KEY_NEG_INF)
    thr_ref[...] = pltpu.bitcast(jnp.where(p < 0, p ^ 0x7FFFFFFF, p), F32)
    need_ref[...] = kf - count(p, True)
    tie_ref[...] = jnp.where(jnp.logical_and(count(p, False) > kf, p != KEY_NEG_INF), 1, 0).astype(I32)


def _topk_thr(scores, *, k, cw, causal, tr):
    r, s = scores.shape
    tr = min(tr, r)
    assert cw >= k and s % cw == 0
    return pl.pallas_call(
        functools.partial(_thr_kernel, k=k, cw=cw, causal=causal),
        grid=(r // tr,),
        in_specs=[pl.BlockSpec((tr, s), lambda i: (i, 0))],
        out_specs=[pl.BlockSpec((tr, 1), lambda i: (i, 0))] * 3,
        out_shape=[jax.ShapeDtypeStruct((r, 1), F32), jax.ShapeDtypeStruct((r, 1), F32),
                   jax.ShapeDtypeStruct((r, 1), I32)],
        scratch_shapes=[pltpu.VMEM((tr, s), I32)],
        compiler_params=_cp(("arbitrary",), 40),
        name="topk_thr",
    )(scores)


def _select_mask(sc, thr, need, causal, eq_before, tie):
    def visible(m, scc, c0):
        if isinstance(causal, str):
            return m
        return jnp.logical_and(m, scc > -jnp.inf if causal is None else causal[:, c0:c0 + scc.shape[1]])

    if not tie:
        return visible(sc >= thr, sc, 0), None
    n = sc.shape[1]
    cs = min(n, 512)
    upper = jnp.where(lax.broadcasted_iota(I32, (cs, cs), 0) < lax.broadcasted_iota(I32, (cs, cs), 1),
                      1.0, 0.0).astype(BF16)
    masks = []
    for c0 in range(0, n, cs):
        scc = sc[:, c0:c0 + cs]
        eq = scc == thr
        eqf = jnp.where(eq, 1.0, 0.0)
        rank = eq_before + _dot(eqf.astype(BF16), upper)
        sel = jnp.logical_or(scc > thr, jnp.logical_and(eq, rank < need))
        masks.append(visible(sel, scc, c0))
        eq_before = eq_before + jnp.sum(eqf, axis=-1, keepdims=True)
    if len(masks) == 1:
        return masks[0], eq_before
    return jnp.concatenate([jnp.where(m, 1.0, 0.0) for m in masks], axis=1) > 0.5, eq_before


def _flash_update(s, m_ref, acc_ref, r0, rows, vblk):
    ts = s.shape[1]
    m_old = m_ref[r0:r0 + rows, :]
    m_new = jnp.maximum(m_old, jnp.max(s, axis=-1, keepdims=True))
    p = jnp.exp(s - jnp.tile(m_new, (1, ts // LANES)))
    alpha = jnp.exp(m_old - m_new)
    acc_ref[r0:r0 + rows, :] = (jnp.tile(alpha, (1, acc_ref.shape[1] // LANES)) * acc_ref[r0:r0 + rows, :]
                                + _dot(p.astype(BF16), vblk))
    m_ref[r0:r0 + rows, :] = m_new


def _dsa_attn_kernel(tie_ref, q_ref, k_ref, v_ref, sc_ref, thr_ref, need_ref, o_ref,
                     kb_ref, vb_ref, qs_ref, m_ref, acc_ref, eq_ref, *, tq, ts):
    i = pl.program_id(1)
    hg = B_HEADS // B_KV
    d = B_HD

    @pl.when(i == 0)
    def _():
        kb_ref[...] = k_ref[...].astype(BF16)
        vb_ref[:, 0:d] = v_ref[...].astype(BF16)
        vb_ref[:, d:2 * d] = jnp.ones((vb_ref.shape[0], d), BF16)

    for h in range(hg):
        qs_ref[h * tq:(h + 1) * tq, :] = (q_ref[:, h * d:(h + 1) * d] * d ** -0.5).astype(BF16)
    m_ref[...] = jnp.full(m_ref.shape, NEG_BIG, F32)
    acc_ref[...] = jnp.zeros(acc_ref.shape, F32)
    eq_ref[...] = jnp.zeros(eq_ref.shape, F32)
    thr = thr_ref[...]
    need = need_ref[...]
    nblk = (i * tq + tq + ts - 1) // ts

    nfull = (i * tq) // ts

    def run(tie):
        def block(jb, diagonal):
            off = pl.multiple_of(jb * ts, ts)
            sc = sc_ref[:, pl.ds(off, ts)]
            if diagonal:
                row = i * tq + lax.broadcasted_iota(I32, (tq, ts), 0)
                col = off + lax.broadcasted_iota(I32, (tq, ts), 1)
                causal = col <= row
            else:
                causal = "all"
            mask, eq_new = _select_mask(sc, thr, need, causal, eq_ref[...], tie)
            if tie:
                eq_ref[...] = eq_new
            kblk = kb_ref[pl.ds(off, ts), :]
            vblk = vb_ref[pl.ds(off, ts), :]
            for h in range(hg):
                s = jnp.where(mask, _dot(qs_ref[h * tq:(h + 1) * tq, :], kblk, NT), NEG_BIG)
                _flash_update(s, m_ref, acc_ref, h * tq, tq, vblk)

        def full_body(jb, carry):
            block(jb, False)
            return carry

        def diag_body(jb, carry):
            block(jb, True)
            return carry

        lax.fori_loop(0, nfull, full_body, 0)
        lax.fori_loop(nfull, nblk, diag_body, 0)

    has_tie = tie_ref[i] > 0

    @pl.when(has_tie)
    def _():
        run(True)

    @pl.when(jnp.logical_not(has_tie))
    def _():
        run(False)

    for h in range(hg):
        o_ref[:, h * d:(h + 1) * d] = acc_ref[h * tq:(h + 1) * tq, 0:d] / acc_ref[h * tq:(h + 1) * tq, d:2 * d]


def _dsa_attn(tie_blk, q_r, k_r, v_src, v_col0, scores, thr, need, *, tq, ts=512):
    t = q_r.shape[0]
    hg = B_HEADS // B_KV
    once = dict(pipeline_mode=pl.Buffered(1))
    grid_spec = pltpu.PrefetchScalarGridSpec(
        num_scalar_prefetch=1,
        grid=(B_KV, t // tq),
        in_specs=[pl.BlockSpec((tq, hg * B_HD), lambda g, i, tie: (i, g)),
                  pl.BlockSpec((t, B_HD), lambda g, i, tie: (0, g), **once),
                  pl.BlockSpec((t, B_HD), lambda g, i, tie: (0, v_col0 + g), **once),
                  pl.BlockSpec((tq, t), lambda g, i, tie: (i, 0)),
                  pl.BlockSpec((tq, 1), lambda g, i, tie: (i, 0)),
                  pl.BlockSpec((tq, 1), lambda g, i, tie: (i, 0))],
        out_specs=pl.BlockSpec((tq, hg * B_HD), lambda g, i, tie: (i, g)),
        scratch_shapes=[pltpu.VMEM((t, B_HD), BF16), pltpu.VMEM((t, 2 * B_HD), BF16),
                        pltpu.VMEM((hg * tq, B_HD), BF16), pltpu.VMEM((hg * tq, LANES), F32),
                        pltpu.VMEM((hg * tq, 2 * B_HD), F32), pltpu.VMEM((tq, 1), F32)])
    return pl.pallas_call(
        functools.partial(_dsa_attn_kernel, tq=tq, ts=ts),
        grid_spec=grid_spec,
        out_shape=jax.ShapeDtypeStruct((t, B_HEADS * B_HD), F32),
        compiler_params=_cp(("arbitrary", "arbitrary"), 56),
        name="dsa_attn",
    )(tie_blk, q_r, k_r, v_src, scores, thr, need)


def _lambda_value(lam_ref, lam_init):
    lp = lam_ref[...]
    a = jnp.sum(lp[0:1, :] * lp[1:2, :], axis=-1, keepdims=True)
    b = jnp.sum(lp[2:3, :] * lp[3:4, :], axis=-1, keepdims=True)
    return jnp.exp(a) - jnp.exp(b) + lam_init


def _diff_attn_kernel(q_ref, k_ref, v_ref, lam_ref, ng_ref, o_ref, kb_ref, vb_ref, qs_ref, m_ref, acc_ref,
                      *, tq, ts, rsub, lam_init):
    i = pl.program_id(1)
    d2 = 2 * C_HD
    rows = 2 * tq

    @pl.when(i == 0)
    def _():
        kb_ref[...] = k_ref[...].astype(BF16)
        vb_ref[:, 0:d2] = v_ref[...].astype(BF16)
        vb_ref[:, d2:2 * d2] = jnp.ones((vb_ref.shape[0], d2), BF16)

    q = q_ref[...] * C_HD ** -0.5
    first = lax.broadcasted_iota(I32, (tq, d2), 1) < C_HD
    qs_ref[0:tq, :] = jnp.where(first, q, 0.0).astype(BF16)
    qs_ref[tq:rows, :] = jnp.where(first, 0.0, q).astype(BF16)
    m_ref[...] = jnp.full(m_ref.shape, NEG_BIG, F32)
    acc_ref[...] = jnp.zeros(acc_ref.shape, F32)

    def step(jb, diag_col0):
        off = pl.multiple_of(jb * ts, ts)
        kblk = kb_ref[pl.ds(off, ts), :]
        vblk = vb_ref[pl.ds(off, ts), :]
        for r0 in range(0, rows, rsub):
            q0 = r0 % tq
            if diag_col0 is not None and q0 + rsub - 1 < diag_col0:
                continue
            s = _dot(qs_ref[r0:r0 + rsub, :], kblk, NT)
            if diag_col0 is not None and q0 < diag_col0 + ts - 1:
                qrow = q0 + lax.broadcasted_iota(I32, (rsub, ts), 0)
                s = jnp.where(diag_col0 + lax.broadcasted_iota(I32, (rsub, ts), 1) <= qrow, s, NEG_BIG)
            _flash_update(s, m_ref, acc_ref, r0, rsub, vblk)

    def body(jb, carry):
        step(jb, None)
        return carry

    nfull = (i * tq) // ts
    lax.fori_loop(0, nfull, body, 0)
    for dj in range(tq // ts):
        step(nfull + dj, dj * ts)

    lam = _lambda_value(lam_ref, lam_init)
    o = (acc_ref[0:tq, 0:d2] / acc_ref[0:tq, d2:2 * d2]
         - lam * (acc_ref[tq:rows, 0:d2] / acc_ref[tq:rows, d2:2 * d2]))
    on = o * lax.rsqrt(jnp.mean(o * o, axis=-1, keepdims=True) + EPS) * ng_ref[...]
    o_ref[...] = on * (1.0 - lam_init)


def _diff_attn(q_r, k_r, v_src, v_col0, c_lambda, c_norm_g, *, lam_init, tq=2048, ts=1024, rsub=512):
    t = q_r.shape[0]
    tq = min(tq, t)
    d2 = 2 * C_HD
    once = dict(pipeline_mode=pl.Buffered(1))
    return pl.pallas_call(
        functools.partial(_diff_attn_kernel, tq=tq, ts=ts, rsub=rsub, lam_init=lam_init),
        grid=(C_HEADS, t // tq),
        in_specs=[pl.BlockSpec((tq, d2), lambda h, i: (i, h)),
                  pl.BlockSpec((t, d2), lambda h, i: (0, h), **once),
                  pl.BlockSpec((t, d2), lambda h, i: (0, v_col0 + h), **once),
                  pl.BlockSpec((4, C_HD), lambda h, i: (0, 0)),
                  pl.BlockSpec((1, d2), lambda h, i: (0, 0))],
        out_specs=pl.BlockSpec((tq, d2), lambda h, i: (i, h)),
        out_shape=jax.ShapeDtypeStruct((t, C_HEADS * d2), F32),
        scratch_shapes=[pltpu.VMEM((t, d2), BF16), pltpu.VMEM((t, 2 * d2), BF16), pltpu.VMEM((2 * tq, d2), BF16),
                        pltpu.VMEM((2 * tq, LANES), F32), pltpu.VMEM((2 * tq, 2 * d2), F32)],
        compiler_params=_cp(("arbitrary", "arbitrary"), 48),
        name="diff_attn",
    )(q_r, k_r, v_src, c_lambda, c_norm_g.reshape(1, d2))


TS_PAD = 8


def _page_specs(n_pg, blk, layer, n_prefetch):
    def spec(k):
        if n_prefetch == 1:
            return pl.BlockSpec((1, 1) + blk, lambda bi, p, pt: (layer, pt[bi, p * n_pg + k], 0, 0))
        return pl.BlockSpec((1, 1) + blk, lambda bi, p, pt, tie: (layer, pt[bi, p * n_pg + k], 0, 0))

    return [spec(k) for k in range(n_pg)]


def _idx_scores_s_kernel(pt_ref, q_ref, w_ref, *refs, n_pg, t_new):
    pools = refs[:n_pg]
    new_ref, o_ref, onew_ref = refs[n_pg:]
    q = q_ref[0]
    w = w_ref[0] * IDX_SCALE

    def weigh(s):
        acc = jnp.zeros((TS_PAD, s.shape[1]), F32)
        for h in range(IDX_HEADS):
            acc = acc + w[h * TS_PAD:(h + 1) * TS_PAD, :] * jnp.maximum(s[h * TS_PAD:(h + 1) * TS_PAD, :], 0.0)
        return acc

    for k in range(n_pg):
        o_ref[0, :, k * PAGE:(k + 1) * PAGE] = weigh(_dot3(q, pools[k][0, 0]))

    @pl.when(pl.program_id(1) == pl.num_programs(1) - 1)
    def _():
        tq = lax.broadcasted_iota(I32, (TS_PAD, PAGE), 0)
        tk = lax.broadcasted_iota(I32, (TS_PAD, PAGE), 1)
        ok = jnp.logical_and(tk <= tq, tk < t_new)
        onew_ref[0] = jnp.where(ok, weigh(_dot3(q, new_ref[0], NT)), -jnp.inf)


def _idx_scores_s(page_table, q_s, w_s, pool_ki_t, ki_new, *, layer, t_new, n_pg=8):
    b, n_pages = page_table.shape
    grid_spec = pltpu.PrefetchScalarGridSpec(
        num_scalar_prefetch=1,
        grid=(b, n_pages // n_pg),
        in_specs=[pl.BlockSpec((1, IDX_HEADS * TS_PAD, IDX_HD), lambda bi, p, pt: (bi, 0, 0)),
                  pl.BlockSpec((1, IDX_HEADS * TS_PAD, 1), lambda bi, p, pt: (bi, 0, 0))]
        + _page_specs(n_pg, (IDX_HD, PAGE), layer, 1)
        + [pl.BlockSpec((1, PAGE, IDX_HD), lambda bi, p, pt: (bi, 0, 0))],
        out_specs=[pl.BlockSpec((1, TS_PAD, n_pg * PAGE), lambda bi, p, pt: (bi, 0, p)),
                   pl.BlockSpec((1, TS_PAD, PAGE), lambda bi, p, pt: (bi, 0, 0))])
    return pl.pallas_call(
        functools.partial(_idx_scores_s_kernel, n_pg=n_pg, t_new=t_new),
        grid_spec=grid_spec,
        out_shape=[jax.ShapeDtypeStruct((b, TS_PAD, n_pages * PAGE), F32),
                   jax.ShapeDtypeStruct((b, TS_PAD, PAGE), F32)],
        compiler_params=_cp(("arbitrary", "arbitrary")),
        name="idx_scores_sample",
    )(page_table, q_s, w_s, *([pool_ki_t] * n_pg), ki_new)


def _dsa_attn_s_kernel(pt_ref, tie_ref, q_ref, *refs, n_pg):
    pk = refs[:n_pg]
    pv = refs[n_pg:2 * n_pg]
    nk_ref, nv_ref, sc_ref, scn_ref, thr_ref, need_ref, o_ref, m_ref, l_ref, acc_ref, eq_ref = refs[2 * n_pg:]
    bi = pl.program_id(0)
    p = pl.program_id(1)
    hg = B_HEADS // B_KV

    @pl.when(p == 0)
    def _():
        m_ref[...] = jnp.full(m_ref.shape, NEG_BIG, F32)
        l_ref[...] = jnp.zeros(l_ref.shape, F32)
        acc_ref[...] = jnp.zeros(acc_ref.shape, F32)
        eq_ref[...] = jnp.zeros(eq_ref.shape, F32)

    thr = thr_ref[0]
    need = need_ref[0]
    scale = B_HD ** -0.5

    def attend(sc, keys_of, vals_of, tie):
        mask8, eq_new = _select_mask(sc, thr, need, None, eq_ref[...], tie)
        if tie:
            eq_ref[...] = eq_new
        mask = jnp.concatenate([mask8] * hg, axis=0)
        for g in range(B_KV):
            s = jnp.where(mask, _dot3(q_ref[0, g] * scale, keys_of(g), NT), NEG_BIG)
            m_old = m_ref[g]
            m_new = jnp.maximum(m_old, jnp.max(s, axis=-1, keepdims=True))
            pr = jnp.where(mask, jnp.exp(s - m_new), 0.0)
            alpha = jnp.exp(m_old - m_new)
            l_ref[g] = alpha * l_ref[g] + jnp.sum(pr, axis=-1, keepdims=True)
            acc_ref[g] = alpha * acc_ref[g] + _dot3(pr, vals_of(g))
            m_ref[g] = m_new

    def paged(refs_):
        return lambda g: jnp.concatenate([r[0, 0, pl.ds(g, PAGE, stride=B_KV), :] for r in refs_], axis=0)

    def run(tie):
        attend(sc_ref[0], paged(pk), paged(pv), tie)

        @pl.when(p == pl.num_programs(1) - 1)
        def _():
            attend(scn_ref[0], lambda g: nk_ref[0, :, g * B_HD:(g + 1) * B_HD],
                   lambda g: nv_ref[0, :, g * B_HD:(g + 1) * B_HD], tie)

    has_tie = tie_ref[bi] > 0

    @pl.when(has_tie)
    def _():
        run(True)

    @pl.when(jnp.logical_not(has_tie))
    def _():
        run(False)

    @pl.when(p == pl.num_programs(1) - 1)
    def _():
        for g in range(B_KV):
            o_ref[0, g] = acc_ref[g] / l_ref[g]


def _dsa_attn_s(page_table, tie_b, q_s, pool_k2, pool_v2, k_new, v_new, scores, scores_new, thr, need,
                *, layer, n_pg=4):
    b, n_pages = page_table.shape
    hg = B_HEADS // B_KV
    rows = hg * TS_PAD
    kvw = B_KV * B_HD
    grid_spec = pltpu.PrefetchScalarGridSpec(
        num_scalar_prefetch=2,
        grid=(b, n_pages // n_pg),
        in_specs=[pl.BlockSpec((1, B_KV, rows, B_HD), lambda bi, p, pt, tie: (bi, 0, 0, 0))]
        + _page_specs(n_pg, (PAGE * B_KV, B_HD), layer, 2) + _page_specs(n_pg, (PAGE * B_KV, B_HD), layer, 2)
        + [pl.BlockSpec((1, PAGE, kvw), lambda bi, p, pt, tie: (bi, 0, 0)),
           pl.BlockSpec((1, PAGE, kvw), lambda bi, p, pt, tie: (bi, 0, 0)),
           pl.BlockSpec((1, TS_PAD, n_pg * PAGE), lambda bi, p, pt, tie: (bi, 0, p)),
           pl.BlockSpec((1, TS_PAD, PAGE), lambda bi, p, pt, tie: (bi, 0, 0)),
           pl.BlockSpec((1, TS_PAD, 1), lambda bi, p, pt, tie: (bi, 0, 0)),
           pl.BlockSpec((1, TS_PAD, 1), lambda bi, p, pt, tie: (bi, 0, 0))],
        out_specs=pl.BlockSpec((1, B_KV, rows, B_HD), lambda bi, p, pt, tie: (bi, 0, 0, 0)),
        scratch_shapes=[pltpu.VMEM((B_KV, rows, 1), F32), pltpu.VMEM((B_KV, rows, 1), F32),
                        pltpu.VMEM((B_KV, rows, B_HD), F32), pltpu.VMEM((TS_PAD, 1), F32)])
    return pl.pallas_call(
        functools.partial(_dsa_attn_s_kernel, n_pg=n_pg),
        grid_spec=grid_spec,
        out_shape=jax.ShapeDtypeStruct((b, B_KV, rows, B_HD), F32),
        compiler_params=_cp(("arbitrary", "arbitrary")),
        name="dsa_attn_sample",
    )(page_table, tie_b, q_s, *([pool_k2] * n_pg), *([pool_v2] * n_pg), k_new, v_new, scores, scores_new, thr, need)


def _diff_attn_s_kernel(pt_ref, wt_ref, *refs, n_pg, t_new, lam_init):
    pk = refs[:n_pg]
    pv = refs[n_pg:2 * n_pg]
    nk_ref, nv_ref, ex_ref, hm_ref, lam_ref, ng_ref, o_ref, m_ref, l_ref, acc_ref = refs[2 * n_pg:]
    p = pl.program_id(1)
    nrow = wt_ref.shape[1]
    rph = 2 * t_new
    dv = 2 * C_HD

    @pl.when(p == 0)
    def _():
        m_ref[...] = jnp.full(m_ref.shape, NEG_BIG, F32)
        l_ref[...] = jnp.zeros(l_ref.shape, F32)
        acc_ref[...] = jnp.zeros(acc_ref.shape, F32)

    wt = wt_ref[0]

    def attend(s, pv_of):
        m_old = m_ref[...]
        m_new = jnp.maximum(m_old, jnp.max(s, axis=-1, keepdims=True))
        pr = jnp.exp(s - m_new)
        alpha = jnp.exp(m_old - m_new)
        l_ref[...] = alpha * l_ref[...] + jnp.sum(pr, axis=-1, keepdims=True)
        acc_ref[...] = alpha * acc_ref[...] + pv_of(pr.astype(BF16))
        m_ref[...] = m_new

    def pv_paged(prb):
        out = jnp.zeros((nrow, dv), F32)
        for k in range(n_pg):
            spread = _dot(prb[:, k * PAGE:(k + 1) * PAGE], ex_ref[...]) * hm_ref[...]
            out = out + _dot(spread.astype(BF16), pv[k][0, 0].astype(BF16))
        return out

    s_past = _dot(wt, jnp.concatenate([r[0, 0] for r in pk], axis=1).astype(BF16))
    attend(s_past, pv_paged)

    @pl.when(p == pl.num_programs(1) - 1)
    def _():
        tq = lax.broadcasted_iota(I32, (nrow, TS_PAD), 0) & (t_new - 1)
        tk = lax.broadcasted_iota(I32, (nrow, TS_PAD), 1)
        s_new = jnp.where(tk <= tq, _dot(wt, nk_ref[0].astype(BF16), NT), NEG_BIG)
        attend(s_new, lambda prb: jnp.concatenate(
            [_dot(prb[h * rph:(h + 1) * rph, :], nv_ref[0, :, h * dv:(h + 1) * dv].astype(BF16))
             for h in range(C_HEADS)], axis=0))
        lam = _lambda_value(lam_ref, lam_init)
        a = acc_ref[...] / l_ref[...]
        for h in range(C_HEADS):
            r0 = h * rph
            o = a[r0:r0 + t_new, :] - lam * a[r0 + t_new:r0 + rph, :]
            on = o * lax.rsqrt(jnp.mean(o * o, axis=-1, keepdims=True) + EPS) * ng_ref[...]
            o_ref[0, :, h * dv:(h + 1) * dv] = on * (1.0 - lam_init)


def _diff_attn_s(page_table, wt, pool_kt, pool_v2, k_new, v_new, c_lambda, c_norm_g, *, layer, t_new, lam_init,
                 n_pg=4):
    b, n_pages = page_table.shape
    nrow, width = wt.shape[1:]
    dv = 2 * C_HD
    vrows = PAGE * C_HEADS
    vrow = jnp.arange(vrows, dtype=I32)
    expand = (vrow[None, :] // C_HEADS == jnp.arange(PAGE, dtype=I32)[:, None]).astype(BF16)
    own_head = (vrow[None, :] % C_HEADS == jnp.arange(nrow, dtype=I32)[:, None] // (2 * t_new)).astype(F32)
    grid_spec = pltpu.PrefetchScalarGridSpec(
        num_scalar_prefetch=1,
        grid=(b, n_pages // n_pg),
        in_specs=[pl.BlockSpec((1, nrow, width), lambda bi, p, pt: (bi, 0, 0))]
        + _page_specs(n_pg, (width, PAGE), layer, 1) + _page_specs(n_pg, (vrows, dv), layer, 1)
        + [pl.BlockSpec((1, TS_PAD, width), lambda bi, p, pt: (bi, 0, 0)),
           pl.BlockSpec((1, TS_PAD, width), lambda bi, p, pt: (bi, 0, 0)),
           pl.BlockSpec((PAGE, vrows), lambda bi, p, pt: (0, 0)),
           pl.BlockSpec((nrow, vrows), lambda bi, p, pt: (0, 0)),
           pl.BlockSpec((4, C_HD), lambda bi, p, pt: (0, 0)),
           pl.BlockSpec((1, dv), lambda bi, p, pt: (0, 0))],
        out_specs=pl.BlockSpec((1, t_new, width), lambda bi, p, pt: (bi, 0, 0)),
        scratch_shapes=[pltpu.VMEM((nrow, 1), F32), pltpu.VMEM((nrow, 1), F32), pltpu.VMEM((nrow, dv), F32)])
    return pl.pallas_call(
        functools.partial(_diff_attn_s_kernel, n_pg=n_pg, t_new=t_new, lam_init=lam_init),
        grid_spec=grid_spec,
        out_shape=jax.ShapeDtypeStruct((b, t_new, width), F32),
        compiler_params=_cp(("arbitrary", "arbitrary"), 56),
        name="diff_attn_sample",
    )(page_table, wt, *([pool_kt] * n_pg), *([pool_v2] * n_pg), k_new, v_new, expand, own_head, c_lambda,
      c_norm_g.reshape(1, dv))


def _split_ab_weights(w_in):
    n_a = 4 * A_HEADS * A_DK
    n_gate = 2 * A_HEADS
    n_b = (B_HEADS + 2 * B_KV) * B_HD
    n_idx = IDX_HEADS * IDX_HD + IDX_HD + IDX_HEADS
    big = jnp.concatenate([w_in[:, :n_a], w_in[:, n_a + n_gate:n_a + n_gate + n_b]], axis=1)
    pad = 5 * LANES - n_idx - n_gate
    small = jnp.concatenate([w_in[:, n_a + n_gate + n_b:], w_in[:, n_a:n_a + n_gate],
                             jnp.zeros((w_in.shape[0], pad), w_in.dtype)], axis=1)
    return big, small


PROMPT_TM = 2048
GATE_OFF = IDX_HD + IDX_HEADS
QB_COL0 = 4 * A_HEADS * A_DK // LANES
KB_COL0 = QB_COL0 + B_HEADS
VB_COL0 = KB_COL0 + B_KV


def _mlp(x, mods, norm_g, w1, w2, *, tm, hi, tn=512, tk=2048):
    sh2, sc2, g2 = mods
    hid = _norm_matmul(x, norm_g[2:3], sc2, sh2, w1, tm=tm, tn=tn, hi=hi, relu2=True,
                       out_dtype=F32 if hi else BF16, vmem_mb=56)
    if hi:
        return _matmul_norm_res(hid, w2, norm_g[3:4], g2, x, tm=tm, tk=512, hi=True)
    return _matmul_norm_res(hid, w2, norm_g[3:4], g2, x, tm=512, tk=tk, vmem_mb=56)


def _prompt_trunk(x, mods, P):
    t = x.shape[0]
    pos = jnp.arange(t, dtype=I32)
    tab128, tab64 = _rope_tables(pos)
    sh1, sc1, g1, sh2, sc2, g2 = mods[0]
    ng = P["norm_g"][0]
    big = _norm_matmul(x, ng[0:1], sc1, sh1, P["w_big"], tm=PROMPT_TM, tn=256, vmem_mb=56)
    small = _norm_matmul(x, ng[0:1], sc1, sh1, P["w_small"], tm=PROMPT_TM // 2, tn=5 * LANES, hi=True)
    small_r = _rope(small, tab64, col0=0, ngroups=5, n_kind0=4, r1=32, r2=96, tm=1024)
    qb_r = _rope(big, tab128, col0=QB_COL0, ngroups=B_HEADS, n_kind0=B_HEADS, r1=64, r2=None, tm=1024, gw=4)
    kb_r = _rope(big, tab128, col0=KB_COL0, ngroups=B_KV, n_kind0=B_KV, r1=64, r2=None, tm=1024)
    qcat, kcat = _idx_prep(small_r)
    scores = _idx_scores(qcat, kcat, small_r)
    thr, need, tie = _topk_thr(scores, k=min(TOPK, t // 4), cw=1024, causal=True, tr=128)
    tq = 256
    tie_blk = jnp.max(tie.reshape(t // tq, tq), axis=1)
    ob = _dsa_attn(tie_blk, qb_r, kb_r, big, VB_COL0, scores, thr, need, tq=tq)
    oa, s_new = _gdn(big[None], small_r[None], jnp.zeros((1, A_CONV - 1, 3 * A_HEADS * A_DK), F32), P["ab_conv_w"],
                     P["ab_A_log"], P["ab_dt_bias"], P["ab_norm_g"], jnp.zeros((1, A_HEADS, A_DK, A_DK), F32),
                     c=128, t_valid=128, gate_group=4, gate_off=GATE_OFF)
    mix = jnp.concatenate([oa[0], ob], axis=1)
    x = _matmul_norm_res(mix, P["ab_w_out_bf16"], ng[1:2], g1, x, tm=512, tk=2048, vmem_mb=56)
    x = _mlp(x, (sh2, sc2, g2), ng, (P["mlp_w1"], 0), (P["mlp_w2_bf16"], 0), tm=PROMPT_TM, hi=False)
    outs0 = (kb_r, big[:, VB_COL0 * LANES:], small_r[:, 4 * LANES:4 * LANES + IDX_HD], s_new,
             big[t - (A_CONV - 1):, :3 * A_HEADS * A_DK])
    sh1, sc1, g1, sh2, sc2, g2 = mods[1]
    ng = P["norm_g"][1]
    lam_init = 0.8 - 0.6 * math.exp(-0.3 * 1)
    proj = _norm_matmul(x, ng[0:1], sc1, sh1, P["c_w_in"], tm=PROMPT_TM, tn=256, vmem_mb=56)
    q_r = _rope(proj, tab64, col0=0, ngroups=C_HEADS, n_kind0=C_HEADS, r1=32, r2=96, tm=1024, gw=4)
    k_r = _rope(proj, tab64, col0=C_HEADS, ngroups=C_HEADS, n_kind0=C_HEADS, r1=32, r2=96, tm=1024, gw=4)
    o = _diff_attn(q_r, k_r, proj, 2 * C_HEADS, P["c_lambda"], P["c_norm_g"], lam_init=lam_init)
    x = _matmul_norm_res(o, P["c_w_out_bf16"], ng[1:2], g1, x, tm=512, tk=2048, vmem_mb=56)
    x = _mlp(x, (sh2, sc2, g2), ng, (P["mlp_w1"], 1), (P["mlp_w2_bf16"], 1), tm=PROMPT_TM, hi=False)
    outs1 = (k_r, proj[:, 4 * C_HEADS * C_HD:])
    return x, outs0, outs1


def _sample_trunk(x, mods, P, cache, past_len):
    b, ts_, d = x.shape
    m = b * ts_
    x = x.reshape(m, d)
    pos = past_len + jnp.arange(ts_, dtype=I32)
    tab128, tab64 = _rope_tables(pos)
    tab128 = jnp.tile(tab128, (b, 1))
    tab64 = jnp.tile(tab64, (b, 1))
    page_table = cache["page_table"]
    n_pages = page_table.shape[1]
    hw = A_HEADS * A_DK

    def rows(a):
        return jnp.repeat(a, ts_, axis=0)

    def pad_t(a, n):
        return jnp.pad(a, ((0, 0), (0, n - ts_), (0, 0)))

    sh1, sc1, g1, sh2, sc2, g2 = (rows(a) for a in mods[0])
    ng = P["norm_g"][0]
    big = _norm_matmul(x, ng[0:1], sc1, sh1, P["w_big"], tm=m, tn=512, hi=True)
    small = _norm_matmul(x, ng[0:1], sc1, sh1, P["w_small"], tm=m, tn=5 * LANES, hi=True)
    small_r = _rope(small, tab64, col0=0, ngroups=5, n_kind0=4, r1=32, r2=96, tm=m)
    qb_r = _rope(big, tab128, col0=QB_COL0, ngroups=B_HEADS, n_kind0=B_HEADS, r1=64, r2=None, tm=m)
    kb_r = _rope(big, tab128, col0=KB_COL0, ngroups=B_KV, n_kind0=B_KV, r1=64, r2=None, tm=m)
    vb = big[:, VB_COL0 * LANES:]
    sm3 = small_r.reshape(b, ts_, 5 * LANES)
    qi = sm3[:, :, :IDX_HEADS * IDX_HD].reshape(b, ts_, IDX_HEADS, IDX_HD)
    qi = jnp.pad(jnp.swapaxes(qi, 1, 2), ((0, 0), (0, 0), (0, TS_PAD - ts_), (0, 0)))
    qi = qi.reshape(b, IDX_HEADS * TS_PAD, IDX_HD)
    wi = sm3[:, :, 4 * LANES + W_OFF:4 * LANES + W_OFF + IDX_HEADS]
    wi = jnp.pad(jnp.swapaxes(wi, 1, 2), ((0, 0), (0, 0), (0, TS_PAD - ts_))).reshape(b, IDX_HEADS * TS_PAD, 1)
    ki_r = sm3[:, :, 4 * LANES:4 * LANES + IDX_HD]
    pool_ki_t = jnp.swapaxes(cache["ab_kidx"], 2, 3)
    sc_past, sc_new = _idx_scores_s(page_table, qi, wi, pool_ki_t, pad_t(ki_r, PAGE), layer=0, t_new=ts_,
                                    n_pg=min(16, n_pages))
    scores = jnp.concatenate([sc_past, sc_new], axis=2)
    ncols = scores.shape[2]
    k_sel = min(TOPK, (past_len + ts_) // 4)
    chunk = min(d for d in range(1, ncols // LANES + 1) if (ncols // LANES) % d == 0 and d * LANES >= k_sel) * LANES
    thr, need, tie = _topk_thr(scores.reshape(b * TS_PAD, ncols), k=k_sel, cw=chunk, causal=False, tr=64)
    tie_b = jnp.max(tie.reshape(b, TS_PAD)[:, :ts_], axis=1)
    hg = B_HEADS // B_KV
    q4 = qb_r.reshape(b, ts_, B_KV, hg, B_HD)
    q4 = jnp.pad(jnp.transpose(q4, (0, 2, 3, 1, 4)), ((0, 0), (0, 0), (0, 0), (0, TS_PAD - ts_), (0, 0)))
    q4 = q4.reshape(b, B_KV, hg * TS_PAD, B_HD)
    kvw = B_KV * B_HD
    pool_k2 = cache["ab_k"].reshape(cache["ab_k"].shape[:2] + (PAGE * B_KV, B_HD))
    pool_v2 = cache["ab_v"].reshape(cache["ab_v"].shape[:2] + (PAGE * B_KV, B_HD))
    ob = _dsa_attn_s(page_table, tie_b, q4, pool_k2, pool_v2, pad_t(kb_r.reshape(b, ts_, kvw), PAGE),
                     pad_t(vb.reshape(b, ts_, kvw), PAGE), sc_past, sc_new, thr.reshape(b, TS_PAD, 1),
                     need.reshape(b, TS_PAD, 1), layer=0, n_pg=min(16, n_pages))
    ob = ob.reshape(b, B_KV, hg, TS_PAD, B_HD)[:, :, :, :ts_]
    ob = jnp.transpose(ob, (0, 3, 1, 2, 4)).reshape(m, B_HEADS * B_HD)
    oa, s_new = _gdn(pad_t(big.reshape(b, ts_, -1), TS_PAD), pad_t(sm3, TS_PAD), cache["ab_conv"][0], P["ab_conv_w"],
                     P["ab_A_log"], P["ab_dt_bias"], P["ab_norm_g"], cache["ab_delta"][0],
                     c=TS_PAD, t_valid=ts_, gate_group=4, gate_off=GATE_OFF)
    mix = jnp.concatenate([oa[:, :ts_].reshape(m, hw), ob], axis=1)
    x = _matmul_norm_res(mix, P["ab_w_out"], ng[1:2], g1, x, tm=m, tk=512, hi=True)
    x = _mlp(x, (sh2, sc2, g2), ng, (P["mlp_w1"], 0), (P["mlp_w2"], 0), tm=m, hi=True)
    conv_in = jnp.concatenate([cache["ab_conv"][0], big.reshape(b, ts_, -1)[:, :, :3 * hw]], axis=1)
    outs0 = (kb_r, vb, ki_r, s_new, conv_in[:, ts_:])
    sh1, sc1, g1, sh2, sc2, g2 = (rows(a) for a in mods[1])
    ng = P["norm_g"][1]
    lam_init = 0.8 - 0.6 * math.exp(-0.3 * 1)
    proj = _norm_matmul(x, ng[0:1], sc1, sh1, P["c_w_in"], tm=m, tn=512, hi=True)
    qk_r = _rope(proj, tab64, col0=0, ngroups=2 * C_HEADS, n_kind0=2 * C_HEADS, r1=32, r2=96, tm=m)
    cw = 2 * C_HEADS * C_HD
    q3 = qk_r[:, :cw].reshape(b, ts_, cw) * C_HD ** -0.5
    k3 = qk_r[:, cw:].reshape(b, ts_, cw)
    v3 = proj[:, 2 * cw:].reshape(b, ts_, cw)
    lane_hp = jnp.arange(cw, dtype=I32) // C_HD
    want = (2 * jnp.arange(C_HEADS, dtype=I32)[:, None] + jnp.arange(2, dtype=I32)[None, :])
    sel = (lane_hp[None, None, :] == want[:, :, None]).astype(F32)
    wt = (q3[:, None, None, :, :] * sel[None, :, :, None, :]).reshape(b, 2 * C_HEADS * ts_, cw).astype(BF16)
    ck = cache["c_k"]
    pool_ckt = jnp.transpose(ck, (0, 1, 3, 4, 5, 2)).reshape(ck.shape[:2] + (cw, PAGE))
    pool_cv2 = cache["c_v"].reshape(cache["c_v"].shape[:2] + (PAGE * C_HEADS, 2 * C_HD))
    o = _diff_attn_s(page_table, wt, pool_ckt, pool_cv2, pad_t(k3, TS_PAD), pad_t(v3, TS_PAD), P["c_lambda"],
                     P["c_norm_g"], layer=0, t_new=ts_, lam_init=lam_init, n_pg=min(8, n_pages))
    x = _matmul_norm_res(o.reshape(m, cw), P["c_w_out"], ng[1:2], g1, x, tm=m, tk=512, hi=True)
    x = _mlp(x, (sh2, sc2, g2), ng, (P["mlp_w1"], 1), (P["mlp_w2"], 1), tm=m, hi=True)
    outs1 = (qk_r[:, cw:], proj[:, 2 * cw:])
    return x.reshape(b, ts_, d), outs0, outs1


def kernel(x_prompt, x_sample, cache_ab_k, cache_ab_v, cache_ab_kidx, state_ab_delta, state_ab_conv, cache_c_k,
           cache_c_v, page_table, c_prompt, c_sample, ada_w, ada_b, norm_g, mlp_w1, mlp_w2, ab_w_in, ab_w_out,
           ab_conv_w, ab_A_log, ab_dt_bias, ab_norm_g, c_w_in, c_w_out, c_lambda, c_norm_g):
    bp, t, d = x_prompt.shape
    bs, ts_, _ = x_sample.shape
    assert bp == 1
    past_len = page_table.shape[1] * PAGE
    w_big, w_small = _split_ab_weights(ab_w_in[0])
    P = {"norm_g": norm_g, "mlp_w1": mlp_w1, "mlp_w2": mlp_w2, "w_big": w_big, "w_small": w_small,
         "ab_w_out": ab_w_out[0], "ab_conv_w": ab_conv_w[0], "ab_A_log": ab_A_log[0], "ab_dt_bias": ab_dt_bias[0],
         "ab_norm_g": ab_norm_g[0], "c_w_in": c_w_in[0], "c_w_out": c_w_out[0], "c_lambda": c_lambda[0],
         "c_norm_g": c_norm_g[0],
         "mlp_w2_bf16": mlp_w2.astype(BF16), "ab_w_out_bf16": ab_w_out[0].astype(BF16),
         "c_w_out_bf16": c_w_out[0].astype(BF16)}
    n_seq = bp + bs
    mc = -(-n_seq // 8) * 8
    c_all = jnp.pad(jnp.concatenate([c_prompt, c_sample], axis=0), ((0, mc - n_seq), (0, 0)))
    mod = _ada(c_all, ada_w, ada_b)
    mods_p = [tuple(mod[i, 0:bp, n * d:(n + 1) * d] for n in range(6)) for i in range(2)]
    mods_s = [tuple(mod[i, bp:n_seq, n * d:(n + 1) * d] for n in range(6)) for i in range(2)]

    y_p, p0, p1 = _prompt_trunk(x_prompt[0], mods_p, P)
    cache = {"ab_k": cache_ab_k, "ab_v": cache_ab_v, "ab_kidx": cache_ab_kidx, "ab_delta": state_ab_delta,
             "ab_conv": state_ab_conv, "c_k": cache_c_k, "c_v": cache_c_v, "page_table": page_table}
    y_s, s0, s1 = _sample_trunk(x_sample, mods_s, P, cache, past_len)

    return (y_p[None], y_s,
            p0[0].reshape(1, 1, t, B_KV, B_HD), p0[1].reshape(1, 1, t, B_KV, B_HD), p0[2].reshape(1, 1, t, IDX_HD),
            p0[3][None], p0[4].reshape(1, 1, A_CONV - 1, -1),
            p1[0].reshape(1, 1, t, C_HEADS, 2, C_HD), p1[1].reshape(1, 1, t, C_HEADS, 2 * C_HD),
            s0[0].reshape(1, bs, ts_, B_KV, B_HD), s0[1].reshape(1, bs, ts_, B_KV, B_HD),
            s0[2].reshape(1, bs, ts_, IDX_HD), s0[3][None], s0[4][None],
            s1[0].reshape(1, bs, ts_, C_HEADS, 2, C_HD), s1[1].reshape(1, bs, ts_, C_HEADS, 2 * C_HD))
```

```python
import functools
import math

import jax
import jax.numpy as jnp
from jax import lax
from jax.experimental import pallas as pl
from jax.experimental.pallas import tpu as pltpu

F32 = jnp.float32
BF16 = jnp.bfloat16
I32 = jnp.int32

EPS = 1e-6
NEG_BIG = -1e30
ROPE_THETA = 10000.0
PAGE = 128
A_HEADS = 8
A_DK = 128
A_CONV = 4
B_HEADS = 8
B_KV = 2
B_HD = 128
IDX_HEADS = 8
IDX_HD = 64
TOPK = 256
C_HEADS = 16
C_HD = 64
LANES = 128

NN = (((1,), (0,)), ((), ()))
NT = (((1,), (1,)), ((), ()))
TN = (((0,), (0,)), ((), ()))
BNN = (((2,), (1,)), ((0,), (0,)))
BNT = (((2,), (2,)), ((0,), (0,)))
INT_MIN = -2147483648
KEY_NEG_INF = -2139095041


def _cp(dims, vmem_mb=None):
    kw = dict(dimension_semantics=dims)
    if vmem_mb is not None:
        kw["vmem_limit_bytes"] = vmem_mb << 20
    return pltpu.CompilerParams(**kw)


def _dot(a, b, dims=NN):
    return lax.dot_general(a, b, dims, preferred_element_type=F32)


def _split2(a):
    hi = a.astype(BF16)
    return hi, (a - hi.astype(F32)).astype(BF16)


def _split3(a):
    hi = a.astype(BF16)
    r = a - hi.astype(F32)
    mid = r.astype(BF16)
    return hi, mid, (r - mid.astype(F32)).astype(BF16)


def _dot1(a, b, dims=NN):
    return _dot(a.astype(BF16), b.astype(BF16), dims)


def _dot3(a, b, dims=NN):
    ah, al = _split2(a)
    bh, bl = _split2(b)
    return _dot(ah, bh, dims) + (_dot(ah, bl, dims) + _dot(al, bh, dims))


def _dot_exact_left(ones_bf16, b, dims=NN):
    b0, b1, b2 = _split3(b)
    return _dot(ones_bf16, b0, dims) + (_dot(ones_bf16, b1, dims) + _dot(ones_bf16, b2, dims))


def _sigmoid(x):
    return 1.0 / (1.0 + jnp.exp(-x))


def _silu(x):
    return x * _sigmoid(x)


def _ada_kernel(c_ref, w_ref, b_ref, o_ref):
    o_ref[0] = _dot3(_silu(c_ref[...]), w_ref[0]) + b_ref[0]


def _ada(c_all, ada_w, ada_b, tn=512):
    nl, d, n = ada_w.shape
    mc = c_all.shape[0]
    return pl.pallas_call(
        _ada_kernel,
        grid=(nl, n // tn),
        in_specs=[pl.BlockSpec((mc, d), lambda l, j: (0, 0)),
                  pl.BlockSpec((1, d, tn), lambda l, j: (l, 0, j)),
                  pl.BlockSpec((1, 1, tn), lambda l, j: (l, 0, j))],
        out_specs=pl.BlockSpec((1, mc, tn), lambda l, j: (l, 0, j)),
        out_shape=jax.ShapeDtypeStruct((nl, mc, n), F32),
        compiler_params=_cp(("arbitrary", "arbitrary"), 40),
        name="ada_mod",
    )(c_all, ada_w, ada_b.reshape(nl, 1, n))


def _norm_mm_kernel(x_ref, g_ref, sc_ref, sh_ref, w_ref, o_ref, *scratch, hi, relu2):
    hh_ref = scratch[0]

    @pl.when(pl.program_id(1) == 0)
    def _():
        x = x_ref[...]
        y = x * lax.rsqrt(jnp.mean(x * x, axis=-1, keepdims=True) + EPS)
        h = (y * g_ref[...]) * (1.0 + sc_ref[...]) + sh_ref[...]
        hh = h.astype(BF16)
        hh_ref[...] = hh
        if hi:
            scratch[1][...] = (h - hh.astype(F32)).astype(BF16)

    w = w_ref[0]
    wh = w.astype(BF16)
    acc = _dot(hh_ref[...], wh)
    if hi:
        wl = (w - wh.astype(F32)).astype(BF16)
        acc = acc + (_dot(hh_ref[...], wl) + _dot(scratch[1][...], wh))
    if relu2:
        acc = jnp.square(jnp.maximum(acc, 0.0))
    o_ref[...] = acc.astype(o_ref.dtype)


def _layer_weight(w):
    return w if isinstance(w, tuple) else (w[None], 0)


def _norm_matmul(x, g, sc, sh, w, *, tm, tn, hi=False, relu2=False, out_dtype=F32, vmem_mb=48):
    m, d = x.shape
    tm = min(tm, m)
    w, layer = _layer_weight(w)
    n = w.shape[2]
    per_row = sc.shape[0] != 1
    mod_spec = pl.BlockSpec((tm, d), lambda i, j: (i, 0)) if per_row else pl.BlockSpec((1, d), lambda i, j: (0, 0))
    scratch = [pltpu.VMEM((tm, d), BF16)] + ([pltpu.VMEM((tm, d), BF16)] if hi else [])
    return pl.pallas_call(
        functools.partial(_norm_mm_kernel, hi=hi, relu2=relu2),
        grid=(m // tm, n // tn),
        in_specs=[pl.BlockSpec((tm, d), lambda i, j: (i, 0), pipeline_mode=pl.Buffered(1)),
                  pl.BlockSpec((1, d), lambda i, j: (0, 0)),
                  mod_spec, mod_spec,
                  pl.BlockSpec((1, d, tn), lambda i, j: (layer, 0, j))],
        out_specs=pl.BlockSpec((tm, tn), lambda i, j: (i, j)),
        out_shape=jax.ShapeDtypeStruct((m, n), out_dtype),
        scratch_shapes=scratch,
        compiler_params=_cp(("arbitrary", "arbitrary"), vmem_mb),
        name="norm_matmul",
    )(x, g, sc, sh, w)


def _mm_norm_res_kernel(a_ref, w_ref, ng_ref, gate_ref, res_ref, o_ref, *, hi):
    k = pl.program_id(1)

    @pl.when(k == 0)
    def _():
        o_ref[...] = jnp.zeros(o_ref.shape, F32)

    if hi:
        o_ref[...] += _dot3(a_ref[...].astype(F32), w_ref[0])
    else:
        o_ref[...] += _dot1(a_ref[...], w_ref[0])

    @pl.when(k == pl.num_programs(1) - 1)
    def _():
        m = o_ref[...]
        y = m * lax.rsqrt(jnp.mean(m * m, axis=-1, keepdims=True) + EPS)
        o_ref[...] = res_ref[...] + gate_ref[...] * (y * ng_ref[...])


def _matmul_norm_res(a, w, ng, gate, res, *, tm, tk, hi=False, vmem_mb=48):
    m, kdim = a.shape
    w, layer = _layer_weight(w)
    n = w.shape[2]
    per_row = gate.shape[0] != 1
    gate_spec = pl.BlockSpec((tm, n), lambda i, k: (i, 0)) if per_row else pl.BlockSpec((1, n), lambda i, k: (0, 0))
    return pl.pallas_call(
        functools.partial(_mm_norm_res_kernel, hi=hi),
        grid=(m // tm, kdim // tk),
        in_specs=[pl.BlockSpec((tm, tk), lambda i, k: (i, k)),
                  pl.BlockSpec((1, tk, n), lambda i, k: (layer, k, 0)),
                  pl.BlockSpec((1, n), lambda i, k: (0, 0)),
                  gate_spec,
                  pl.BlockSpec((tm, n), lambda i, k: (i, 0), pipeline_mode=pl.Buffered(1))],
        out_specs=pl.BlockSpec((tm, n), lambda i, k: (i, 0)),
        out_shape=jax.ShapeDtypeStruct((m, n), F32),
        compiler_params=_cp(("arbitrary", "arbitrary"), vmem_mb),
        name="matmul_norm_res",
    )(a, w, ng, gate, res)


def _rope_kernel(x_ref, tab_ref, o_ref, *, r1, r2, gw):
    for k in range(gw):
        x = x_ref[:, k * LANES:(k + 1) * LANES]
        out = x * tab_ref[:, 0:LANES] + pltpu.roll(x, r1, 1) * tab_ref[:, LANES:2 * LANES]
        if r2 is not None:
            out = out + pltpu.roll(x, r2, 1) * tab_ref[:, 2 * LANES:3 * LANES]
        o_ref[:, k * LANES:(k + 1) * LANES] = out


def _rope(x, tab, *, col0, ngroups, n_kind0, r1, r2, tm, gw=1):
    t = x.shape[0]
    assert col0 % gw == 0 and ngroups % gw == 0 and n_kind0 % gw == 0
    return pl.pallas_call(
        functools.partial(_rope_kernel, r1=r1, r2=r2, gw=gw),
        grid=(t // tm, ngroups // gw),
        in_specs=[pl.BlockSpec((tm, gw * LANES), lambda i, j: (i, col0 // gw + j)),
                  pl.BlockSpec((tm, 3 * LANES), lambda i, j: (i, jnp.where(j * gw >= n_kind0, 1, 0)))],
        out_specs=pl.BlockSpec((tm, gw * LANES), lambda i, j: (i, j)),
        out_shape=jax.ShapeDtypeStruct((t, ngroups * LANES), F32),
        compiler_params=_cp(("arbitrary", "arbitrary")),
        name="rope",
    )(x, tab)


def _rope_tables(pos):
    p = pos.astype(F32)[:, None]

    def cs(half):
        inv = ROPE_THETA ** (-jnp.arange(half, dtype=F32) / half)
        ang = p * inv[None, :]
        return jnp.cos(ang), jnp.sin(ang)

    c, s = cs(64)
    tab128 = jnp.concatenate([c, c, -s, s, jnp.zeros_like(c), jnp.zeros_like(c)], axis=1)
    c, s = cs(32)
    z = jnp.zeros_like(c)
    one64 = jnp.ones((pos.shape[0], 64), F32)
    z64 = jnp.zeros((pos.shape[0], 64), F32)
    kind0 = jnp.concatenate([c, c, c, c, z, s, z, s, -s, z, -s, z], axis=1)
    kind1 = jnp.concatenate([c, c, one64, z, s, z64, -s, z, z64], axis=1)
    return tab128, jnp.concatenate([kind0, kind1], axis=1)


def _gdn_kernel(qkv_ref, z_ref, gate_ref, buf_ref, cw_ref, alog_ref, dtb_ref, ng_ref, s0_ref,
                o_ref, sout_ref, xbuf_ref, s_ref, *, c, t_valid, gate_off):
    ci = pl.program_id(1)
    nh, dk = A_HEADS, A_DK
    hw = nh * dk

    @pl.when(ci == 0)
    def _():
        s_ref[...] = s0_ref[0]
        xbuf_ref[0:8, :] = jnp.zeros((8, 3 * hw), F32)
        xbuf_ref[8 - (A_CONV - 1):8, :] = buf_ref[0]

    xbuf_ref[8:8 + c, :] = qkv_ref[0]
    y = xbuf_ref[5:5 + c, :] * cw_ref[0:1, :]
    for j in range(1, A_CONV):
        y = y + xbuf_ref[5 + j:5 + j + c, :] * cw_ref[j:j + 1, :]
    tail = xbuf_ref[8 + c - 3:8 + c, :]
    xbuf_ref[5:8, :] = tail
    y = _silu(y)

    gt = gate_ref[0]
    ba = gt[:, gate_off:gate_off + nh]
    aa = gt[:, gate_off + nh:gate_off + 2 * nh]
    beta = _sigmoid(ba)
    xs = aa + dtb_ref[...]
    softplus = jnp.maximum(xs, 0.0) + jnp.log1p(jnp.exp(-jnp.abs(xs)))
    g = -jnp.exp(alog_ref[...]) * softplus
    row = lax.broadcasted_iota(I32, (c, c), 0)
    col = lax.broadcasted_iota(I32, (c, c), 1)
    if t_valid < c:
        valid = lax.broadcasted_iota(I32, (c, nh), 0) < t_valid
        beta = jnp.where(valid, beta, 0.0)
        g = jnp.where(valid, g, 0.0)
    incl = (row >= col)[None]
    strict = (row > col)[None]
    tri = jnp.where(row >= col, 1.0, 0.0).astype(BF16)
    gc = _dot_exact_left(tri, g)
    eye_h = jnp.where(lax.broadcasted_iota(I32, (nh, nh), 0) == lax.broadcasted_iota(I32, (nh, nh), 1),
                      1.0, 0.0).astype(BF16)
    gct = _dot_exact_left(eye_h, gc, NT)
    eye_c = jnp.where(row == col, 1.0, 0.0)[None]

    def heads(a, off):
        return jnp.stack([a[:, off + h * dk:off + (h + 1) * dk] for h in range(nh)], axis=0)

    def cols(a):
        return jnp.stack([a[:, h:h + 1] for h in range(nh)], axis=0)

    q3 = heads(y, 0)
    k3 = heads(y, hw)
    v3 = heads(y, 2 * hw)
    q3 = q3 * lax.rsqrt(jnp.sum(q3 * q3, axis=-1, keepdims=True) + 1e-6)
    k3 = k3 * lax.rsqrt(jnp.sum(k3 * k3, axis=-1, keepdims=True) + 1e-6)
    b3 = cols(beta)
    gch = cols(gc)
    gct3 = jnp.stack([gct[h:h + 1, :] for h in range(nh)], axis=0)
    decay = jnp.where(incl, jnp.exp(jnp.minimum(gch - gct3, 0.0)), 0.0)
    kb = k3 * b3
    vb = v3 * b3
    a_mat = jnp.where(strict, _dot3(kb, k3, BNT) * decay, 0.0)
    def same_block(s):
        sh = int(round(math.log2(s)))
        return (jnp.right_shift(row, sh) == jnp.right_shift(col, sh))[None]

    s_blk = min(8, c)
    pw = jnp.where(same_block(s_blk), -a_mat, 0.0)
    t_inv = eye_c + pw
    for _ in range(int(round(math.log2(s_blk))) - 1):
        pw = _dot3(pw, pw, BNN)
        t_inv = t_inv + _dot3(t_inv, pw, BNN)
    while s_blk < c:
        off = jnp.where(jnp.logical_and(same_block(2 * s_blk), jnp.logical_not(same_block(s_blk))), a_mat, 0.0)
        t_inv = t_inv - _dot3(t_inv, _dot3(off, t_inv, BNN), BNN)
        s_blk *= 2
    egc = jnp.exp(gch)
    u = _dot3(t_inv, vb, BNN)
    w = _dot3(t_inv, kb * egc, BNN)
    qs = q3 * dk ** -0.5
    qk = _dot1(qs, k3, BNT) * decay
    g_last = gch[:, c - 1:c, :]
    k_dec = k3 * jnp.exp(jnp.minimum(g_last - gch, 0.0))
    s_old = s_ref[...]
    v_new = u - _dot3(w, s_old, BNN)
    o = _dot1(qs * egc, s_old, BNN) + _dot1(qk, v_new, BNN)
    eg_last = jnp.exp(g_last)
    on = o * lax.rsqrt(jnp.mean(o * o, axis=-1, keepdims=True) + EPS) * ng_ref[...]
    for h in range(nh):
        s_ref[h] = s_old[h] * eg_last[h] + _dot3(k_dec[h], v_new[h], TN)
        o_ref[0, :, h * dk:(h + 1) * dk] = on[h] * _silu(z_ref[0, :, h * dk:(h + 1) * dk])

    @pl.when(ci == pl.num_programs(1) - 1)
    def _():
        sout_ref[0] = s_ref[...]


def _gdn(proj, small, conv_buf, conv_w, a_log, dt_bias, norm_g, s0, *, c, t_valid, gate_group, gate_off):
    b, t = proj.shape[:2]
    nh, dk = A_HEADS, A_DK
    hw = nh * dk
    return pl.pallas_call(
        functools.partial(_gdn_kernel, c=c, t_valid=t_valid, gate_off=gate_off),
        grid=(b, t // c),
        in_specs=[pl.BlockSpec((1, c, 3 * hw), lambda bi, ci: (bi, ci, 0)),
                  pl.BlockSpec((1, c, hw), lambda bi, ci: (bi, ci, 3)),
                  pl.BlockSpec((1, c, LANES), lambda bi, ci: (bi, ci, gate_group)),
                  pl.BlockSpec((1, A_CONV - 1, 3 * hw), lambda bi, ci: (bi, 0, 0)),
                  pl.BlockSpec((A_CONV, 3 * hw), lambda bi, ci: (0, 0)),
                  pl.BlockSpec((1, nh), lambda bi, ci: (0, 0)),
                  pl.BlockSpec((1, nh), lambda bi, ci: (0, 0)),
                  pl.BlockSpec((1, dk), lambda bi, ci: (0, 0)),
                  pl.BlockSpec((1, nh, dk, dk), lambda bi, ci: (bi, 0, 0, 0))],
        out_specs=[pl.BlockSpec((1, c, hw), lambda bi, ci: (bi, ci, 0)),
                   pl.BlockSpec((1, nh, dk, dk), lambda bi, ci: (bi, 0, 0, 0))],
        out_shape=[jax.ShapeDtypeStruct((b, t, hw), F32), jax.ShapeDtypeStruct((b, nh, dk, dk), F32)],
        scratch_shapes=[pltpu.VMEM((8 + c, 3 * hw), F32), pltpu.VMEM((nh, dk, dk), F32)],
        compiler_params=_cp(("arbitrary", "arbitrary"), 40),
        name="gdn",
    )(proj, proj, small, conv_buf, conv_w, a_log.reshape(1, nh), dt_bias.reshape(1, nh), norm_g.reshape(1, dk), s0)


def _idx_prep_kernel(x_ref, q_ref, k_ref):
    tm = x_ref.shape[0]
    lo_half = lax.broadcasted_iota(I32, (tm, LANES), 1) < 64

    def split(xg):
        hi = xg.astype(BF16).astype(F32)
        return hi, xg - hi

    for gq in range(IDX_HEADS // 2):
        hi, lo = split(x_ref[:, gq * LANES:(gq + 1) * LANES])
        hi_r = pltpu.roll(hi, 64, 1)
        lo_r = pltpu.roll(lo, 64, 1)
        base = 2 * gq * 256
        q_ref[:, base:base + 128] = jnp.where(lo_half, hi, hi_r).astype(BF16)
        q_ref[:, base + 128:base + 256] = jnp.where(lo_half, lo, lo_r).astype(BF16)
        q_ref[:, base + 256:base + 384] = jnp.where(lo_half, hi_r, hi).astype(BF16)
        q_ref[:, base + 384:base + 512] = jnp.where(lo_half, lo_r, lo).astype(BF16)
    hi, lo = split(x_ref[:, 4 * LANES:5 * LANES])
    kk = jnp.where(lo_half, hi, pltpu.roll(lo, 64, 1)).astype(BF16)
    k_ref[:, 0:128] = kk
    k_ref[:, 128:256] = kk


def _idx_prep(small_r, tm=512):
    t = small_r.shape[0]
    return pl.pallas_call(
        _idx_prep_kernel,
        grid=(t // tm,),
        in_specs=[pl.BlockSpec((tm, 5 * LANES), lambda i: (i, 0))],
        out_specs=[pl.BlockSpec((tm, IDX_HEADS * 256), lambda i: (i, 0)),
                   pl.BlockSpec((tm, 256), lambda i: (i, 0))],
        out_shape=[jax.ShapeDtypeStruct((t, IDX_HEADS * 256), BF16), jax.ShapeDtypeStruct((t, 256), BF16)],
        compiler_params=_cp(("arbitrary",)),
        name="idx_prep",
    )(small_r)


IDX_SCALE = IDX_HEADS ** -0.5 * IDX_HD ** -0.5
W_OFF = 64


def _idx_scores_kernel(q_ref, k_ref, w_ref, o_ref, *, tq, ts):
    i = pl.program_id(0)
    j = pl.program_id(1)
    live = j * ts <= i * tq + tq - 1

    @pl.when(live)
    def _():
        k = k_ref[...]
        wg = w_ref[...] * IDX_SCALE
        acc = jnp.zeros((tq, ts), F32)
        for h in range(IDX_HEADS):
            s = _dot(q_ref[:, h * 256:(h + 1) * 256], k, NT)
            acc = acc + wg[:, W_OFF + h:W_OFF + h + 1] * jnp.maximum(s, 0.0)
        row = i * tq + lax.broadcasted_iota(I32, (tq, ts), 0)
        col = j * ts + lax.broadcasted_iota(I32, (tq, ts), 1)
        o_ref[...] = jnp.where(col <= row, acc, -jnp.inf)

    @pl.when(jnp.logical_not(live))
    def _():
        o_ref[...] = jnp.full((tq, ts), -jnp.inf, F32)


def _idx_scores(qcat, kcat, small_r, *, tq=256, ts=512):
    t = qcat.shape[0]
    return pl.pallas_call(
        functools.partial(_idx_scores_kernel, tq=tq, ts=ts),
        grid=(t // tq, t // ts),
        in_specs=[pl.BlockSpec((tq, IDX_HEADS * 256), lambda i, j: (i, 0)),
                  pl.BlockSpec((ts, 256), lambda i, j: (j, 0)),
                  pl.BlockSpec((tq, LANES), lambda i, j: (i, 4))],
        out_specs=pl.BlockSpec((tq, ts), lambda i, j: (i, j)),
        out_shape=jax.ShapeDtypeStruct((t, t), F32),
        compiler_params=_cp(("arbitrary", "arbitrary")),
        name="idx_scores",
    )(qcat, kcat, small_r)


def _thr_kernel(s_ref, thr_ref, need_ref, tie_ref, key_ref, *, k, cw, causal):
    tr, s_cols = s_ref.shape
    nchunks = ((pl.program_id(0) + 1) * tr + cw - 1) // cw if causal else s_cols // cw
    kf = float(k)

    def fill(ci, carry):
        off = pl.multiple_of(ci * cw, cw)
        bits = pltpu.bitcast(s_ref[:, pl.ds(off, cw)] + 0.0, I32)
        key_ref[:, pl.ds(off, cw)] = jnp.where(bits < 0, bits ^ 0x7FFFFFFF, bits)
        return carry

    lax.fori_loop(0, nchunks, fill, 0)

    def count(cand, strict):
        def cbody(ci, acc):
            off = pl.multiple_of(ci * cw, cw)
            blk = key_ref[:, pl.ds(off, cw)]
            for t in range(cw // LANES):
                kk = blk[:, t * LANES:(t + 1) * LANES]
                acc = acc + jnp.where(kk > cand if strict else kk >= cand, 1.0, 0.0)
            return acc

        acc = lax.fori_loop(0, nchunks, cbody, jnp.zeros((tr, LANES), F32))
        return jnp.sum(acc, axis=-1, keepdims=True)

    cnt0 = count(0, False)
    p0 = jnp.where(cnt0 >= kf, 0, INT_MIN).astype(I32)
    visited = (nchunks * cw).astype(F32) if causal else float(s_cols)
    cnt_p0 = jnp.where(cnt0 >= kf, cnt0, visited)

    def cond(state):
        b, _, cnt_p = state
        return jnp.logical_and(b < 31, jnp.max(cnt_p) > kf)

    def body(state):
        b, p, cnt_p = state
        cand = p | jnp.left_shift(jnp.int32(1), 30 - b)
        cnt = count(cand, False)
        take = cnt >= kf
        return b + 1, jnp.where(take, cand, p), jnp.where(take, cnt, cnt_p)

    _, p, _ = lax.while_loop(cond, body, (jnp.int32(0), p0, cnt_p0))
    p = jnp.maximum(p, KEY_NEG_INF)
    thr_ref[...] = pltpu.bitcast(jnp.where(p < 0, p ^ 0x7FFFFFFF, p), F32)
    need_ref[...] = kf - count(p, True)
    tie_ref[...] = jnp.where(jnp.logical_and(count(p, False) > kf, p != KEY_NEG_INF), 1, 0).astype(I32)


def _topk_thr(scores, *, k, cw, causal, tr):
    r, s = scores.shape
    tr = min(tr, r)
    assert cw >= k and s % cw == 0
    return pl.pallas_call(
        functools.partial(_thr_kernel, k=k, cw=cw, causal=causal),
        grid=(r // tr,),
        in_specs=[pl.BlockSpec((tr, s), lambda i: (i, 0))],
        out_specs=[pl.BlockSpec((tr, 1), lambda i: (i, 0))] * 3,
        out_shape=[jax.ShapeDtypeStruct((r, 1), F32), jax.ShapeDtypeStruct((r, 1), F32),
                   jax.ShapeDtypeStruct((r, 1), I32)],
        scratch_shapes=[pltpu.VMEM((tr, s), I32)],
        compiler_params=_cp(("arbitrary",), 40),
        name="topk_thr",
    )(scores)


def _select_mask(sc, thr, need, causal, eq_before, tie):
    def visible(m, scc, c0):
        if isinstance(causal, str):
            return m
        return jnp.logical_and(m, scc > -jnp.inf if causal is None else causal[:, c0:c0 + scc.shape[1]])

    if not tie:
        return visible(sc >= thr, sc, 0), None
    n = sc.shape[1]
    cs = min(n, 512)
    upper = jnp.where(lax.broadcasted_iota(I32, (cs, cs), 0) < lax.broadcasted_iota(I32, (cs, cs), 1),
                      1.0, 0.0).astype(BF16)
    masks = []
    for c0 in range(0, n, cs):
        scc = sc[:, c0:c0 + cs]
        eq = scc == thr
        eqf = jnp.where(eq, 1.0, 0.0)
        rank = eq_before + _dot(eqf.astype(BF16), upper)
        sel = jnp.logical_or(scc > thr, jnp.logical_and(eq, rank < need))
        masks.append(visible(sel, scc, c0))
        eq_before = eq_before + jnp.sum(eqf, axis=-1, keepdims=True)
    if len(masks) == 1:
        return masks[0], eq_before
    return jnp.concatenate([jnp.where(m, 1.0, 0.0) for m in masks], axis=1) > 0.5, eq_before


def _flash_update(s, m_ref, acc_ref, r0, rows, vblk):
    ts = s.shape[1]
    m_old = m_ref[r0:r0 + rows, :]
    m_new = jnp.maximum(m_old, jnp.max(s, axis=-1, keepdims=True))
    p = jnp.exp(s - jnp.tile(m_new, (1, ts // LANES)))
    alpha = jnp.exp(m_old - m_new)
    acc_ref[r0:r0 + rows, :] = (jnp.tile(alpha, (1, acc_ref.shape[1] // LANES)) * acc_ref[r0:r0 + rows, :]
                                + _dot(p.astype(BF16), vblk))
    m_ref[r0:r0 + rows, :] = m_new


def _dsa_attn_kernel(tie_ref, q_ref, k_ref, v_ref, sc_ref, thr_ref, need_ref, o_ref,
                     kb_ref, vb_ref, qs_ref, m_ref, acc_ref, eq_ref, *, tq, ts):
    i = pl.program_id(1)
    hg = B_HEADS // B_KV
    d = B_HD

    @pl.when(i == 0)
    def _():
        kb_ref[...] = k_ref[...].astype(BF16)
        vb_ref[:, 0:d] = v_ref[...].astype(BF16)
        vb_ref[:, d:2 * d] = jnp.ones((vb_ref.shape[0], d), BF16)

    for h in range(hg):
        qs_ref[h * tq:(h + 1) * tq, :] = (q_ref[:, h * d:(h + 1) * d] * d ** -0.5).astype(BF16)
    m_ref[...] = jnp.full(m_ref.shape, NEG_BIG, F32)
    acc_ref[...] = jnp.zeros(acc_ref.shape, F32)
    eq_ref[...] = jnp.zeros(eq_ref.shape, F32)
    thr = thr_ref[...]
    need = need_ref[...]
    nblk = (i * tq + tq + ts - 1) // ts

    nfull = (i * tq) // ts

    def run(tie):
        def block(jb, diagonal):
            off = pl.multiple_of(jb * ts, ts)
            sc = sc_ref[:, pl.ds(off, ts)]
            if diagonal:
                row = i * tq + lax.broadcasted_iota(I32, (tq, ts), 0)
                col = off + lax.broadcasted_iota(I32, (tq, ts), 1)
                causal = col <= row
            else:
                causal = "all"
            mask, eq_new = _select_mask(sc, thr, need, causal, eq_ref[...], tie)
            if tie:
                eq_ref[...] = eq_new
            kblk = kb_ref[pl.ds(off, ts), :]
            vblk = vb_ref[pl.ds(off, ts), :]
            for h in range(hg):
                s = jnp.where(mask, _dot(qs_ref[h * tq:(h + 1) * tq, :], kblk, NT), NEG_BIG)
                _flash_update(s, m_ref, acc_ref, h * tq, tq, vblk)

        def full_body(jb, carry):
            block(jb, False)
            return carry

        def diag_body(jb, carry):
            block(jb, True)
            return carry

        lax.fori_loop(0, nfull, full_body, 0)
        lax.fori_loop(nfull, nblk, diag_body, 0)

    has_tie = tie_ref[i] > 0

    @pl.when(has_tie)
    def _():
        run(True)

    @pl.when(jnp.logical_not(has_tie))
    def _():
        run(False)

    for h in range(hg):
        o_ref[:, h * d:(h + 1) * d] = acc_ref[h * tq:(h + 1) * tq, 0:d] / acc_ref[h * tq:(h + 1) * tq, d:2 * d]


def _dsa_attn(tie_blk, q_r, k_r, v_src, v_col0, scores, thr, need, *, tq, ts=512):
    t = q_r.shape[0]
    hg = B_HEADS // B_KV
    once = dict(pipeline_mode=pl.Buffered(1))
    grid_spec = pltpu.PrefetchScalarGridSpec(
        num_scalar_prefetch=1,
        grid=(B_KV, t // tq),
        in_specs=[pl.BlockSpec((tq, hg * B_HD), lambda g, i, tie: (i, g)),
                  pl.BlockSpec((t, B_HD), lambda g, i, tie: (0, g), **once),
                  pl.BlockSpec((t, B_HD), lambda g, i, tie: (0, v_col0 + g), **once),
                  pl.BlockSpec((tq, t), lambda g, i, tie: (i, 0)),
                  pl.BlockSpec((tq, 1), lambda g, i, tie: (i, 0)),
                  pl.BlockSpec((tq, 1), lambda g, i, tie: (i, 0))],
        out_specs=pl.BlockSpec((tq, hg * B_HD), lambda g, i, tie: (i, g)),
        scratch_shapes=[pltpu.VMEM((t, B_HD), BF16), pltpu.VMEM((t, 2 * B_HD), BF16),
                        pltpu.VMEM((hg * tq, B_HD), BF16), pltpu.VMEM((hg * tq, LANES), F32),
                        pltpu.VMEM((hg * tq, 2 * B_HD), F32), pltpu.VMEM((tq, 1), F32)])
    return pl.pallas_call(
        functools.partial(_dsa_attn_kernel, tq=tq, ts=ts),
        grid_spec=grid_spec,
        out_shape=jax.ShapeDtypeStruct((t, B_HEADS * B_HD), F32),
        compiler_params=_cp(("arbitrary", "arbitrary"), 56),
        name="dsa_attn",
    )(tie_blk, q_r, k_r, v_src, scores, thr, need)


def _lambda_value(lam_ref, lam_init):
    lp = lam_ref[...]
    a = jnp.sum(lp[0:1, :] * lp[1:2, :], axis=-1, keepdims=True)
    b = jnp.sum(lp[2:3, :] * lp[3:4, :], axis=-1, keepdims=True)
    return jnp.exp(a) - jnp.exp(b) + lam_init


def _diff_attn_kernel(q_ref, k_ref, v_ref, lam_ref, ng_ref, o_ref, kb_ref, vb_ref, qs_ref, m_ref, acc_ref,
                      *, tq, ts, rsub, lam_init):
    i = pl.program_id(1)
    d2 = 2 * C_HD
    rows = 2 * tq

    @pl.when(i == 0)
    def _():
        kb_ref[...] = k_ref[...].astype(BF16)
        vb_ref[:, 0:d2] = v_ref[...].astype(BF16)
        vb_ref[:, d2:2 * d2] = jnp.ones((vb_ref.shape[0], d2), BF16)

    q = q_ref[...] * C_HD ** -0.5
    first = lax.broadcasted_iota(I32, (tq, d2), 1) < C_HD
    qs_ref[0:tq, :] = jnp.where(first, q, 0.0).astype(BF16)
    qs_ref[tq:rows, :] = jnp.where(first, 0.0, q).astype(BF16)
    m_ref[...] = jnp.full(m_ref.shape, NEG_BIG, F32)
    acc_ref[...] = jnp.zeros(acc_ref.shape, F32)

    def step(jb, diag_col0):
        off = pl.multiple_of(jb * ts, ts)
        kblk = kb_ref[pl.ds(off, ts), :]
        vblk = vb_ref[pl.ds(off, ts), :]
        for r0 in range(0, rows, rsub):
            q0 = r0 % tq
            if diag_col0 is not None and q0 + rsub - 1 < diag_col0:
                continue
            s = _dot(qs_ref[r0:r0 + rsub, :], kblk, NT)
            if diag_col0 is not None and q0 < diag_col0 + ts - 1:
                qrow = q0 + lax.broadcasted_iota(I32, (rsub, ts), 0)
                s = jnp.where(diag_col0 + lax.broadcasted_iota(I32, (rsub, ts), 1) <= qrow, s, NEG_BIG)
            _flash_update(s, m_ref, acc_ref, r0, rsub, vblk)

    def body(jb, carry):
        step(jb, None)
        return carry

    nfull = (i * tq) // ts
    lax.fori_loop(0, nfull, body, 0)
    for dj in range(tq // ts):
        step(nfull + dj, dj * ts)

    lam = _lambda_value(lam_ref, lam_init)
    o = (acc_ref[0:tq, 0:d2] / acc_ref[0:tq, d2:2 * d2]
         - lam * (acc_ref[tq:rows, 0:d2] / acc_ref[tq:rows, d2:2 * d2]))
    on = o * lax.rsqrt(jnp.mean(o * o, axis=-1, keepdims=True) + EPS) * ng_ref[...]
    o_ref[...] = on * (1.0 - lam_init)


def _diff_attn(q_r, k_r, v_src, v_col0, c_lambda, c_norm_g, *, lam_init, tq=2048, ts=1024, rsub=512):
    t = q_r.shape[0]
    tq = min(tq, t)
    d2 = 2 * C_HD
    once = dict(pipeline_mode=pl.Buffered(1))
    return pl.pallas_call(
        functools.partial(_diff_attn_kernel, tq=tq, ts=ts, rsub=rsub, lam_init=lam_init),
        grid=(C_HEADS, t // tq),
        in_specs=[pl.BlockSpec((tq, d2), lambda h, i: (i, h)),
                  pl.BlockSpec((t, d2), lambda h, i: (0, h), **once),
                  pl.BlockSpec((t, d2), lambda h, i: (0, v_col0 + h), **once),
                  pl.BlockSpec((4, C_HD), lambda h, i: (0, 0)),
                  pl.BlockSpec((1, d2), lambda h, i: (0, 0))],
        out_specs=pl.BlockSpec((tq, d2), lambda h, i: (i, h)),
        out_shape=jax.ShapeDtypeStruct((t, C_HEADS * d2), F32),
        scratch_shapes=[pltpu.VMEM((t, d2), BF16), pltpu.VMEM((t, 2 * d2), BF16), pltpu.VMEM((2 * tq, d2), BF16),
                        pltpu.VMEM((2 * tq, LANES), F32), pltpu.VMEM((2 * tq, 2 * d2), F32)],
        compiler_params=_cp(("arbitrary", "arbitrary"), 48),
        name="diff_attn",
    )(q_r, k_r, v_src, c_lambda, c_norm_g.reshape(1, d2))


TS_PAD = 8


def _page_specs(n_pg, blk, layer, n_prefetch):
    def spec(k):
        if n_prefetch == 1:
            return pl.BlockSpec((1, 1) + blk, lambda bi, p, pt: (layer, pt[bi, p * n_pg + k], 0, 0))
        return pl.BlockSpec((1, 1) + blk, lambda bi, p, pt, tie: (layer, pt[bi, p * n_pg + k], 0, 0))

    return [spec(k) for k in range(n_pg)]


def _idx_scores_s_kernel(pt_ref, q_ref, w_ref, *refs, n_pg, t_new):
    pools = refs[:n_pg]
    new_ref, o_ref, onew_ref = refs[n_pg:]
    q = q_ref[0]
    w = w_ref[0] * IDX_SCALE

    def weigh(s):
        acc = jnp.zeros((TS_PAD, s.shape[1]), F32)
        for h in range(IDX_HEADS):
            acc = acc + w[h * TS_PAD:(h + 1) * TS_PAD, :] * jnp.maximum(s[h * TS_PAD:(h + 1) * TS_PAD, :], 0.0)
        return acc

    for k in range(n_pg):
        o_ref[0, :, k * PAGE:(k + 1) * PAGE] = weigh(_dot3(q, pools[k][0, 0]))

    @pl.when(pl.program_id(1) == pl.num_programs(1) - 1)
    def _():
        tq = lax.broadcasted_iota(I32, (TS_PAD, PAGE), 0)
        tk = lax.broadcasted_iota(I32, (TS_PAD, PAGE), 1)
        ok = jnp.logical_and(tk <= tq, tk < t_new)
        onew_ref[0] = jnp.where(ok, weigh(_dot3(q, new_ref[0], NT)), -jnp.inf)


def _idx_scores_s(page_table, q_s, w_s, pool_ki_t, ki_new, *, layer, t_new, n_pg=8):
    b, n_pages = page_table.shape
    grid_spec = pltpu.PrefetchScalarGridSpec(
        num_scalar_prefetch=1,
        grid=(b, n_pages // n_pg),
        in_specs=[pl.BlockSpec((1, IDX_HEADS * TS_PAD, IDX_HD), lambda bi, p, pt: (bi, 0, 0)),
                  pl.BlockSpec((1, IDX_HEADS * TS_PAD, 1), lambda bi, p, pt: (bi, 0, 0))]
        + _page_specs(n_pg, (IDX_HD, PAGE), layer, 1)
        + [pl.BlockSpec((1, PAGE, IDX_HD), lambda bi, p, pt: (bi, 0, 0))],
        out_specs=[pl.BlockSpec((1, TS_PAD, n_pg * PAGE), lambda bi, p, pt: (bi, 0, p)),
                   pl.BlockSpec((1, TS_PAD, PAGE), lambda bi, p, pt: (bi, 0, 0))])
    return pl.pallas_call(
        functools.partial(_idx_scores_s_kernel, n_pg=n_pg, t_new=t_new),
        grid_spec=grid_spec,
        out_shape=[jax.ShapeDtypeStruct((b, TS_PAD, n_pages * PAGE), F32),
                   jax.ShapeDtypeStruct((b, TS_PAD, PAGE), F32)],
        compiler_params=_cp(("arbitrary", "arbitrary")),
        name="idx_scores_sample",
    )(page_table, q_s, w_s, *([pool_ki_t] * n_pg), ki_new)


def _dsa_attn_s_kernel(pt_ref, tie_ref, q_ref, *refs, n_pg):
    pk = refs[:n_pg]
    pv = refs[n_pg:2 * n_pg]
    nk_ref, nv_ref, sc_ref, scn_ref, thr_ref, need_ref, o_ref, m_ref, l_ref, acc_ref, eq_ref = refs[2 * n_pg:]
    bi = pl.program_id(0)
    p = pl.program_id(1)
    hg = B_HEADS // B_KV

    @pl.when(p == 0)
    def _():
        m_ref[...] = jnp.full(m_ref.shape, NEG_BIG, F32)
        l_ref[...] = jnp.zeros(l_ref.shape, F32)
        acc_ref[...] = jnp.zeros(acc_ref.shape, F32)
        eq_ref[...] = jnp.zeros(eq_ref.shape, F32)

    thr = thr_ref[0]
    need = need_ref[0]
    scale = B_HD ** -0.5

    def attend(sc, keys_of, vals_of, tie):
        mask8, eq_new = _select_mask(sc, thr, need, None, eq_ref[...], tie)
        if tie:
            eq_ref[...] = eq_new
        mask = jnp.concatenate([mask8] * hg, axis=0)
        for g in range(B_KV):
            s = jnp.where(mask, _dot3(q_ref[0, g] * scale, keys_of(g), NT), NEG_BIG)
            m_old = m_ref[g]
            m_new = jnp.maximum(m_old, jnp.max(s, axis=-1, keepdims=True))
            pr = jnp.where(mask, jnp.exp(s - m_new), 0.0)
            alpha = jnp.exp(m_old - m_new)
            l_ref[g] = alpha * l_ref[g] + jnp.sum(pr, axis=-1, keepdims=True)
            acc_ref[g] = alpha * acc_ref[g] + _dot3(pr, vals_of(g))
            m_ref[g] = m_new

    def paged(refs_):
        return lambda g: jnp.concatenate([r[0, 0, pl.ds(g, PAGE, stride=B_KV), :] for r in refs_], axis=0)

    def run(tie):
        attend(sc_ref[0], paged(pk), paged(pv), tie)

        @pl.when(p == pl.num_programs(1) - 1)
        def _():
            attend(scn_ref[0], lambda g: nk_ref[0, :, g * B_HD:(g + 1) * B_HD],
                   lambda g: nv_ref[0, :, g * B_HD:(g + 1) * B_HD], tie)

    has_tie = tie_ref[bi] > 0

    @pl.when(has_tie)
    def _():
        run(True)

    @pl.when(jnp.logical_not(has_tie))
    def _():
        run(False)

    @pl.when(p == pl.num_programs(1) - 1)
    def _():
        for g in range(B_KV):
            o_ref[0, g] = acc_ref[g] / l_ref[g]


def _dsa_attn_s(page_table, tie_b, q_s, pool_k2, pool_v2, k_new, v_new, scores, scores_new, thr, need,
                *, layer, n_pg=4):
    b, n_pages = page_table.shape
    hg = B_HEADS // B_KV
    rows = hg * TS_PAD
    kvw = B_KV * B_HD
    grid_spec = pltpu.PrefetchScalarGridSpec(
        num_scalar_prefetch=2,
        grid=(b, n_pages // n_pg),
        in_specs=[pl.BlockSpec((1, B_KV, rows, B_HD), lambda bi, p, pt, tie: (bi, 0, 0, 0))]
        + _page_specs(n_pg, (PAGE * B_KV, B_HD), layer, 2) + _page_specs(n_pg, (PAGE * B_KV, B_HD), layer, 2)
        + [pl.BlockSpec((1, PAGE, kvw), lambda bi, p, pt, tie: (bi, 0, 0)),
           pl.BlockSpec((1, PAGE, kvw), lambda bi, p, pt, tie: (bi, 0, 0)),
           pl.BlockSpec((1, TS_PAD, n_pg * PAGE), lambda bi, p, pt, tie: (bi, 0, p)),
           pl.BlockSpec((1, TS_PAD, PAGE), lambda bi, p, pt, tie: (bi, 0, 0)),
           pl.BlockSpec((1, TS_PAD, 1), lambda bi, p, pt, tie: (bi, 0, 0)),
           pl.BlockSpec((1, TS_PAD, 1), lambda bi, p, pt, tie: (bi, 0, 0))],
        out_specs=pl.BlockSpec((1, B_KV, rows, B_HD), lambda bi, p, pt, tie: (bi, 0, 0, 0)),
        scratch_shapes=[pltpu.VMEM((B_KV, rows, 1), F32), pltpu.VMEM((B_KV, rows, 1), F32),
                        pltpu.VMEM((B_KV, rows, B_HD), F32), pltpu.VMEM((TS_PAD, 1), F32)])
    return pl.pallas_call(
        functools.partial(_dsa_attn_s_kernel, n_pg=n_pg),
        grid_spec=grid_spec,
        out_shape=jax.ShapeDtypeStruct((b, B_KV, rows, B_HD), F32),
        compiler_params=_cp(("arbitrary", "arbitrary")),
        name="dsa_attn_sample",
    )(page_table, tie_b, q_s, *([pool_k2] * n_pg), *([pool_v2] * n_pg), k_new, v_new, scores, scores_new, thr, need)


def _diff_attn_s_kernel(pt_ref, wt_ref, *refs, n_pg, t_new, lam_init):
    pk = refs[:n_pg]
    pv = refs[n_pg:2 * n_pg]
    nk_ref, nv_ref, ex_ref, hm_ref, lam_ref, ng_ref, o_ref, m_ref, l_ref, acc_ref = refs[2 * n_pg:]
    p = pl.program_id(1)
    nrow = wt_ref.shape[1]
    rph = 2 * t_new
    dv = 2 * C_HD

    @pl.when(p == 0)
    def _():
        m_ref[...] = jnp.full(m_ref.shape, NEG_BIG, F32)
        l_ref[...] = jnp.zeros(l_ref.shape, F32)
        acc_ref[...] = jnp.zeros(acc_ref.shape, F32)

    wt = wt_ref[0]

    def attend(s, pv_of):
        m_old = m_ref[...]
        m_new = jnp.maximum(m_old, jnp.max(s, axis=-1, keepdims=True))
        pr = jnp.exp(s - m_new)
        alpha = jnp.exp(m_old - m_new)
        l_ref[...] = alpha * l_ref[...] + jnp.sum(pr, axis=-1, keepdims=True)
        acc_ref[...] = alpha * acc_ref[...] + pv_of(pr.astype(BF16))
        m_ref[...] = m_new

    def pv_paged(prb):
        out = jnp.zeros((nrow, dv), F32)
        for k in range(n_pg):
            spread = _dot(prb[:, k * PAGE:(k + 1) * PAGE], ex_ref[...]) * hm_ref[...]
            out = out + _dot(spread.astype(BF16), pv[k][0, 0].astype(BF16))
        return out

    s_past = _dot(wt, jnp.concatenate([r[0, 0] for r in pk], axis=1).astype(BF16))
    attend(s_past, pv_paged)

    @pl.when(p == pl.num_programs(1) - 1)
    def _():
        tq = lax.broadcasted_iota(I32, (nrow, TS_PAD), 0) & (t_new - 1)
        tk = lax.broadcasted_iota(I32, (nrow, TS_PAD), 1)
        s_new = jnp.where(tk <= tq, _dot(wt, nk_ref[0].astype(BF16), NT), NEG_BIG)
        attend(s_new, lambda prb: jnp.concatenate(
            [_dot(prb[h * rph:(h + 1) * rph, :], nv_ref[0, :, h * dv:(h + 1) * dv].astype(BF16))
             for h in range(C_HEADS)], axis=0))
        lam = _lambda_value(lam_ref, lam_init)
        a = acc_ref[...] / l_ref[...]
        for h in range(C_HEADS):
            r0 = h * rph
            o = a[r0:r0 + t_new, :] - lam * a[r0 + t_new:r0 + rph, :]
            on = o * lax.rsqrt(jnp.mean(o * o, axis=-1, keepdims=True) + EPS) * ng_ref[...]
            o_ref[0, :, h * dv:(h + 1) * dv] = on * (1.0 - lam_init)


def _diff_attn_s(page_table, wt, pool_kt, pool_v2, k_new, v_new, c_lambda, c_norm_g, *, layer, t_new, lam_init,
                 n_pg=4):
    b, n_pages = page_table.shape
    nrow, width = wt.shape[1:]
    dv = 2 * C_HD
    vrows = PAGE * C_HEADS
    vrow = jnp.arange(vrows, dtype=I32)
    expand = (vrow[None, :] // C_HEADS == jnp.arange(PAGE, dtype=I32)[:, None]).astype(BF16)
    own_head = (vrow[None, :] % C_HEADS == jnp.arange(nrow, dtype=I32)[:, None] // (2 * t_new)).astype(F32)
    grid_spec = pltpu.PrefetchScalarGridSpec(
        num_scalar_prefetch=1,
        grid=(b, n_pages // n_pg),
        in_specs=[pl.BlockSpec((1, nrow, width), lambda bi, p, pt: (bi, 0, 0))]
        + _page_specs(n_pg, (width, PAGE), layer, 1) + _page_specs(n_pg, (vrows, dv), layer, 1)
        + [pl.BlockSpec((1, TS_PAD, width), lambda bi, p, pt: (bi, 0, 0)),
           pl.BlockSpec((1, TS_PAD, width), lambda bi, p, pt: (bi, 0, 0)),
           pl.BlockSpec((PAGE, vrows), lambda bi, p, pt: (0, 0)),
           pl.BlockSpec((nrow, vrows), lambda bi, p, pt: (0, 0)),
           pl.BlockSpec((4, C_HD), lambda bi, p, pt: (0, 0)),
           pl.BlockSpec((1, dv), lambda bi, p, pt: (0, 0))],
        out_specs=pl.BlockSpec((1, t_new, width), lambda bi, p, pt: (bi, 0, 0)),
        scratch_shapes=[pltpu.VMEM((nrow, 1), F32), pltpu.VMEM((nrow, 1), F32), pltpu.VMEM((nrow, dv), F32)])
    return pl.pallas_call(
        functools.partial(_diff_attn_s_kernel, n_pg=n_pg, t_new=t_new, lam_init=lam_init),
        grid_spec=grid_spec,
        out_shape=jax.ShapeDtypeStruct((b, t_new, width), F32),
        compiler_params=_cp(("arbitrary", "arbitrary"), 56),
        name="diff_attn_sample",
    )(page_table, wt, *([pool_kt] * n_pg), *([pool_v2] * n_pg), k_new, v_new, expand, own_head, c_lambda,
      c_norm_g.reshape(1, dv))


def _split_ab_weights(w_in):
    n_a = 4 * A_HEADS * A_DK
    n_gate = 2 * A_HEADS
    n_b = (B_HEADS + 2 * B_KV) * B_HD
    n_idx = IDX_HEADS * IDX_HD + IDX_HD + IDX_HEADS
    big = jnp.concatenate([w_in[:, :n_a], w_in[:, n_a + n_gate:n_a + n_gate + n_b]], axis=1)
    pad = 5 * LANES - n_idx - n_gate
    small = jnp.concatenate([w_in[:, n_a + n_gate + n_b:], w_in[:, n_a:n_a + n_gate],
                             jnp.zeros((w_in.shape[0], pad), w_in.dtype)], axis=1)
    return big, small


PROMPT_TM = 2048
GATE_OFF = IDX_HD + IDX_HEADS
QB_COL0 = 4 * A_HEADS * A_DK // LANES
KB_COL0 = QB_COL0 + B_HEADS
VB_COL0 = KB_COL0 + B_KV


def _mlp(x, mods, norm_g, w1, w2, *, tm, hi, tn=512, tk=2048):
    sh2, sc2, g2 = mods
    hid = _norm_matmul(x, norm_g[2:3], sc2, sh2, w1, tm=tm, tn=tn, hi=hi, relu2=True,
                       out_dtype=F32 if hi else BF16, vmem_mb=56)
    if hi:
        return _matmul_norm_res(hid, w2, norm_g[3:4], g2, x, tm=tm, tk=512, hi=True)
    return _matmul_norm_res(hid, w2, norm_g[3:4], g2, x, tm=512, tk=tk, vmem_mb=56)


def _prompt_trunk(x, mods, P):
    t = x.shape[0]
    pos = jnp.arange(t, dtype=I32)
    tab128, tab64 = _rope_tables(pos)
    sh1, sc1, g1, sh2, sc2, g2 = mods[0]
    ng = P["norm_g"][0]
    big = _norm_matmul(x, ng[0:1], sc1, sh1, P["w_big"], tm=PROMPT_TM, tn=256, vmem_mb=56)
    small = _norm_matmul(x, ng[0:1], sc1, sh1, P["w_small"], tm=PROMPT_TM // 2, tn=5 * LANES, hi=True)
    small_r = _rope(small, tab64, col0=0, ngroups=5, n_kind0=4, r1=32, r2=96, tm=1024)
    qb_r = _rope(big, tab128, col0=QB_COL0, ngroups=B_HEADS, n_kind0=B_HEADS, r1=64, r2=None, tm=1024, gw=4)
    kb_r = _rope(big, tab128, col0=KB_COL0, ngroups=B_KV, n_kind0=B_KV, r1=64, r2=None, tm=1024)
    qcat, kcat = _idx_prep(small_r)
    scores = _idx_scores(qcat, kcat, small_r)
    thr, need, tie = _topk_thr(scores, k=min(TOPK, t // 4), cw=1024, causal=True, tr=128)
    tq = 256
    tie_blk = jnp.max(tie.reshape(t // tq, tq), axis=1)
    ob = _dsa_attn(tie_blk, qb_r, kb_r, big, VB_COL0, scores, thr, need, tq=tq)
    oa, s_new = _gdn(big[None], small_r[None], jnp.zeros((1, A_CONV - 1, 3 * A_HEADS * A_DK), F32), P["ab_conv_w"],
                     P["ab_A_log"], P["ab_dt_bias"], P["ab_norm_g"], jnp.zeros((1, A_HEADS, A_DK, A_DK), F32),
                     c=128, t_valid=128, gate_group=4, gate_off=GATE_OFF)
    mix = jnp.concatenate([oa[0], ob], axis=1)
    x = _matmul_norm_res(mix, P["ab_w_out_bf16"], ng[1:2], g1, x, tm=512, tk=2048, vmem_mb=56)
    x = _mlp(x, (sh2, sc2, g2), ng, (P["mlp_w1"], 0), (P["mlp_w2_bf16"], 0), tm=PROMPT_TM, hi=False)
    outs0 = (kb_r, big[:, VB_COL0 * LANES:], small_r[:, 4 * LANES:4 * LANES + IDX_HD], s_new,
             big[t - (A_CONV - 1):, :3 * A_HEADS * A_DK])
    sh1, sc1, g1, sh2, sc2, g2 = mods[1]
    ng = P["norm_g"][1]
    lam_init = 0.8 - 0.6 * math.exp(-0.3 * 1)
    proj = _norm_matmul(x, ng[0:1], sc1, sh1, P["c_w_in"], tm=PROMPT_TM, tn=256, vmem_mb=56)
    q_r = _rope(proj, tab64, col0=0, ngroups=C_HEADS, n_kind0=C_HEADS, r1=32, r2=96, tm=1024, gw=4)
    k_r = _rope(proj, tab64, col0=C_HEADS, ngroups=C_HEADS, n_kind0=C_HEADS, r1=32, r2=96, tm=1024, gw=4)
    o = _diff_attn(q_r, k_r, proj, 2 * C_HEADS, P["c_lambda"], P["c_norm_g"], lam_init=lam_init)
    x = _matmul_norm_res(o, P["c_w_out_bf16"], ng[1:2], g1, x, tm=512, tk=2048, vmem_mb=56)
    x = _mlp(x, (sh2, sc2, g2), ng, (P["mlp_w1"], 1), (P["mlp_w2_bf16"], 1), tm=PROMPT_TM, hi=False)
    outs1 = (k_r, proj[:, 4 * C_HEADS * C_HD:])
    return x, outs0, outs1


def _sample_trunk(x, mods, P, cache, past_len):
    b, ts_, d = x.shape
    m = b * ts_
    x = x.reshape(m, d)
    pos = past_len + jnp.arange(ts_, dtype=I32)
    tab128, tab64 = _rope_tables(pos)
    tab128 = jnp.tile(tab128, (b, 1))
    tab64 = jnp.tile(tab64, (b, 1))
    page_table = cache["page_table"]
    n_pages = page_table.shape[1]
    hw = A_HEADS * A_DK

    def rows(a):
        return jnp.repeat(a, ts_, axis=0)

    def pad_t(a, n):
        return jnp.pad(a, ((0, 0), (0, n - ts_), (0, 0)))

    sh1, sc1, g1, sh2, sc2, g2 = (rows(a) for a in mods[0])
    ng = P["norm_g"][0]
    big = _norm_matmul(x, ng[0:1], sc1, sh1, P["w_big"], tm=m, tn=512, hi=True)
    small = _norm_matmul(x, ng[0:1], sc1, sh1, P["w_small"], tm=m, tn=5 * LANES, hi=True)
    small_r = _rope(small, tab64, col0=0, ngroups=5, n_kind0=4, r1=32, r2=96, tm=m)
    qb_r = _rope(big, tab128, col0=QB_COL0, ngroups=B_HEADS, n_kind0=B_HEADS, r1=64, r2=None, tm=m)
    kb_r = _rope(big, tab128, col0=KB_COL0, ngroups=B_KV, n_kind0=B_KV, r1=64, r2=None, tm=m)
    vb = big[:, VB_COL0 * LANES:]
    sm3 = small_r.reshape(b, ts_, 5 * LANES)
    qi = sm3[:, :, :IDX_HEADS * IDX_HD].reshape(b, ts_, IDX_HEADS, IDX_HD)
    qi = jnp.pad(jnp.swapaxes(qi, 1, 2), ((0, 0), (0, 0), (0, TS_PAD - ts_), (0, 0)))
    qi = qi.reshape(b, IDX_HEADS * TS_PAD, IDX_HD)
    wi = sm3[:, :, 4 * LANES + W_OFF:4 * LANES + W_OFF + IDX_HEADS]
    wi = jnp.pad(jnp.swapaxes(wi, 1, 2), ((0, 0), (0, 0), (0, TS_PAD - ts_))).reshape(b, IDX_HEADS * TS_PAD, 1)
    ki_r = sm3[:, :, 4 * LANES:4 * LANES + IDX_HD]
    pool_ki_t = jnp.swapaxes(cache["ab_kidx"], 2, 3)
    sc_past, sc_new = _idx_scores_s(page_table, qi, wi, pool_ki_t, pad_t(ki_r, PAGE), layer=0, t_new=ts_,
                                    n_pg=min(32, n_pages))
    scores = jnp.concatenate([sc_past, sc_new], axis=2)
    ncols = scores.shape[2]
    k_sel = min(TOPK, (past_len + ts_) // 4)
    chunk = min(d for d in range(1, ncols // LANES + 1) if (ncols // LANES) % d == 0 and d * LANES >= k_sel) * LANES
    thr, need, tie = _topk_thr(scores.reshape(b * TS_PAD, ncols), k=k_sel, cw=chunk, causal=False, tr=64)
    tie_b = jnp.max(tie.reshape(b, TS_PAD)[:, :ts_], axis=1)
    hg = B_HEADS // B_KV
    q4 = qb_r.reshape(b, ts_, B_KV, hg, B_HD)
    q4 = jnp.pad(jnp.transpose(q4, (0, 2, 3, 1, 4)), ((0, 0), (0, 0), (0, 0), (0, TS_PAD - ts_), (0, 0)))
    q4 = q4.reshape(b, B_KV, hg * TS_PAD, B_HD)
    kvw = B_KV * B_HD
    pool_k2 = cache["ab_k"].reshape(cache["ab_k"].shape[:2] + (PAGE * B_KV, B_HD))
    pool_v2 = cache["ab_v"].reshape(cache["ab_v"].shape[:2] + (PAGE * B_KV, B_HD))
    ob = _dsa_attn_s(page_table, tie_b, q4, pool_k2, pool_v2, pad_t(kb_r.reshape(b, ts_, kvw), PAGE),
                     pad_t(vb.reshape(b, ts_, kvw), PAGE), sc_past, sc_new, thr.reshape(b, TS_PAD, 1),
                     need.reshape(b, TS_PAD, 1), layer=0, n_pg=min(32, n_pages))
    ob = ob.reshape(b, B_KV, hg, TS_PAD, B_HD)[:, :, :, :ts_]
    ob = jnp.transpose(ob, (0, 3, 1, 2, 4)).reshape(m, B_HEADS * B_HD)
    oa, s_new = _gdn(pad_t(big.reshape(b, ts_, -1), TS_PAD), pad_t(sm3, TS_PAD), cache["ab_conv"][0], P["ab_conv_w"],
                     P["ab_A_log"], P["ab_dt_bias"], P["ab_norm_g"], cache["ab_delta"][0],
                     c=TS_PAD, t_valid=ts_, gate_group=4, gate_off=GATE_OFF)
    mix = jnp.concatenate([oa[:, :ts_].reshape(m, hw), ob], axis=1)
    x = _matmul_norm_res(mix, P["ab_w_out"], ng[1:2], g1, x, tm=m, tk=512, hi=True)
    x = _mlp(x, (sh2, sc2, g2), ng, (P["mlp_w1"], 0), (P["mlp_w2"], 0), tm=m, hi=True)
    conv_in = jnp.concatenate([cache["ab_conv"][0], big.reshape(b, ts_, -1)[:, :, :3 * hw]], axis=1)
    outs0 = (kb_r, vb, ki_r, s_new, conv_in[:, ts_:])
    sh1, sc1, g1, sh2, sc2, g2 = (rows(a) for a in mods[1])
    ng = P["norm_g"][1]
    lam_init = 0.8 - 0.6 * math.exp(-0.3 * 1)
    proj = _norm_matmul(x, ng[0:1], sc1, sh1, P["c_w_in"], tm=m, tn=512, hi=True)
    qk_r = _rope(proj, tab64, col0=0, ngroups=2 * C_HEADS, n_kind0=2 * C_HEADS, r1=32, r2=96, tm=m)
    cw = 2 * C_HEADS * C_HD
    q3 = qk_r[:, :cw].reshape(b, ts_, cw) * C_HD ** -0.5
    k3 = qk_r[:, cw:].reshape(b, ts_, cw)
    v3 = proj[:, 2 * cw:].reshape(b, ts_, cw)
    lane_hp = jnp.arange(cw, dtype=I32) // C_HD
    want = (2 * jnp.arange(C_HEADS, dtype=I32)[:, None] + jnp.arange(2, dtype=I32)[None, :])
    sel = (lane_hp[None, None, :] == want[:, :, None]).astype(F32)
    wt = (q3[:, None, None, :, :] * sel[None, :, :, None, :]).reshape(b, 2 * C_HEADS * ts_, cw).astype(BF16)
    ck = cache["c_k"]
    pool_ckt = jnp.transpose(ck, (0, 1, 3, 4, 5, 2)).reshape(ck.shape[:2] + (cw, PAGE))
    pool_cv2 = cache["c_v"].reshape(cache["c_v"].shape[:2] + (PAGE * C_HEADS, 2 * C_HD))
    o = _diff_attn_s(page_table, wt, pool_ckt, pool_cv2, pad_t(k3, TS_PAD), pad_t(v3, TS_PAD), P["c_lambda"],
                     P["c_norm_g"], layer=0, t_new=ts_, lam_init=lam_init, n_pg=min(8, n_pages))
    x = _matmul_norm_res(o.reshape(m, cw), P["c_w_out"], ng[1:2], g1, x, tm=m, tk=512, hi=True)
    x = _mlp(x, (sh2, sc2, g2), ng, (P["mlp_w1"], 1), (P["mlp_w2"], 1), tm=m, hi=True)
    outs1 = (qk_r[:, cw:], proj[:, 2 * cw:])
    return x.reshape(b, ts_, d), outs0, outs1


def kernel(x_prompt, x_sample, cache_ab_k, cache_ab_v, cache_ab_kidx, state_ab_delta, state_ab_conv, cache_c_k,
           cache_c_v, page_table, c_prompt, c_sample, ada_w, ada_b, norm_g, mlp_w1, mlp_w2, ab_w_in, ab_w_out,
           ab_conv_w, ab_A_log, ab_dt_bias, ab_norm_g, c_w_in, c_w_out, c_lambda, c_norm_g):
    bp, t, d = x_prompt.shape
    bs, ts_, _ = x_sample.shape
    assert bp == 1
    past_len = page_table.shape[1] * PAGE
    w_big, w_small = _split_ab_weights(ab_w_in[0])
    P = {"norm_g": norm_g, "mlp_w1": mlp_w1, "mlp_w2": mlp_w2, "w_big": w_big, "w_small": w_small,
         "ab_w_out": ab_w_out[0], "ab_conv_w": ab_conv_w[0], "ab_A_log": ab_A_log[0], "ab_dt_bias": ab_dt_bias[0],
         "ab_norm_g": ab_norm_g[0], "c_w_in": c_w_in[0], "c_w_out": c_w_out[0], "c_lambda": c_lambda[0],
         "c_norm_g": c_norm_g[0],
         "mlp_w2_bf16": mlp_w2.astype(BF16), "ab_w_out_bf16": ab_w_out[0].astype(BF16),
         "c_w_out_bf16": c_w_out[0].astype(BF16)}
    n_seq = bp + bs
    mc = -(-n_seq // 8) * 8
    c_all = jnp.pad(jnp.concatenate([c_prompt, c_sample], axis=0), ((0, mc - n_seq), (0, 0)))
    mod = _ada(c_all, ada_w, ada_b)
    mods_p = [tuple(mod[i, 0:bp, n * d:(n + 1) * d] for n in range(6)) for i in range(2)]
    mods_s = [tuple(mod[i, bp:n_seq, n * d:(n + 1) * d] for n in range(6)) for i in range(2)]

    y_p, p0, p1 = _prompt_trunk(x_prompt[0], mods_p, P)
    cache = {"ab_k": cache_ab_k, "ab_v": cache_ab_v, "ab_kidx": cache_ab_kidx, "ab_delta": state_ab_delta,
             "ab_conv": state_ab_conv, "c_k": cache_c_k, "c_v": cache_c_v, "page_table": page_table}
    y_s, s0, s1 = _sample_trunk(x_sample, mods_s, P, cache, past_len)

    return (y_p[None], y_s,
            p0[0].reshape(1, 1, t, B_KV, B_HD), p0[1].reshape(1, 1, t, B_KV, B_HD), p0[2].reshape(1, 1, t, IDX_HD),
            p0[3][None], p0[4].reshape(1, 1, A_CONV - 1, -1),
            p1[0].reshape(1, 1, t, C_HEADS, 2, C_HD), p1[1].reshape(1, 1, t, C_HEADS, 2 * C_HD),
            s0[0].reshape(1, bs, ts_, B_KV, B_HD), s0[1].reshape(1, bs, ts_, B_KV, B_HD),
            s0[2].reshape(1, bs, ts_, IDX_HD), s0[3][None], s0[4][None],
            s1[0].reshape(1, bs, ts_, C_HEADS, 2, C_HD), s1[1].reshape(1, bs, ts_, C_HEADS, 2 * C_HD))
```
